```python
import jax, jax.numpy as jnp
from jax import lax
import numpy as np

D_MODEL = 2048
BATCH = 8
SEQ = 2048
DEPTH = 1

HEAD_DIM = 128
N_HEADS_TOTAL = D_MODEL // HEAD_DIM
N_FOX_HEADS = N_HEADS_TOTAL // 2
N_SWA_HEADS = N_HEADS_TOTAL - N_FOX_HEADS
N_SWA_KV_HEADS = max(1, N_SWA_HEADS // 4)
SWA_WINDOW = 128
Q_BLOCK = 128
D_FF = 4 * D_MODEL
ROPE_THETA = 10000.0
NORM_EPS = 1e-6
FOX_W = N_FOX_HEADS * HEAD_DIM
SWA_Q_W = N_SWA_HEADS * HEAD_DIM
SWA_KV_W = N_SWA_KV_HEADS * HEAD_DIM
MIX_W = FOX_W + SWA_Q_W
IN_SPLITS = [FOX_W, FOX_W, FOX_W, N_FOX_HEADS, SWA_Q_W, SWA_KV_W, SWA_KV_W]
IN_PROJ_W = sum(IN_SPLITS)
N_MOD = 6

kernel_name = "hymba_fox_swa_sink_hybrid"


def rmsnorm(x, g):
    xf = x.astype(jnp.float32)
    y = xf * lax.rsqrt(jnp.mean(xf * xf, axis=-1, keepdims=True) + NORM_EPS)
    return (y * g.astype(jnp.float32)).astype(x.dtype)


def rope(x, pos):
    d = x.shape[-1]
    half = d // 2
    inv_freq = 1.0 / (ROPE_THETA ** (jnp.arange(half, dtype=jnp.float32) * (2.0 / d)))
    ang = pos.astype(jnp.float32)[:, None] * inv_freq[None, :]
    cos = jnp.cos(ang)[None, :, None, :]
    sin = jnp.sin(ang)[None, :, None, :]
    xf = x.astype(jnp.float32)
    x1, x2 = xf[..., :half], xf[..., half:]
    out = jnp.concatenate([x1 * cos - x2 * sin, x2 * cos + x1 * sin], axis=-1)
    return out.astype(x.dtype)


def forgetting_attention(q, k, v, log_f):
    B, S, H, d = q.shape
    cum = jnp.cumsum(log_f, axis=1).transpose(0, 2, 1)
    scale = d ** -0.5
    tri = jnp.tril(jnp.ones((Q_BLOCK, Q_BLOCK), dtype=bool))
    outs = []
    for i in range(S // Q_BLOCK):
        q0 = i * Q_BLOCK
        end = q0 + Q_BLOCK
        s = jnp.einsum('bqhd,bkhd->bhqk', q[:, q0:end], k[:, :end],
                       preferred_element_type=jnp.float32) * scale
        s = s + cum[:, :, q0:end, None] - cum[:, :, None, :end]
        mask = jnp.concatenate([jnp.ones((Q_BLOCK, q0), dtype=bool), tri], axis=1)
        s = jnp.where(mask[None, None], s, -jnp.inf)
        p = jax.nn.softmax(s, axis=-1)
        outs.append(jnp.einsum('bhqk,bkhd->bqhd', p.astype(v.dtype), v[:, :end]))
    return jnp.concatenate(outs, axis=1)


def sliding_window_sink_attention(q, k, v, sinks):
    B, S, H, d = q.shape
    KVH = k.shape[2]
    G = H // KVH
    nb = S // Q_BLOCK
    scale = d ** -0.5
    pad = ((0, 0), (Q_BLOCK, 0), (0, 0), (0, 0))
    kp = jnp.pad(k, pad).reshape(B, nb + 1, Q_BLOCK, KVH, d)
    vp = jnp.pad(v, pad).reshape(B, nb + 1, Q_BLOCK, KVH, d)
    kb = jnp.concatenate([kp[:, :-1], kp[:, 1:]], axis=2)
    vb = jnp.concatenate([vp[:, :-1], vp[:, 1:]], axis=2)
    qb = q.reshape(B, nb, Q_BLOCK, KVH, G, d)
    s = jnp.einsum('bnqkgd,bnjkd->bnkgqj', qb, kb,
                   preferred_element_type=jnp.float32) * scale
    qi = jnp.arange(Q_BLOCK)[:, None]
    kj = jnp.arange(2 * Q_BLOCK)[None, :]
    diff = qi + Q_BLOCK - kj
    band = (diff >= 0) & (diff < SWA_WINDOW)
    key_idx = jnp.arange(nb)[:, None] * Q_BLOCK + jnp.arange(2 * Q_BLOCK)[None, :] - Q_BLOCK
    valid = key_idx >= 0
    mask = band[None, :, :] & valid[:, None, :]
    s = jnp.where(mask[None, :, None, None], s, -jnp.inf)
    sink = jnp.broadcast_to(sinks.astype(jnp.float32).reshape(KVH, G)[None, None, :, :, None, None],
                            s.shape[:-1] + (1,))
    p = jax.nn.softmax(jnp.concatenate([s, sink], axis=-1), axis=-1)[..., :-1]
    o = jnp.einsum('bnkgqj,bnjkd->bnqkgd', p.astype(v.dtype), vb)
    return o.reshape(B, S, H, d)


def setup_inputs(seed: int = 0) -> dict:
    key = jax.random.key(seed)
    ks = jax.random.split(key, 16)
    f32 = jnp.float32
    D = D_MODEL
    def nrm(k, shape, s):
        return jax.random.normal(k, shape, f32) * s
    return {
        "x": nrm(ks[0], (BATCH, SEQ, D), 1.0),
        "c": nrm(ks[1], (BATCH, D), 1.0),
        "w_mod": nrm(ks[2], (DEPTH, D, N_MOD * D), D ** -0.5),
        "b_mod": nrm(ks[3], (DEPTH, N_MOD * D), 0.02),
        "g_pre_mix": 1.0 + nrm(ks[4], (DEPTH, D), 0.02),
        "g_post_mix": 1.0 + nrm(ks[5], (DEPTH, D), 0.02),
        "w_in": nrm(ks[6], (DEPTH, D, IN_PROJ_W), D ** -0.5),
        "b_forget": jax.random.uniform(ks[7], (DEPTH, N_FOX_HEADS), f32, 1.0, 5.0),
        "swa_sinks": nrm(ks[8], (DEPTH, N_SWA_HEADS), 0.5),
        "w_out": nrm(ks[9], (DEPTH, MIX_W, D), MIX_W ** -0.5),
        "g_pre_mlp": 1.0 + nrm(ks[10], (DEPTH, D), 0.02),
        "g_post_mlp": 1.0 + nrm(ks[11], (DEPTH, D), 0.02),
        "w_up": nrm(ks[12], (DEPTH, D, D_FF), D ** -0.5),
        "w_down": nrm(ks[13], (DEPTH, D_FF, D), D_FF ** -0.5),
    }


def reference(x, c, w_mod, b_mod, g_pre_mix, g_post_mix, w_in, b_forget, swa_sinks,
              w_out, g_pre_mlp, g_post_mlp, w_up, w_down):
    B, S, D = x.shape
    pos = jnp.arange(S)
    split_idx = np.cumsum(IN_SPLITS)[:-1].tolist()
    cond = jax.nn.silu(c)
    for l in range(DEPTH):
        mod = cond @ w_mod[l] + b_mod[l]
        sh_a, sc_a, gt_a, sh_m, sc_m, gt_m = [m[:, None, :] for m in jnp.split(mod, N_MOD, axis=-1)]

        h = rmsnorm(x, g_pre_mix[l]) * (1.0 + sc_a) + sh_a
        proj = h @ w_in[l]
        fq, fk, fv, fg, sq, sk, sv = jnp.split(proj, split_idx, axis=-1)

        log_f = jax.nn.log_sigmoid(fg.astype(jnp.float32) + b_forget[l].astype(jnp.float32))
        fox = forgetting_attention(fq.reshape(B, S, N_FOX_HEADS, HEAD_DIM),
                                   fk.reshape(B, S, N_FOX_HEADS, HEAD_DIM),
                                   fv.reshape(B, S, N_FOX_HEADS, HEAD_DIM), log_f)

        sq = rope(sq.reshape(B, S, N_SWA_HEADS, HEAD_DIM), pos)
        sk = rope(sk.reshape(B, S, N_SWA_KV_HEADS, HEAD_DIM), pos)
        sv = sv.reshape(B, S, N_SWA_KV_HEADS, HEAD_DIM)
        swa = sliding_window_sink_attention(sq, sk, sv, swa_sinks[l])

        mix = jnp.concatenate([fox.reshape(B, S, FOX_W), swa.reshape(B, S, SWA_Q_W)], axis=-1) @ w_out[l]
        x = x + gt_a * rmsnorm(mix, g_post_mix[l])

        h = rmsnorm(x, g_pre_mlp[l]) * (1.0 + sc_m) + sh_m
        y = jnp.square(jax.nn.relu(h @ w_up[l])) @ w_down[l]
        x = x + gt_m * rmsnorm(y, g_post_mlp[l])
    return x
```

```python
import functools

import jax
import jax.numpy as jnp
from jax import lax
from jax.experimental import pallas as pl
from jax.experimental.pallas import tpu as pltpu

HEAD_DIM = 128
N_FOX_HEADS = 8
N_SWA_HEADS = 8
N_SWA_KV_HEADS = 2
SWA_GROUP = N_SWA_HEADS // N_SWA_KV_HEADS
SWA_WINDOW = 128
ROPE_THETA = 10000.0
NORM_EPS = 1e-6
N_MOD = 6
MASK_VALUE = -1e30

FOX_W = N_FOX_HEADS * HEAD_DIM
SWA_Q_W = N_SWA_HEADS * HEAD_DIM
SWA_KV_W = N_SWA_KV_HEADS * HEAD_DIM
COL_FQ = 0
COL_FK = FOX_W
COL_FV = 2 * FOX_W
COL_SQ = 3 * FOX_W
COL_SK = COL_SQ + SWA_Q_W
COL_SV = COL_SK + SWA_KV_W
PROJ_W = COL_SV + SWA_KV_W
FG_ROWS = 16

V7X_VMEM_LIMIT = 60 * 1024 * 1024

BF16 = jnp.bfloat16
F32 = jnp.float32
_NT = (((1,), (1,)), ((), ()))


def _rmsnorm(x, g):
    return x * lax.rsqrt(jnp.mean(x * x, axis=-1, keepdims=True) + NORM_EPS) * g


def _mod_kernel(c_ref, w_ref, b_ref, o_ref):
    c = c_ref[...]
    cond = c * jax.nn.sigmoid(c)
    o_ref[...] = jnp.dot(cond.astype(BF16), w_ref[...].astype(BF16),
                         preferred_element_type=F32) + b_ref[...]


def _mod(c, w, b, *, tn=1024):
    bsz, d = c.shape
    n = w.shape[1]
    return pl.pallas_call(
        _mod_kernel,
        grid=(n // tn,),
        in_specs=[pl.BlockSpec((bsz, d), lambda j: (0, 0)),
                  pl.BlockSpec((d, tn), lambda j: (0, j)),
                  pl.BlockSpec((1, tn), lambda j: (0, j))],
        out_specs=pl.BlockSpec((bsz, tn), lambda j: (0, j)),
        out_shape=jax.ShapeDtypeStruct((bsz, n), F32),
        compiler_params=pltpu.CompilerParams(
            dimension_semantics=("arbitrary",), vmem_limit_bytes=V7X_VMEM_LIMIT),
        name="mod",
    )(c, w, b.reshape(1, n))


def _inproj_kernel(x_ref, mod_ref, g_ref, w_ref, wfg_ref, bf_ref, cs_ref,
                   proj_ref, cum_ref, h_scr, carry_scr, *, tiles_per_seq):
    i = pl.program_id(0)
    j = pl.program_id(1)
    tm = x_ref.shape[0]

    @pl.when(j == 0)
    def _():
        y = _rmsnorm(x_ref[...], g_ref[...])
        h = y * (1.0 + mod_ref[0, 1:2, :]) + mod_ref[0, 0:1, :]
        hb = h.astype(BF16)
        h_scr[...] = hb
        fg = lax.dot_general(wfg_ref[...], hb, _NT, preferred_element_type=F32)
        z = fg[:N_FOX_HEADS] + bf_ref[...]
        logf = jnp.minimum(z, 0.0) - jnp.log1p(jnp.exp(-jnp.abs(z)))
        r = lax.broadcasted_iota(jnp.int32, (tm, tm), 0)
        cidx = lax.broadcasted_iota(jnp.int32, (tm, tm), 1)
        tri = (r <= cidx).astype(F32)
        cs = jnp.dot(logf, tri, precision=lax.Precision.HIGHEST,
                     preferred_element_type=F32)

        @pl.when(i % tiles_per_seq == 0)
        def _():
            carry_scr[...] = jnp.zeros_like(carry_scr)

        cs = cs + carry_scr[...]
        cum_ref[0] = cs
        carry_scr[...] = cs[:, tm - 1:tm]

    acc = jnp.dot(h_scr[...], w_ref[...], preferred_element_type=F32)
    proj_ref[...] = (acc * cs_ref[...]).astype(BF16)


def _inproj(x2, mod3, g, w_main, w_fg, b_forget, colscale, *, seq, tm=512, tn=1536):
    t, d = x2.shape
    n = w_main.shape[1]
    bsz = t // seq
    tiles_per_seq = seq // tm
    kern = functools.partial(_inproj_kernel, tiles_per_seq=tiles_per_seq)
    return pl.pallas_call(
        kern,
        grid=(t // tm, n // tn),
        in_specs=[
            pl.BlockSpec((tm, d), lambda i, j: (i, 0)),
            pl.BlockSpec((1, N_MOD, d), lambda i, j: (i // tiles_per_seq, 0, 0)),
            pl.BlockSpec((1, d), lambda i, j: (0, 0)),
            pl.BlockSpec((d, tn), lambda i, j: (0, j)),
            pl.BlockSpec((FG_ROWS, d), lambda i, j: (0, 0)),
            pl.BlockSpec((N_FOX_HEADS, 1), lambda i, j: (0, 0)),
            pl.BlockSpec((1, tn), lambda i, j: (0, j)),
        ],
        out_specs=[
            pl.BlockSpec((tm, tn), lambda i, j: (i, j)),
            pl.BlockSpec((1, N_FOX_HEADS, tm),
                         lambda i, j: (i // tiles_per_seq, 0, i % tiles_per_seq)),
        ],
        out_shape=[jax.ShapeDtypeStruct((t, n), BF16),
                   jax.ShapeDtypeStruct((bsz, N_FOX_HEADS, seq), F32)],
        scratch_shapes=[pltpu.VMEM((tm, d), BF16),
                        pltpu.VMEM((N_FOX_HEADS, 1), F32)],
        compiler_params=pltpu.CompilerParams(
            dimension_semantics=("arbitrary", "arbitrary"),
            vmem_limit_bytes=V7X_VMEM_LIMIT),
        name="in_proj",
    )(x2, mod3, g.reshape(1, d), w_main, w_fg, b_forget.reshape(N_FOX_HEADS, 1), colscale)


def _fox_kernel(q_ref, k_ref, v_ref, cum_ref, o_ref, m_scr, l_scr, acc_scr, *, tq):
    i = pl.program_id(2)
    q = q_ref[0]

    m_scr[...] = jnp.full_like(m_scr, MASK_VALUE)
    l_scr[...] = jnp.zeros_like(l_scr)
    acc_scr[...] = jnp.zeros_like(acc_scr)

    def step(j, masked):
        off = pl.multiple_of(j * tq, tq)
        ks = k_ref[0, pl.ds(off, tq), :]
        vs = v_ref[0, pl.ds(off, tq), :]
        ck = cum_ref[0, :, pl.ds(off, tq)]
        s = lax.dot_general(q, ks, _NT, preferred_element_type=F32) - ck
        if masked:
            r = lax.broadcasted_iota(jnp.int32, (tq, tq), 0)
            c = lax.broadcasted_iota(jnp.int32, (tq, tq), 1)
            s = jnp.where(r >= c, s, MASK_VALUE)
        m_prev = m_scr[...]
        m_new = jnp.maximum(m_prev, jnp.max(s, axis=-1, keepdims=True))
        alpha = jnp.exp(m_prev - m_new)
        p = jnp.exp(s - m_new)
        l_scr[...] = alpha * l_scr[...] + jnp.sum(p, axis=-1, keepdims=True)
        acc_scr[...] = alpha * acc_scr[...] + jnp.dot(
            p.astype(BF16), vs, preferred_element_type=F32)
        m_scr[...] = m_new

    def body(j, carry):
        step(j, masked=False)
        return carry

    lax.fori_loop(0, i, body, 0)
    step(i, masked=True)
    o_ref[0] = (acc_scr[...] / l_scr[...]).astype(o_ref.dtype)


def _fox(proj3, cum3, *, tq=256):
    bsz, seq, _ = proj3.shape
    hd = HEAD_DIM
    qb, kb, vb = COL_FQ // hd, COL_FK // hd, COL_FV // hd
    kern = functools.partial(_fox_kernel, tq=tq)
    return pl.pallas_call(
        kern,
        grid=(bsz, N_FOX_HEADS, seq // tq),
        in_specs=[
            pl.BlockSpec((1, tq, hd), lambda b, h, i: (b, i, qb + h)),
            pl.BlockSpec((1, seq, hd), lambda b, h, i: (b, 0, kb + h)),
            pl.BlockSpec((1, seq, hd), lambda b, h, i: (b, 0, vb + h)),
            pl.BlockSpec((1, 1, seq), lambda b, h, i: (b * N_FOX_HEADS + h, 0, 0)),
        ],
        out_specs=pl.BlockSpec((1, tq, hd), lambda b, h, i: (b, i, h)),
        out_shape=jax.ShapeDtypeStruct((bsz, seq, FOX_W), BF16),
        scratch_shapes=[pltpu.VMEM((tq, 1), F32), pltpu.VMEM((tq, 1), F32),
                        pltpu.VMEM((tq, hd), F32)],
        compiler_params=pltpu.CompilerParams(
            dimension_semantics=("arbitrary", "arbitrary", "arbitrary"),
            vmem_limit_bytes=V7X_VMEM_LIMIT),
        name="fox",
    )(proj3, proj3, proj3, cum3)


def _rope(x, cos, sin_signed):
    return x * cos + pltpu.roll(x, HEAD_DIM // 2, 1) * sin_signed


def _swa_kernel(q_ref, kc_ref, kp_ref, vc_ref, vp_ref, cosc_ref, sinc_ref,
                cosp_ref, sinp_ref, sink_ref, o_ref):
    n = pl.program_id(2)
    g = pl.program_id(1)
    w = SWA_WINDOW
    hd = HEAD_DIM
    cos_c, sin_c = cosc_ref[...], sinc_ref[...]
    qall = q_ref[0].astype(F32)
    qs = [_rope(qall[:, r * hd:(r + 1) * hd], cos_c, sin_c) for r in range(SWA_GROUP)]
    qcat = jnp.concatenate(qs, axis=0).astype(BF16)
    k_cur = _rope(kc_ref[0].astype(F32), cos_c, sin_c)
    k_prev = _rope(kp_ref[0].astype(F32), cosp_ref[...], sinp_ref[...])
    kcat = jnp.concatenate([k_prev, k_cur], axis=0).astype(BF16)
    vcat = jnp.concatenate([vp_ref[0], vc_ref[0]], axis=0)

    s = lax.dot_general(qcat, kcat, _NT, preferred_element_type=F32)
    qi = lax.broadcasted_iota(jnp.int32, (SWA_GROUP * w, 2 * w), 0) & (w - 1)
    kj = lax.broadcasted_iota(jnp.int32, (SWA_GROUP * w, 2 * w), 1)
    diff = qi + w - kj
    mask = (diff >= 0) & (diff < SWA_WINDOW) & ((kj >= w) | (n > 0))
    s = jnp.where(mask, s, MASK_VALUE)

    sink = jnp.concatenate(
        [jnp.full((w, 1), sink_ref[g * SWA_GROUP + r], F32) for r in range(SWA_GROUP)],
        axis=0)

    m = jnp.maximum(jnp.max(s, axis=-1, keepdims=True), sink)
    p = jnp.exp(s - m)
    denom = jnp.sum(p, axis=-1, keepdims=True) + jnp.exp(sink - m)
    o = jnp.dot(p.astype(BF16), vcat, preferred_element_type=F32) / denom
    o_ref[0] = jnp.concatenate(
        [o[r * w:(r + 1) * w, :] for r in range(SWA_GROUP)], axis=1).astype(o_ref.dtype)


def _swa(proj3, cos, sin_signed, sinks):
    bsz, seq, _ = proj3.shape
    w, hd = SWA_WINDOW, HEAD_DIM
    gw = SWA_GROUP * hd
    qb, kb, vb = COL_SQ // gw, COL_SK // hd, COL_SV // hd
    prev = lambda n: jnp.maximum(n - 1, 0)
    return pl.pallas_call(
        _swa_kernel,
        grid=(bsz, N_SWA_KV_HEADS, seq // w),
        in_specs=[
            pl.BlockSpec((1, w, gw), lambda b, g, n: (b, n, qb + g)),
            pl.BlockSpec((1, w, hd), lambda b, g, n: (b, n, kb + g)),
            pl.BlockSpec((1, w, hd), lambda b, g, n: (b, prev(n), kb + g)),
            pl.BlockSpec((1, w, hd), lambda b, g, n: (b, n, vb + g)),
            pl.BlockSpec((1, w, hd), lambda b, g, n: (b, prev(n), vb + g)),
            pl.BlockSpec((w, hd), lambda b, g, n: (n, 0)),
            pl.BlockSpec((w, hd), lambda b, g, n: (n, 0)),
            pl.BlockSpec((w, hd), lambda b, g, n: (prev(n), 0)),
            pl.BlockSpec((w, hd), lambda b, g, n: (prev(n), 0)),
            pl.BlockSpec(memory_space=pltpu.SMEM),
        ],
        out_specs=pl.BlockSpec((1, w, gw), lambda b, g, n: (b, n, g)),
        out_shape=jax.ShapeDtypeStruct((bsz, seq, SWA_Q_W), BF16),
        compiler_params=pltpu.CompilerParams(
            dimension_semantics=("arbitrary", "arbitrary", "arbitrary"),
            vmem_limit_bytes=V7X_VMEM_LIMIT),
        name="swa",
    )(proj3, proj3, proj3, proj3, proj3, cos, sin_signed, cos, sin_signed, sinks)


def _outproj_kernel(fox_ref, swa_ref, x_ref, mod_ref, g_ref, wf_ref, ws_ref, o_ref):
    mix = jnp.dot(fox_ref[...], wf_ref[...], preferred_element_type=F32)
    mix = mix + jnp.dot(swa_ref[...], ws_ref[...], preferred_element_type=F32)
    o_ref[...] = x_ref[...] + mod_ref[0, 2:3, :] * _rmsnorm(mix, g_ref[...])


def _outproj(fox2, swa2, x2, mod3, g, w_fox, w_swa, *, seq, tm=512):
    t, d = x2.shape
    tiles_per_seq = seq // tm
    return pl.pallas_call(
        _outproj_kernel,
        grid=(t // tm,),
        in_specs=[
            pl.BlockSpec((tm, FOX_W), lambda i: (i, 0)),
            pl.BlockSpec((tm, SWA_Q_W), lambda i: (i, 0)),
            pl.BlockSpec((tm, d), lambda i: (i, 0)),
            pl.BlockSpec((1, N_MOD, d), lambda i: (i // tiles_per_seq, 0, 0)),
            pl.BlockSpec((1, d), lambda i: (0, 0)),
            pl.BlockSpec((FOX_W, d), lambda i: (0, 0)),
            pl.BlockSpec((SWA_Q_W, d), lambda i: (0, 0)),
        ],
        out_specs=pl.BlockSpec((tm, d), lambda i: (i, 0)),
        out_shape=jax.ShapeDtypeStruct((t, d), F32),
        compiler_params=pltpu.CompilerParams(
            dimension_semantics=("arbitrary",), vmem_limit_bytes=V7X_VMEM_LIMIT),
        name="out_proj",
    )(fox2, swa2, x2, mod3, g.reshape(1, d), w_fox, w_swa)


def _mlp_kernel(x_ref, mod_ref, gpre_ref, gpost_ref, wu_ref, wd_ref, o_ref, h_scr, acc_scr):
    j = pl.program_id(1)

    @pl.when(j == 0)
    def _():
        y = _rmsnorm(x_ref[...], gpre_ref[...])
        h = y * (1.0 + mod_ref[0, 4:5, :]) + mod_ref[0, 3:4, :]
        h_scr[...] = h.astype(BF16)
        acc_scr[...] = jnp.zeros_like(acc_scr)

    u = jnp.dot(h_scr[...], wu_ref[...], preferred_element_type=F32)
    a = jnp.square(jnp.maximum(u, 0.0)).astype(BF16)
    acc_scr[...] += jnp.dot(a, wd_ref[...], preferred_element_type=F32)

    @pl.when(j == pl.num_programs(1) - 1)
    def _():
        o_ref[...] = x_ref[...] + mod_ref[0, 5:6, :] * _rmsnorm(acc_scr[...], gpost_ref[...])


def _mlp(x2, mod3, g_pre, g_post, w_up, w_down, *, seq, tm=512, tf=512):
    t, d = x2.shape
    ff = w_up.shape[1]
    tiles_per_seq = seq // tm
    return pl.pallas_call(
        _mlp_kernel,
        grid=(t // tm, ff // tf),
        in_specs=[
            pl.BlockSpec((tm, d), lambda i, j: (i, 0)),
            pl.BlockSpec((1, N_MOD, d), lambda i, j: (i // tiles_per_seq, 0, 0)),
            pl.BlockSpec((1, d), lambda i, j: (0, 0)),
            pl.BlockSpec((1, d), lambda i, j: (0, 0)),
            pl.BlockSpec((d, tf), lambda i, j: (0, j)),
            pl.BlockSpec((tf, d), lambda i, j: (j, 0)),
        ],
        out_specs=pl.BlockSpec((tm, d), lambda i, j: (i, 0)),
        out_shape=jax.ShapeDtypeStruct((t, d), F32),
        scratch_shapes=[pltpu.VMEM((tm, d), BF16), pltpu.VMEM((tm, d), F32)],
        compiler_params=pltpu.CompilerParams(
            dimension_semantics=("arbitrary", "arbitrary"),
            vmem_limit_bytes=V7X_VMEM_LIMIT),
        name="mlp",
    )(x2, mod3, g_pre.reshape(1, d), g_post.reshape(1, d), w_up, w_down)


def _rope_tables(seq):
    half = HEAD_DIM // 2
    inv_freq = 1.0 / (ROPE_THETA ** (jnp.arange(half, dtype=F32) * (2.0 / HEAD_DIM)))
    ang = jnp.arange(seq).astype(F32)[:, None] * inv_freq[None, :]
    cos, sin = jnp.cos(ang), jnp.sin(ang)
    return (jnp.concatenate([cos, cos], axis=-1),
            jnp.concatenate([-sin, sin], axis=-1))


def kernel(x, c, w_mod, b_mod, g_pre_mix, g_post_mix, w_in, b_forget, swa_sinks,
           w_out, g_pre_mlp, g_post_mlp, w_up, w_down):
    bsz, seq, d = x.shape
    depth = w_mod.shape[0]
    t = bsz * seq
    cos, sin_signed = _rope_tables(seq)
    scale = HEAD_DIM ** -0.5
    colscale = jnp.ones((1, PROJ_W), F32)
    colscale = colscale.at[:, COL_FQ:COL_FQ + FOX_W].set(scale)
    colscale = colscale.at[:, COL_SQ:COL_SQ + SWA_Q_W].set(scale)
    fg0 = 3 * FOX_W

    x2 = x.reshape(t, d)
    for l in range(depth):
        w_main = jnp.concatenate(
            [w_in[l][:, :fg0], w_in[l][:, fg0 + N_FOX_HEADS:]], axis=1).astype(BF16)
        w_fg = jnp.zeros((FG_ROWS, d), BF16).at[:N_FOX_HEADS].set(
            w_in[l][:, fg0:fg0 + N_FOX_HEADS].T.astype(BF16))
        w_o = w_out[l].astype(BF16)
        w_u = w_up[l].astype(BF16)
        w_d = w_down[l].astype(BF16)

        mod3 = _mod(c, w_mod[l], b_mod[l]).reshape(bsz, N_MOD, d)
        proj, cum = _inproj(x2, mod3, g_pre_mix[l], w_main, w_fg, b_forget[l], colscale,
                            seq=seq)
        proj3 = proj.reshape(bsz, seq, PROJ_W)
        fox = _fox(proj3, cum.reshape(bsz * N_FOX_HEADS, 1, seq))
        swa = _swa(proj3, cos, sin_signed, swa_sinks[l])
        x2 = _outproj(fox.reshape(t, FOX_W), swa.reshape(t, SWA_Q_W), x2, mod3,
                      g_post_mix[l], w_o[:FOX_W], w_o[FOX_W:], seq=seq)
        x2 = _mlp(x2, mod3, g_pre_mlp[l], g_post_mlp[l], w_u, w_d, seq=seq)
    return x2.reshape(bsz, seq, d)
```

```python
import functools
import math

import jax
import jax.numpy as jnp
from jax import lax
from jax.experimental import pallas as pl
from jax.experimental.pallas import tpu as pltpu

HEAD_DIM = 128
N_FOX_HEADS = 8
N_SWA_HEADS = 8
N_SWA_KV_HEADS = 2
SWA_GROUP = N_SWA_HEADS // N_SWA_KV_HEADS
SWA_WINDOW = 128
ROPE_THETA = 10000.0
NORM_EPS = 1e-6
N_MOD = 6
MASK_VALUE = -1e30
LOG2E = math.log2(math.e)
Q_SCALE = HEAD_DIM ** -0.5 * LOG2E

FOX_W = N_FOX_HEADS * HEAD_DIM
SWA_Q_W = N_SWA_HEADS * HEAD_DIM
SWA_KV_W = N_SWA_KV_HEADS * HEAD_DIM
TOK_W = FOX_W + SWA_KV_W
ROW_FQ = 0
ROW_FV = FOX_W
ROW_SQ = 2 * FOX_W
ROW_SV = ROW_SQ + SWA_Q_W
FEAT_W = ROW_SV + SWA_KV_W
CUM_SPLIT = 3
LANES = 128

V7X_VMEM_LIMIT = 60 * 1024 * 1024

BF16 = jnp.bfloat16
F32 = jnp.float32
_NT = (((1,), (1,)), ((), ()))


def _rmsnorm(x, g):
    return x * lax.rsqrt(jnp.mean(x * x, axis=-1, keepdims=True) + NORM_EPS) * g


def _resident(shape, index_map):
    return pl.BlockSpec(shape, index_map, pipeline_mode=pl.Buffered(1))


def _mod_kernel(c_ref, w_ref, b_ref, o_ref):
    c = c_ref[...]
    cond = c * jax.nn.sigmoid(c)
    o_ref[...] = jnp.dot(cond.astype(BF16), w_ref[...].astype(BF16),
                         preferred_element_type=F32) + b_ref[...]


def _mod(c, w, b, *, tn=1024):
    bsz, d = c.shape
    n = w.shape[1]
    return pl.pallas_call(
        _mod_kernel,
        grid=(n // tn,),
        in_specs=[pl.BlockSpec((bsz, d), lambda j: (0, 0)),
                  pl.BlockSpec((d, tn), lambda j: (0, j)),
                  pl.BlockSpec((1, tn), lambda j: (0, j))],
        out_specs=pl.BlockSpec((bsz, tn), lambda j: (0, j)),
        out_shape=jax.ShapeDtypeStruct((bsz, n), F32),
        compiler_params=pltpu.CompilerParams(
            dimension_semantics=("arbitrary",), vmem_limit_bytes=V7X_VMEM_LIMIT),
        name="mod",
    )(c, w, b.reshape(1, n))


def _inproj_kernel(x_ref, mod_ref, g_ref, wtok_ref, wfeat_ref, wfg_ref, bfg_ref, phase_ref,
                   tri_ref, cos_ref, sin_ref, cost_ref, sint_ref,
                   tok_ref, aug_ref, feat_ref, h_scr, carry_scr, *, tiles_per_seq):
    i = pl.program_id(0)
    hd = HEAD_DIM
    half = hd // 2

    y = _rmsnorm(x_ref[...], g_ref[...])
    h = y * (1.0 + mod_ref[0, 1:2, :]) + mod_ref[0, 0:1, :]
    h_scr[...] = h.astype(BF16)

    kt = jnp.dot(h_scr[...], wtok_ref[...], preferred_element_type=F32)
    tok_ref[:, :FOX_W] = kt[:, :FOX_W].astype(BF16)
    cos, sin_signed = cos_ref[...], sin_ref[...]
    for r in range(N_SWA_KV_HEADS):
        blk = kt[:, FOX_W + r * hd:FOX_W + (r + 1) * hd]
        tok_ref[:, FOX_W + r * hd:FOX_W + (r + 1) * hd] = (
            blk * cos + pltpu.roll(blk, half, 1) * sin_signed).astype(BF16)

    z = jnp.dot(h_scr[...], wfg_ref[...], preferred_element_type=F32) + bfg_ref[...]
    logf = jnp.minimum(z, 0.0) - jnp.log1p(jnp.exp(-jnp.abs(z)))
    p0 = logf.astype(BF16)
    r0 = logf - p0.astype(F32)
    p1 = r0.astype(BF16)
    p2 = (r0 - p1.astype(F32)).astype(BF16)
    cs = jnp.dot(tri_ref[...], jnp.concatenate([p0, p1, p2], axis=1),
                 preferred_element_type=F32)

    @pl.when(i % tiles_per_seq == 0)
    def _():
        carry_scr[...] = jnp.zeros_like(carry_scr)

    cum = cs[:, :LANES] + cs[:, LANES:2 * LANES] + cs[:, 2 * LANES:] + carry_scr[...]
    carry_scr[...] = cum[cum.shape[0] - 1:, :]
    negc = cum * (-LOG2E)
    hi = negc.astype(BF16).astype(F32)
    rem = negc - hi
    mid = rem.astype(BF16).astype(F32)
    lo = rem - mid
    ph = phase_ref[...]
    aug_ref[...] = jnp.where(ph == 0, hi, jnp.where(ph == 1, mid, lo)).astype(BF16)

    def feat(lo_row, hi_row):
        return lax.dot_general(wfeat_ref[lo_row:hi_row, :], h_scr[...], _NT,
                               preferred_element_type=F32)

    feat_ref[ROW_FQ:ROW_FQ + FOX_W, :] = (feat(ROW_FQ, ROW_FQ + FOX_W) * Q_SCALE).astype(BF16)
    feat_ref[ROW_FV:ROW_FV + FOX_W, :] = feat(ROW_FV, ROW_FV + FOX_W).astype(BF16)
    sq = feat(ROW_SQ, ROW_SQ + SWA_Q_W)
    cos_t, sin_t = cost_ref[...], sint_ref[...]
    for hh in range(N_SWA_HEADS):
        x1 = sq[hh * hd:hh * hd + half, :]
        x2 = sq[hh * hd + half:(hh + 1) * hd, :]
        base = ROW_SQ + hh * hd
        feat_ref[base:base + half, :] = ((x1 * cos_t - x2 * sin_t) * Q_SCALE).astype(BF16)
        feat_ref[base + half:base + hd, :] = ((x2 * cos_t + x1 * sin_t) * Q_SCALE).astype(BF16)
    feat_ref[ROW_SV:, :] = feat(ROW_SV, FEAT_W).astype(BF16)


def _inproj(x2, mod3, g, w_tok, w_feat_t, w_fg, b_fg, phase, tri, cos, sin_signed,
            cos_t, sin_t, *, seq, tm):
    t, d = x2.shape
    tiles_per_seq = seq // tm
    half = HEAD_DIM // 2
    kern = functools.partial(_inproj_kernel, tiles_per_seq=tiles_per_seq)
    const = lambda i: (0, 0)
    pos = lambda i: (i % tiles_per_seq, 0)
    pos_t = lambda i: (0, i % tiles_per_seq)
    return pl.pallas_call(
        kern,
        grid=(t // tm,),
        in_specs=[
            pl.BlockSpec((tm, d), lambda i: (i, 0)),
            pl.BlockSpec((1, N_MOD, d), lambda i: (i // tiles_per_seq, 0, 0)),
            _resident((1, d), const),
            _resident((d, TOK_W), const),
            _resident((FEAT_W, d), const),
            _resident((d, LANES), const),
            _resident((1, LANES), const),
            _resident((1, LANES), const),
            _resident((tm, tm), const),
            pl.BlockSpec((tm, HEAD_DIM), pos),
            pl.BlockSpec((tm, HEAD_DIM), pos),
            pl.BlockSpec((half, tm), pos_t),
            pl.BlockSpec((half, tm), pos_t),
        ],
        out_specs=[
            pl.BlockSpec((tm, TOK_W), lambda i: (i, 0)),
            pl.BlockSpec((tm, LANES), lambda i: (i, 0)),
            pl.BlockSpec((FEAT_W, tm), lambda i: (0, i)),
        ],
        out_shape=[jax.ShapeDtypeStruct((t, TOK_W), BF16),
                   jax.ShapeDtypeStruct((t, LANES), BF16),
                   jax.ShapeDtypeStruct((FEAT_W, t), BF16)],
        scratch_shapes=[pltpu.VMEM((tm, d), BF16), pltpu.VMEM((1, LANES), F32)],
        compiler_params=pltpu.CompilerParams(
            dimension_semantics=("arbitrary",), vmem_limit_bytes=V7X_VMEM_LIMIT),
        name="in_proj",
    )(x2, mod3, g.reshape(1, d), w_tok, w_feat_t, w_fg, b_fg, phase, tri, cos, sin_signed,
      cos_t, sin_t)


def _fox_kernel(qt_ref, k_ref, aug_ref, vt_ref, o_ref, *, blk):
    hidx = pl.program_id(1)
    seq = k_ref.shape[1]
    nblk = seq // blk
    r = lax.broadcasted_iota(jnp.int32, (LANES, blk), 0)
    sel = (r >= CUM_SPLIT * hidx) & (r < CUM_SPLIT * (hidx + 1))
    onehot_t = jnp.where(sel, 1.0, 0.0).astype(BF16)
    kk = lax.broadcasted_iota(jnp.int32, (blk, blk), 0)
    qq = lax.broadcasted_iota(jnp.int32, (blk, blk), 1)
    causal = kk <= qq

    def scores(i):
        nk = (i + 1) * blk
        q_aug = jnp.concatenate([qt_ref[:, i * blk:nk], onehot_t], axis=0)
        k_aug = jnp.concatenate([k_ref[0, :nk, :], aug_ref[0, :nk, :]], axis=1)
        return jnp.dot(k_aug, q_aug, preferred_element_type=F32)

    s_next = scores(0)
    for i in range(nblk):
        qs = slice(i * blk, (i + 1) * blk)
        nk = (i + 1) * blk
        s = s_next
        if i + 1 < nblk:
            s_next = scores(i + 1)
        s_diag = jnp.where(causal, s[nk - blk:, :], MASK_VALUE)
        s = s_diag if i == 0 else jnp.concatenate([s[:nk - blk, :], s_diag], axis=0)
        m = jnp.max(s, axis=0, keepdims=True)
        p = jnp.exp2(s - m)
        l = jnp.sum(p, axis=0, keepdims=True)
        acc = jnp.dot(vt_ref[:, :nk], p.astype(BF16), preferred_element_type=F32)
        o_ref[0, qs, :] = (acc * (1.0 / l)).T.astype(o_ref.dtype)


def _fox(feat_t, tok3, aug3, *, blk=256):
    bsz, seq, _ = tok3.shape
    hd = HEAD_DIM
    qb, vb = ROW_FQ // hd, ROW_FV // hd
    kern = functools.partial(_fox_kernel, blk=blk)
    return pl.pallas_call(
        kern,
        grid=(bsz, N_FOX_HEADS),
        in_specs=[
            pl.BlockSpec((hd, seq), lambda b, h: (qb + h, b)),
            pl.BlockSpec((1, seq, hd), lambda b, h: (b, 0, h)),
            pl.BlockSpec((1, seq, LANES), lambda b, h: (b, 0, 0)),
            pl.BlockSpec((hd, seq), lambda b, h: (vb + h, b)),
        ],
        out_specs=pl.BlockSpec((1, seq, hd), lambda b, h: (b, 0, h)),
        out_shape=jax.ShapeDtypeStruct((bsz, seq, FOX_W), BF16),
        compiler_params=pltpu.CompilerParams(
            dimension_semantics=("arbitrary", "arbitrary"),
            vmem_limit_bytes=V7X_VMEM_LIMIT),
        name="fox",
    )(feat_t, tok3, aug3, feat_t)


def _swa_kernel(qt_ref, k_ref, vt_ref, sink_ref, o_ref):
    g = pl.program_id(1)
    w, hd = SWA_WINDOW, HEAD_DIM
    seq = k_ref.shape[1]
    gl = SWA_GROUP * w
    sink = jnp.concatenate(
        [jnp.full((1, w), sink_ref[g * SWA_GROUP + r] * LOG2E, F32) for r in range(SWA_GROUP)],
        axis=1)
    kk = lax.broadcasted_iota(jnp.int32, (2 * w, gl), 0)
    qq = lax.broadcasted_iota(jnp.int32, (2 * w, gl), 1) & (w - 1)
    band = (kk > qq) & (kk - w <= qq)
    band_first = band[w:, :]

    for n in range(seq // w):
        q_t = jnp.concatenate(
            [qt_ref[r * hd:(r + 1) * hd, n * w:(n + 1) * w] for r in range(SWA_GROUP)],
            axis=1)
        ks = slice(max(n - 1, 0) * w, (n + 1) * w)
        s = jnp.dot(k_ref[0, ks, :], q_t, preferred_element_type=F32)
        s = jnp.where(band if n > 0 else band_first, s, MASK_VALUE)
        m = jnp.maximum(jnp.max(s, axis=0, keepdims=True), sink)
        p = jnp.exp2(s - m)
        denom = jnp.sum(p, axis=0, keepdims=True) + jnp.exp2(sink - m)
        o_t = jnp.dot(vt_ref[:, ks], p.astype(BF16), preferred_element_type=F32)
        o_t = o_t * (1.0 / denom)
        for r in range(SWA_GROUP):
            o_ref[0, n * w:(n + 1) * w, r * hd:(r + 1) * hd] = (
                o_t[:, r * w:(r + 1) * w].T.astype(o_ref.dtype))


def _swa(feat_t, tok3, sinks):
    bsz, seq, _ = tok3.shape
    hd = HEAD_DIM
    gw = SWA_GROUP * hd
    qb, kb, vb = ROW_SQ // gw, FOX_W // hd, ROW_SV // hd
    return pl.pallas_call(
        _swa_kernel,
        grid=(bsz, N_SWA_KV_HEADS),
        in_specs=[
            pl.BlockSpec((gw, seq), lambda b, g: (qb + g, b)),
            pl.BlockSpec((1, seq, hd), lambda b, g: (b, 0, kb + g)),
            pl.BlockSpec((hd, seq), lambda b, g: (vb + g, b)),
            pl.BlockSpec(memory_space=pltpu.SMEM),
        ],
        out_specs=pl.BlockSpec((1, seq, gw), lambda b, g: (b, 0, g)),
        out_shape=jax.ShapeDtypeStruct((bsz, seq, SWA_Q_W), BF16),
        compiler_params=pltpu.CompilerParams(
            dimension_semantics=("arbitrary", "arbitrary"),
            vmem_limit_bytes=V7X_VMEM_LIMIT),
        name="swa",
    )(feat_t, tok3, feat_t, sinks)


def _outproj_kernel(fox_ref, swa_ref, x_ref, mod_ref, g_ref, wf_ref, ws_ref, o_ref):
    mix = jnp.dot(fox_ref[...], wf_ref[...], preferred_element_type=F32)
    mix = mix + jnp.dot(swa_ref[...], ws_ref[...], preferred_element_type=F32)
    o_ref[...] = x_ref[...] + mod_ref[0, 2:3, :] * _rmsnorm(mix, g_ref[...])


def _outproj(fox2, swa2, x2, mod3, g, w_fox, w_swa, *, seq, tm=512):
    t, d = x2.shape
    tiles_per_seq = seq // tm
    return pl.pallas_call(
        _outproj_kernel,
        grid=(t // tm,),
        in_specs=[
            pl.BlockSpec((tm, FOX_W), lambda i: (i, 0)),
            pl.BlockSpec((tm, SWA_Q_W), lambda i: (i, 0)),
            pl.BlockSpec((tm, d), lambda i: (i, 0)),
            pl.BlockSpec((1, N_MOD, d), lambda i: (i // tiles_per_seq, 0, 0)),
            pl.BlockSpec((1, d), lambda i: (0, 0)),
            pl.BlockSpec((FOX_W, d), lambda i: (0, 0)),
            pl.BlockSpec((SWA_Q_W, d), lambda i: (0, 0)),
        ],
        out_specs=pl.BlockSpec((tm, d), lambda i: (i, 0)),
        out_shape=jax.ShapeDtypeStruct((t, d), F32),
        compiler_params=pltpu.CompilerParams(
            dimension_semantics=("arbitrary",), vmem_limit_bytes=V7X_VMEM_LIMIT),
        name="out_proj",
    )(fox2, swa2, x2, mod3, g.reshape(1, d), w_fox, w_swa)


def _mlp_kernel(x_ref, mod_ref, gpre_ref, gpost_ref, wu_ref, wd_ref, o_ref, h_scr, acc_scr):
    j = pl.program_id(1)

    @pl.when(j == 0)
    def _():
        y = _rmsnorm(x_ref[...], gpre_ref[...])
        h = y * (1.0 + mod_ref[0, 4:5, :]) + mod_ref[0, 3:4, :]
        h_scr[...] = h.astype(BF16)
        acc_scr[...] = jnp.zeros_like(acc_scr)

    u = jnp.dot(h_scr[...], wu_ref[...], preferred_element_type=F32)
    a = jnp.square(jnp.maximum(u, 0.0)).astype(BF16)
    acc_scr[...] += jnp.dot(a, wd_ref[...], preferred_element_type=F32)

    @pl.when(j == pl.num_programs(1) - 1)
    def _():
        o_ref[...] = x_ref[...] + mod_ref[0, 5:6, :] * _rmsnorm(acc_scr[...], gpost_ref[...])


def _mlp(x2, mod3, g_pre, g_post, w_up, w_down, *, seq, tm=512, tf=512):
    t, d = x2.shape
    ff = w_up.shape[1]
    tiles_per_seq = seq // tm
    return pl.pallas_call(
        _mlp_kernel,
        grid=(t // tm, ff // tf),
        in_specs=[
            pl.BlockSpec((tm, d), lambda i, j: (i, 0)),
            pl.BlockSpec((1, N_MOD, d), lambda i, j: (i // tiles_per_seq, 0, 0)),
            pl.BlockSpec((1, d), lambda i, j: (0, 0)),
            pl.BlockSpec((1, d), lambda i, j: (0, 0)),
            pl.BlockSpec((d, tf), lambda i, j: (0, j)),
            pl.BlockSpec((tf, d), lambda i, j: (j, 0)),
        ],
        out_specs=pl.BlockSpec((tm, d), lambda i, j: (i, 0)),
        out_shape=jax.ShapeDtypeStruct((t, d), F32),
        scratch_shapes=[pltpu.VMEM((tm, d), BF16), pltpu.VMEM((tm, d), F32)],
        compiler_params=pltpu.CompilerParams(
            dimension_semantics=("arbitrary", "arbitrary"),
            vmem_limit_bytes=V7X_VMEM_LIMIT),
        name="mlp",
    )(x2, mod3, g_pre.reshape(1, d), g_post.reshape(1, d), w_up, w_down)


def _rope_tables(seq):
    half = HEAD_DIM // 2
    inv_freq = 1.0 / (ROPE_THETA ** (jnp.arange(half, dtype=F32) * (2.0 / HEAD_DIM)))
    ang = jnp.arange(seq).astype(F32)[:, None] * inv_freq[None, :]
    cos, sin = jnp.cos(ang), jnp.sin(ang)
    return (jnp.concatenate([cos, cos], axis=-1), jnp.concatenate([-sin, sin], axis=-1),
            cos.T, sin.T)


def kernel(x, c, w_mod, b_mod, g_pre_mix, g_post_mix, w_in, b_forget, swa_sinks,
           w_out, g_pre_mlp, g_post_mlp, w_up, w_down):
    bsz, seq, d = x.shape
    depth = w_mod.shape[0]
    t = bsz * seq
    tm_in = 512
    cos, sin_signed, cos_t, sin_t = _rope_tables(seq)
    phase = (jnp.arange(LANES, dtype=jnp.int32) % CUM_SPLIT).reshape(1, LANES)
    tri = jnp.tril(jnp.ones((tm_in, tm_in), BF16))
    o_fk, o_fv, o_fg = FOX_W, 2 * FOX_W, 3 * FOX_W
    o_sq = o_fg + N_FOX_HEADS
    o_sk = o_sq + SWA_Q_W
    o_sv = o_sk + SWA_KV_W
    n_rep = CUM_SPLIT * N_FOX_HEADS

    x2 = x.reshape(t, d)
    for l in range(depth):
        wi = w_in[l]
        w_tok = jnp.concatenate([wi[:, o_fk:o_fv], wi[:, o_sk:o_sv]], axis=1).astype(BF16)
        w_feat_t = jnp.concatenate(
            [wi[:, :o_fk], wi[:, o_fv:o_fg], wi[:, o_sq:o_sk], wi[:, o_sv:]],
            axis=1).T.astype(BF16)
        w_fg = jnp.zeros((d, LANES), F32).at[:, :n_rep].set(
            jnp.repeat(wi[:, o_fg:o_sq], CUM_SPLIT, axis=1)).astype(BF16)
        b_fg = jnp.zeros((1, LANES), F32).at[0, :n_rep].set(
            jnp.repeat(b_forget[l].astype(F32), CUM_SPLIT))
        w_o = w_out[l].astype(BF16)
        w_u = w_up[l].astype(BF16)
        w_d = w_down[l].astype(BF16)

        mod3 = _mod(c, w_mod[l], b_mod[l]).reshape(bsz, N_MOD, d)
        tok, aug, feat_t = _inproj(x2, mod3, g_pre_mix[l], w_tok, w_feat_t, w_fg, b_fg, phase,
                                   tri, cos, sin_signed, cos_t, sin_t, seq=seq, tm=tm_in)
        tok3 = tok.reshape(bsz, seq, TOK_W)
        fox = _fox(feat_t, tok3, aug.reshape(bsz, seq, LANES))
        swa = _swa(feat_t, tok3, swa_sinks[l])
        x2 = _outproj(fox.reshape(t, FOX_W), swa.reshape(t, SWA_Q_W), x2, mod3,
                      g_post_mix[l], w_o[:FOX_W], w_o[FOX_W:], seq=seq)
        x2 = _mlp(x2, mod3, g_pre_mlp[l], g_post_mlp[l], w_u, w_d, seq=seq)
    return x2.reshape(bsz, seq, d)
```

```python
import functools
import math

import jax
import jax.numpy as jnp
from jax import lax
from jax.experimental import pallas as pl
from jax.experimental.pallas import tpu as pltpu

HEAD_DIM = 128
N_FOX_HEADS = 8
N_SWA_HEADS = 8
N_SWA_KV_HEADS = 2
SWA_GROUP = N_SWA_HEADS // N_SWA_KV_HEADS
SWA_WINDOW = 128
ROPE_THETA = 10000.0
NORM_EPS = 1e-6
N_MOD = 6
MASK_VALUE = -1e30
LOG2E = math.log2(math.e)
Q_SCALE = HEAD_DIM ** -0.5 * LOG2E

FOX_W = N_FOX_HEADS * HEAD_DIM
SWA_Q_W = N_SWA_HEADS * HEAD_DIM
SWA_KV_W = N_SWA_KV_HEADS * HEAD_DIM
TOK_W = FOX_W + SWA_KV_W
ROW_FQ = 0
ROW_FV = FOX_W
ROW_SQ = 2 * FOX_W
ROW_SV = ROW_SQ + SWA_Q_W
FEAT_W = ROW_SV + SWA_KV_W
CUM_SPLIT = 3
LANES = 128

V7X_VMEM_LIMIT = 60 * 1024 * 1024

BF16 = jnp.bfloat16
F32 = jnp.float32
_NT = (((1,), (1,)), ((), ()))


NORM_ROWS = 16


def _inv_rms(x):
    return lax.rsqrt(jnp.mean(x * x, axis=-1, keepdims=True) + NORM_EPS)


def _norm_modulate(x_ref, gain_row, shift_row, out_ref):
    for c in range(x_ref.shape[0] // NORM_ROWS):
        rows = slice(c * NORM_ROWS, (c + 1) * NORM_ROWS)
        x = x_ref[rows, :]
        out_ref[rows, :] = ((x * _inv_rms(x)) * gain_row + shift_row).astype(out_ref.dtype)


def _norm_gate_residual(y_ref, x_ref, gate_row):
    for c in range(x_ref.shape[0] // NORM_ROWS):
        rows = slice(c * NORM_ROWS, (c + 1) * NORM_ROWS)
        y = y_ref[rows, :]
        y_ref[rows, :] = x_ref[rows, :] + (y * _inv_rms(y)) * gate_row


def _resident(shape, index_map):
    return pl.BlockSpec(shape, index_map, pipeline_mode=pl.Buffered(1))


def _mod_kernel(c_ref, w_ref, b_ref, o_ref):
    c = c_ref[...]
    cond = c * jax.nn.sigmoid(c)
    o_ref[...] = jnp.dot(cond.astype(BF16), w_ref[...].astype(BF16),
                         preferred_element_type=F32) + b_ref[...]


def _mod(c, w, b, *, tn=1024):
    bsz, d = c.shape
    n = w.shape[1]
    return pl.pallas_call(
        _mod_kernel,
        grid=(n // tn,),
        in_specs=[pl.BlockSpec((bsz, d), lambda j: (0, 0)),
                  pl.BlockSpec((d, tn), lambda j: (0, j)),
                  pl.BlockSpec((1, tn), lambda j: (0, j))],
        out_specs=pl.BlockSpec((bsz, tn), lambda j: (0, j)),
        out_shape=jax.ShapeDtypeStruct((bsz, n), F32),
        compiler_params=pltpu.CompilerParams(
            dimension_semantics=("arbitrary",), vmem_limit_bytes=V7X_VMEM_LIMIT),
        name="mod",
    )(c, w, b.reshape(1, n))


def _inproj_kernel(x_ref, mod_ref, g_ref, wtok_ref, wfeat_ref, wfg_ref, bfg_ref, phase_ref,
                   tri_ref, cos_ref, sin_ref, cost_ref, sint_ref,
                   tok_ref, aug_ref, feat_ref, h_scr, carry_scr, *, tiles_per_seq):
    i = pl.program_id(0)
    hd = HEAD_DIM
    half = hd // 2

    _norm_modulate(x_ref, g_ref[...] * (1.0 + mod_ref[0, 1:2, :]), mod_ref[0, 0:1, :], h_scr)

    kt = jnp.dot(h_scr[...], wtok_ref[...], preferred_element_type=F32)
    tok_ref[:, :FOX_W] = kt[:, :FOX_W].astype(BF16)
    cos, sin_signed = cos_ref[...], sin_ref[...]
    for r in range(N_SWA_KV_HEADS):
        blk = kt[:, FOX_W + r * hd:FOX_W + (r + 1) * hd]
        tok_ref[:, FOX_W + r * hd:FOX_W + (r + 1) * hd] = (
            blk * cos + pltpu.roll(blk, half, 1) * sin_signed).astype(BF16)

    z = jnp.dot(h_scr[...], wfg_ref[...], preferred_element_type=F32) + bfg_ref[...]
    logf = jnp.minimum(z, 0.0) - jnp.log1p(jnp.exp(-jnp.abs(z)))
    p0 = logf.astype(BF16)
    r0 = logf - p0.astype(F32)
    p1 = r0.astype(BF16)
    p2 = (r0 - p1.astype(F32)).astype(BF16)
    cs = jnp.dot(tri_ref[...], jnp.concatenate([p0, p1, p2], axis=1),
                 preferred_element_type=F32)

    @pl.when(i % tiles_per_seq == 0)
    def _():
        carry_scr[...] = jnp.zeros_like(carry_scr)

    cum = cs[:, :LANES] + cs[:, LANES:2 * LANES] + cs[:, 2 * LANES:] + carry_scr[...]
    carry_scr[...] = cum[cum.shape[0] - 1:, :]
    negc = cum * (-LOG2E)
    hi = negc.astype(BF16).astype(F32)
    rem = negc - hi
    mid = rem.astype(BF16).astype(F32)
    lo = rem - mid
    ph = phase_ref[...]
    aug_ref[...] = jnp.where(ph == 0, hi, jnp.where(ph == 1, mid, lo)).astype(BF16)

    def feat(lo_row, hi_row):
        return lax.dot_general(wfeat_ref[lo_row:hi_row, :], h_scr[...], _NT,
                               preferred_element_type=F32)

    feat_ref[ROW_FQ:ROW_FQ + FOX_W, :] = (feat(ROW_FQ, ROW_FQ + FOX_W) * Q_SCALE).astype(BF16)
    feat_ref[ROW_FV:ROW_FV + FOX_W, :] = feat(ROW_FV, ROW_FV + FOX_W).astype(BF16)
    sq = feat(ROW_SQ, ROW_SQ + SWA_Q_W)
    cos_t, sin_t = cost_ref[...], sint_ref[...]
    for hh in range(N_SWA_HEADS):
        x1 = sq[hh * hd:hh * hd + half, :]
        x2 = sq[hh * hd + half:(hh + 1) * hd, :]
        base = ROW_SQ + hh * hd
        feat_ref[base:base + half, :] = ((x1 * cos_t - x2 * sin_t) * Q_SCALE).astype(BF16)
        feat_ref[base + half:base + hd, :] = ((x2 * cos_t + x1 * sin_t) * Q_SCALE).astype(BF16)
    feat_ref[ROW_SV:, :] = feat(ROW_SV, FEAT_W).astype(BF16)


def _inproj(x2, mod3, g, w_tok, w_feat_t, w_fg, b_fg, phase, tri, cos, sin_signed,
            cos_t, sin_t, *, seq, tm):
    t, d = x2.shape
    tiles_per_seq = seq // tm
    half = HEAD_DIM // 2
    kern = functools.partial(_inproj_kernel, tiles_per_seq=tiles_per_seq)
    const = lambda i: (0, 0)
    pos = lambda i: (i % tiles_per_seq, 0)
    pos_t = lambda i: (0, i % tiles_per_seq)
    return pl.pallas_call(
        kern,
        grid=(t // tm,),
        in_specs=[
            pl.BlockSpec((tm, d), lambda i: (i, 0)),
            pl.BlockSpec((1, N_MOD, d), lambda i: (i // tiles_per_seq, 0, 0)),
            _resident((1, d), const),
            _resident((d, TOK_W), const),
            _resident((FEAT_W, d), const),
            _resident((d, LANES), const),
            _resident((1, LANES), const),
            _resident((1, LANES), const),
            _resident((tm, tm), const),
            pl.BlockSpec((tm, HEAD_DIM), pos),
            pl.BlockSpec((tm, HEAD_DIM), pos),
            pl.BlockSpec((half, tm), pos_t),
            pl.BlockSpec((half, tm), pos_t),
        ],
        out_specs=[
            pl.BlockSpec((tm, TOK_W), lambda i: (i, 0)),
            pl.BlockSpec((tm, LANES), lambda i: (i, 0)),
            pl.BlockSpec((FEAT_W, tm), lambda i: (0, i)),
        ],
        out_shape=[jax.ShapeDtypeStruct((t, TOK_W), BF16),
                   jax.ShapeDtypeStruct((t, LANES), BF16),
                   jax.ShapeDtypeStruct((FEAT_W, t), BF16)],
        scratch_shapes=[pltpu.VMEM((tm, d), BF16), pltpu.VMEM((1, LANES), F32)],
        compiler_params=pltpu.CompilerParams(
            dimension_semantics=("arbitrary",), vmem_limit_bytes=V7X_VMEM_LIMIT),
        name="in_proj",
    )(x2, mod3, g.reshape(1, d), w_tok, w_feat_t, w_fg, b_fg, phase, tri, cos, sin_signed,
      cos_t, sin_t)


def _fox_kernel(qt_ref, k_ref, aug_ref, vt_ref, o_ref, *, blk):
    hidx = pl.program_id(1)
    seq = k_ref.shape[1]
    nblk = seq // blk
    r = lax.broadcasted_iota(jnp.int32, (LANES, blk), 0)
    sel = (r >= CUM_SPLIT * hidx) & (r < CUM_SPLIT * (hidx + 1))
    onehot_t = jnp.where(sel, 1.0, 0.0).astype(BF16)
    kk = lax.broadcasted_iota(jnp.int32, (blk, blk), 0)
    qq = lax.broadcasted_iota(jnp.int32, (blk, blk), 1)
    causal = kk <= qq

    def scores(i):
        nk = (i + 1) * blk
        q_aug = jnp.concatenate([qt_ref[:, i * blk:nk], onehot_t], axis=0)
        k_aug = jnp.concatenate([k_ref[0, :nk, :], aug_ref[0, :nk, :]], axis=1)
        return jnp.dot(k_aug, q_aug, preferred_element_type=F32)

    s_next = scores(0)
    for i in range(nblk):
        qs = slice(i * blk, (i + 1) * blk)
        nk = (i + 1) * blk
        s = s_next
        if i + 1 < nblk:
            s_next = scores(i + 1)
        s_diag = jnp.where(causal, s[nk - blk:, :], MASK_VALUE)
        s = s_diag if i == 0 else jnp.concatenate([s[:nk - blk, :], s_diag], axis=0)
        m = jnp.max(s, axis=0, keepdims=True)
        p = jnp.exp2(s - m)
        l = jnp.sum(p, axis=0, keepdims=True)
        acc = jnp.dot(vt_ref[:, :nk], p.astype(BF16), preferred_element_type=F32)
        o_ref[0, qs, :] = (acc * (1.0 / l)).T.astype(o_ref.dtype)


def _fox(feat_t, tok3, aug3, *, blk=256):
    bsz, seq, _ = tok3.shape
    hd = HEAD_DIM
    qb, vb = ROW_FQ // hd, ROW_FV // hd
    kern = functools.partial(_fox_kernel, blk=blk)
    return pl.pallas_call(
        kern,
        grid=(bsz, N_FOX_HEADS),
        in_specs=[
            pl.BlockSpec((hd, seq), lambda b, h: (qb + h, b)),
            pl.BlockSpec((1, seq, hd), lambda b, h: (b, 0, h)),
            pl.BlockSpec((1, seq, LANES), lambda b, h: (b, 0, 0)),
            pl.BlockSpec((hd, seq), lambda b, h: (vb + h, b)),
        ],
        out_specs=pl.BlockSpec((1, seq, hd), lambda b, h: (b, 0, h)),
        out_shape=jax.ShapeDtypeStruct((bsz, seq, FOX_W), BF16),
        compiler_params=pltpu.CompilerParams(
            dimension_semantics=("arbitrary", "arbitrary"),
            vmem_limit_bytes=V7X_VMEM_LIMIT),
        name="fox",
    )(feat_t, tok3, aug3, feat_t)


def _swa_kernel(qt_ref, k_ref, vt_ref, sink_ref, o_ref):
    g = pl.program_id(1)
    w, hd = SWA_WINDOW, HEAD_DIM
    seq = k_ref.shape[1]
    gl = SWA_GROUP * w
    sink = jnp.concatenate(
        [jnp.full((1, w), sink_ref[g * SWA_GROUP + r] * LOG2E, F32) for r in range(SWA_GROUP)],
        axis=1)
    kk = lax.broadcasted_iota(jnp.int32, (2 * w, gl), 0)
    qq = lax.broadcasted_iota(jnp.int32, (2 * w, gl), 1) & (w - 1)
    band = (kk > qq) & (kk - w <= qq)
    band_first = band[w:, :]

    for n in range(seq // w):
        q_t = jnp.concatenate(
            [qt_ref[r * hd:(r + 1) * hd, n * w:(n + 1) * w] for r in range(SWA_GROUP)],
            axis=1)
        ks = slice(max(n - 1, 0) * w, (n + 1) * w)
        s = jnp.dot(k_ref[0, ks, :], q_t, preferred_element_type=F32)
        s = jnp.where(band if n > 0 else band_first, s, MASK_VALUE)
        m = jnp.maximum(jnp.max(s, axis=0, keepdims=True), sink)
        p = jnp.exp2(s - m)
        denom = jnp.sum(p, axis=0, keepdims=True) + jnp.exp2(sink - m)
        o_t = jnp.dot(vt_ref[:, ks], p.astype(BF16), preferred_element_type=F32)
        o_t = o_t * (1.0 / denom)
        for r in range(SWA_GROUP):
            o_ref[0, n * w:(n + 1) * w, r * hd:(r + 1) * hd] = (
                o_t[:, r * w:(r + 1) * w].T.astype(o_ref.dtype))


def _swa(feat_t, tok3, sinks):
    bsz, seq, _ = tok3.shape
    hd = HEAD_DIM
    gw = SWA_GROUP * hd
    qb, kb, vb = ROW_SQ // gw, FOX_W // hd, ROW_SV // hd
    return pl.pallas_call(
        _swa_kernel,
        grid=(bsz, N_SWA_KV_HEADS),
        in_specs=[
            pl.BlockSpec((gw, seq), lambda b, g: (qb + g, b)),
            pl.BlockSpec((1, seq, hd), lambda b, g: (b, 0, kb + g)),
            pl.BlockSpec((hd, seq), lambda b, g: (vb + g, b)),
            pl.BlockSpec(memory_space=pltpu.SMEM),
        ],
        out_specs=pl.BlockSpec((1, seq, gw), lambda b, g: (b, 0, g)),
        out_shape=jax.ShapeDtypeStruct((bsz, seq, SWA_Q_W), BF16),
        compiler_params=pltpu.CompilerParams(
            dimension_semantics=("arbitrary", "arbitrary"),
            vmem_limit_bytes=V7X_VMEM_LIMIT),
        name="swa",
    )(feat_t, tok3, feat_t, sinks)


def _outproj_kernel(fox_ref, swa_ref, x_ref, mod_ref, g_ref, wf_ref, ws_ref, o_ref):
    mix = jnp.dot(fox_ref[...], wf_ref[...], preferred_element_type=F32)
    mix = mix + jnp.dot(swa_ref[...], ws_ref[...], preferred_element_type=F32)
    o_ref[...] = mix
    _norm_gate_residual(o_ref, x_ref, mod_ref[0, 2:3, :] * g_ref[...])


def _outproj(fox2, swa2, x2, mod3, g, w_fox, w_swa, *, seq, tm=512):
    t, d = x2.shape
    tiles_per_seq = seq // tm
    return pl.pallas_call(
        _outproj_kernel,
        grid=(t // tm,),
        in_specs=[
            pl.BlockSpec((tm, FOX_W), lambda i: (i, 0)),
            pl.BlockSpec((tm, SWA_Q_W), lambda i: (i, 0)),
            pl.BlockSpec((tm, d), lambda i: (i, 0)),
            pl.BlockSpec((1, N_MOD, d), lambda i: (i // tiles_per_seq, 0, 0)),
            pl.BlockSpec((1, d), lambda i: (0, 0)),
            pl.BlockSpec((FOX_W, d), lambda i: (0, 0)),
            pl.BlockSpec((SWA_Q_W, d), lambda i: (0, 0)),
        ],
        out_specs=pl.BlockSpec((tm, d), lambda i: (i, 0)),
        out_shape=jax.ShapeDtypeStruct((t, d), F32),
        compiler_params=pltpu.CompilerParams(
            dimension_semantics=("arbitrary",), vmem_limit_bytes=V7X_VMEM_LIMIT),
        name="out_proj",
    )(fox2, swa2, x2, mod3, g.reshape(1, d), w_fox, w_swa)


def _mlp_kernel(x_ref, mod_ref, gpre_ref, gpost_ref, wu_ref, wd_ref, o_ref, h_scr):
    j = pl.program_id(1)

    @pl.when(j == 0)
    def _():
        _norm_modulate(x_ref, gpre_ref[...] * (1.0 + mod_ref[0, 4:5, :]), mod_ref[0, 3:4, :],
                       h_scr)
        o_ref[...] = jnp.zeros_like(o_ref)

    u = jnp.dot(h_scr[...], wu_ref[...], preferred_element_type=F32)
    a = jnp.square(jnp.maximum(u, 0.0)).astype(BF16)
    o_ref[...] += jnp.dot(a, wd_ref[...], preferred_element_type=F32)

    @pl.when(j == pl.num_programs(1) - 1)
    def _():
        _norm_gate_residual(o_ref, x_ref, mod_ref[0, 5:6, :] * gpost_ref[...])


def _mlp(x2, mod3, g_pre, g_post, w_up, w_down, *, seq, tm=1024, tf=512):
    t, d = x2.shape
    ff = w_up.shape[1]
    tiles_per_seq = seq // tm
    return pl.pallas_call(
        _mlp_kernel,
        grid=(t // tm, ff // tf),
        in_specs=[
            pl.BlockSpec((tm, d), lambda i, j: (i, 0)),
            pl.BlockSpec((1, N_MOD, d), lambda i, j: (i // tiles_per_seq, 0, 0)),
            pl.BlockSpec((1, d), lambda i, j: (0, 0)),
            pl.BlockSpec((1, d), lambda i, j: (0, 0)),
            pl.BlockSpec((d, tf), lambda i, j: (0, j)),
            pl.BlockSpec((tf, d), lambda i, j: (j, 0)),
        ],
        out_specs=pl.BlockSpec((tm, d), lambda i, j: (i, 0)),
        out_shape=jax.ShapeDtypeStruct((t, d), F32),
        scratch_shapes=[pltpu.VMEM((tm, d), BF16)],
        compiler_params=pltpu.CompilerParams(
            dimension_semantics=("arbitrary", "arbitrary"),
            vmem_limit_bytes=V7X_VMEM_LIMIT),
        name="mlp",
    )(x2, mod3, g_pre.reshape(1, d), g_post.reshape(1, d), w_up, w_down)


def _rope_tables(seq):
    half = HEAD_DIM // 2
    inv_freq = 1.0 / (ROPE_THETA ** (jnp.arange(half, dtype=F32) * (2.0 / HEAD_DIM)))
    ang = jnp.arange(seq).astype(F32)[:, None] * inv_freq[None, :]
    cos, sin = jnp.cos(ang), jnp.sin(ang)
    return (jnp.concatenate([cos, cos], axis=-1), jnp.concatenate([-sin, sin], axis=-1),
            cos.T, sin.T)


def kernel(x, c, w_mod, b_mod, g_pre_mix, g_post_mix, w_in, b_forget, swa_sinks,
           w_out, g_pre_mlp, g_post_mlp, w_up, w_down):
    bsz, seq, d = x.shape
    depth = w_mod.shape[0]
    t = bsz * seq
    tm_in = 512
    cos, sin_signed, cos_t, sin_t = _rope_tables(seq)
    phase = (jnp.arange(LANES, dtype=jnp.int32) % CUM_SPLIT).reshape(1, LANES)
    tri = jnp.tril(jnp.ones((tm_in, tm_in), BF16))
    o_fk, o_fv, o_fg = FOX_W, 2 * FOX_W, 3 * FOX_W
    o_sq = o_fg + N_FOX_HEADS
    o_sk = o_sq + SWA_Q_W
    o_sv = o_sk + SWA_KV_W
    n_rep = CUM_SPLIT * N_FOX_HEADS

    x2 = x.reshape(t, d)
    for l in range(depth):
        wi = w_in[l]
        w_tok = jnp.concatenate([wi[:, o_fk:o_fv], wi[:, o_sk:o_sv]], axis=1).astype(BF16)
        w_feat_t = jnp.concatenate(
            [wi[:, :o_fk], wi[:, o_fv:o_fg], wi[:, o_sq:o_sk], wi[:, o_sv:]],
            axis=1).T.astype(BF16)
        w_fg = jnp.zeros((d, LANES), F32).at[:, :n_rep].set(
            jnp.repeat(wi[:, o_fg:o_sq], CUM_SPLIT, axis=1)).astype(BF16)
        b_fg = jnp.zeros((1, LANES), F32).at[0, :n_rep].set(
            jnp.repeat(b_forget[l].astype(F32), CUM_SPLIT))
        w_o = w_out[l].astype(BF16)
        w_u = w_up[l].astype(BF16)
        w_d = w_down[l].astype(BF16)

        mod3 = _mod(c, w_mod[l], b_mod[l]).reshape(bsz, N_MOD, d)
        tok, aug, feat_t = _inproj(x2, mod3, g_pre_mix[l], w_tok, w_feat_t, w_fg, b_fg, phase,
                                   tri, cos, sin_signed, cos_t, sin_t, seq=seq, tm=tm_in)
        tok3 = tok.reshape(bsz, seq, TOK_W)
        fox = _fox(feat_t, tok3, aug.reshape(bsz, seq, LANES))
        swa = _swa(feat_t, tok3, swa_sinks[l])
        x2 = _outproj(fox.reshape(t, FOX_W), swa.reshape(t, SWA_Q_W), x2, mod3,
                      g_post_mix[l], w_o[:FOX_W], w_o[FOX_W:], seq=seq)
        x2 = _mlp(x2, mod3, g_pre_mlp[l], g_post_mlp[l], w_u, w_d, seq=seq)
    return x2.reshape(bsz, seq, d)
```

```python
import functools
import math

import jax
import jax.numpy as jnp
from jax import lax
from jax.experimental import pallas as pl
from jax.experimental.pallas import tpu as pltpu

HEAD_DIM = 128
N_FOX_HEADS = 8
N_SWA_HEADS = 8
N_SWA_KV_HEADS = 2
SWA_GROUP = N_SWA_HEADS // N_SWA_KV_HEADS
SWA_WINDOW = 128
ROPE_THETA = 10000.0
NORM_EPS = 1e-6
N_MOD = 6
MASK_VALUE = -1e30
LOG2E = math.log2(math.e)
Q_SCALE = HEAD_DIM ** -0.5 * LOG2E

FOX_W = N_FOX_HEADS * HEAD_DIM
SWA_Q_W = N_SWA_HEADS * HEAD_DIM
SWA_KV_W = N_SWA_KV_HEADS * HEAD_DIM
TOK_W = FOX_W + SWA_KV_W
ROW_FQ = 0
ROW_FV = FOX_W
ROW_SQ = 2 * FOX_W
ROW_SV = ROW_SQ + SWA_Q_W
FEAT_W = ROW_SV + SWA_KV_W
CUM_SPLIT = 3
LANES = 128

V7X_VMEM_LIMIT = 60 * 1024 * 1024

BF16 = jnp.bfloat16
F32 = jnp.float32
_NT = (((1,), (1,)), ((), ()))


NORM_ROWS = 16


def _inv_rms(x):
    return lax.rsqrt(jnp.mean(x * x, axis=-1, keepdims=True) + NORM_EPS)


def _norm_modulate(x_ref, gain_row, shift_row, out_ref, row0=0, nrows=None):
    nrows = x_ref.shape[0] if nrows is None else nrows
    for c in range(nrows // NORM_ROWS):
        rows = slice(row0 + c * NORM_ROWS, row0 + (c + 1) * NORM_ROWS)
        x = x_ref[rows, :]
        out_ref[rows, :] = ((x * _inv_rms(x)) * gain_row + shift_row).astype(out_ref.dtype)


def _norm_gate_residual(y_ref, x_ref, gate_row, row0=0, nrows=None):
    nrows = x_ref.shape[0] if nrows is None else nrows
    for c in range(nrows // NORM_ROWS):
        rows = slice(row0 + c * NORM_ROWS, row0 + (c + 1) * NORM_ROWS)
        y = y_ref[rows, :]
        y_ref[rows, :] = x_ref[rows, :] + (y * _inv_rms(y)) * gate_row


def _resident(shape, index_map):
    return pl.BlockSpec(shape, index_map, pipeline_mode=pl.Buffered(1))


def _mod_kernel(c_ref, w_ref, b_ref, o_ref):
    c = c_ref[...]
    cond = c * jax.nn.sigmoid(c)
    o_ref[...] = jnp.dot(cond.astype(BF16), w_ref[...].astype(BF16),
                         preferred_element_type=F32) + b_ref[...]


def _mod(c, w, b, *, tn=1024):
    bsz, d = c.shape
    n = w.shape[1]
    return pl.pallas_call(
        _mod_kernel,
        grid=(n // tn,),
        in_specs=[pl.BlockSpec((bsz, d), lambda j: (0, 0)),
                  pl.BlockSpec((d, tn), lambda j: (0, j)),
                  pl.BlockSpec((1, tn), lambda j: (0, j))],
        out_specs=pl.BlockSpec((bsz, tn), lambda j: (0, j)),
        out_shape=jax.ShapeDtypeStruct((bsz, n), F32),
        compiler_params=pltpu.CompilerParams(
            dimension_semantics=("arbitrary",), vmem_limit_bytes=V7X_VMEM_LIMIT),
        name="mod",
    )(c, w, b.reshape(1, n))


def _inproj_kernel(x_ref, mod_ref, g_ref, wtok_ref, wfeat_ref, wfg_ref, bfg_ref, phase_ref,
                   tri_ref, cos_ref, sin_ref, cost_ref, sint_ref,
                   tok_ref, aug_ref, feat_ref, h_scr, carry_scr, *, tiles_per_seq):
    i = pl.program_id(0)
    hd = HEAD_DIM
    half = hd // 2

    _norm_modulate(x_ref, g_ref[...] * (1.0 + mod_ref[0, 1:2, :]), mod_ref[0, 0:1, :], h_scr)

    kt = jnp.dot(h_scr[...], wtok_ref[...], preferred_element_type=F32)
    tok_ref[:, :FOX_W] = kt[:, :FOX_W].astype(BF16)
    cos, sin_signed = cos_ref[...], sin_ref[...]
    for r in range(N_SWA_KV_HEADS):
        blk = kt[:, FOX_W + r * hd:FOX_W + (r + 1) * hd]
        tok_ref[:, FOX_W + r * hd:FOX_W + (r + 1) * hd] = (
            blk * cos + pltpu.roll(blk, half, 1) * sin_signed).astype(BF16)

    z = jnp.dot(h_scr[...], wfg_ref[...], preferred_element_type=F32) + bfg_ref[...]
    logf = jnp.minimum(z, 0.0) - jnp.log1p(jnp.exp(-jnp.abs(z)))
    p0 = logf.astype(BF16)
    r0 = logf - p0.astype(F32)
    p1 = r0.astype(BF16)
    p2 = (r0 - p1.astype(F32)).astype(BF16)
    cs = jnp.dot(tri_ref[...], jnp.concatenate([p0, p1, p2], axis=1),
                 preferred_element_type=F32)

    @pl.when(i % tiles_per_seq == 0)
    def _():
        carry_scr[...] = jnp.zeros_like(carry_scr)

    cum = cs[:, :LANES] + cs[:, LANES:2 * LANES] + cs[:, 2 * LANES:] + carry_scr[...]
    carry_scr[...] = cum[cum.shape[0] - 1:, :]
    negc = cum * (-LOG2E)
    hi = negc.astype(BF16).astype(F32)
    rem = negc - hi
    mid = rem.astype(BF16).astype(F32)
    lo = rem - mid
    ph = phase_ref[...]
    aug_ref[...] = jnp.where(ph == 0, hi, jnp.where(ph == 1, mid, lo)).astype(BF16)

    def feat(lo_row, hi_row):
        return lax.dot_general(wfeat_ref[lo_row:hi_row, :], h_scr[...], _NT,
                               preferred_element_type=F32)

    feat_ref[ROW_FQ:ROW_FQ + FOX_W, :] = (feat(ROW_FQ, ROW_FQ + FOX_W) * Q_SCALE).astype(BF16)
    feat_ref[ROW_FV:ROW_FV + FOX_W, :] = feat(ROW_FV, ROW_FV + FOX_W).astype(BF16)
    sq = feat(ROW_SQ, ROW_SQ + SWA_Q_W)
    cos_t, sin_t = cost_ref[...], sint_ref[...]
    for hh in range(N_SWA_HEADS):
        x1 = sq[hh * hd:hh * hd + half, :]
        x2 = sq[hh * hd + half:(hh + 1) * hd, :]
        base = ROW_SQ + hh * hd
        feat_ref[base:base + half, :] = ((x1 * cos_t - x2 * sin_t) * Q_SCALE).astype(BF16)
        feat_ref[base + half:base + hd, :] = ((x2 * cos_t + x1 * sin_t) * Q_SCALE).astype(BF16)
    feat_ref[ROW_SV:, :] = feat(ROW_SV, FEAT_W).astype(BF16)


def _inproj(x2, mod3, g, w_tok, w_feat_t, w_fg, b_fg, phase, tri, cos, sin_signed,
            cos_t, sin_t, *, seq, tm):
    t, d = x2.shape
    tiles_per_seq = seq // tm
    half = HEAD_DIM // 2
    kern = functools.partial(_inproj_kernel, tiles_per_seq=tiles_per_seq)
    const = lambda i: (0, 0)
    pos = lambda i: (i % tiles_per_seq, 0)
    pos_t = lambda i: (0, i % tiles_per_seq)
    return pl.pallas_call(
        kern,
        grid=(t // tm,),
        in_specs=[
            pl.BlockSpec((tm, d), lambda i: (i, 0)),
            pl.BlockSpec((1, N_MOD, d), lambda i: (i // tiles_per_seq, 0, 0)),
            _resident((1, d), const),
            _resident((d, TOK_W), const),
            _resident((FEAT_W, d), const),
            _resident((d, LANES), const),
            _resident((1, LANES), const),
            _resident((1, LANES), const),
            _resident((tm, tm), const),
            pl.BlockSpec((tm, HEAD_DIM), pos),
            pl.BlockSpec((tm, HEAD_DIM), pos),
            pl.BlockSpec((half, tm), pos_t),
            pl.BlockSpec((half, tm), pos_t),
        ],
        out_specs=[
            pl.BlockSpec((tm, TOK_W), lambda i: (i, 0)),
            pl.BlockSpec((tm, LANES), lambda i: (i, 0)),
            pl.BlockSpec((FEAT_W, tm), lambda i: (0, i)),
        ],
        out_shape=[jax.ShapeDtypeStruct((t, TOK_W), BF16),
                   jax.ShapeDtypeStruct((t, LANES), BF16),
                   jax.ShapeDtypeStruct((FEAT_W, t), BF16)],
        scratch_shapes=[pltpu.VMEM((tm, d), BF16), pltpu.VMEM((1, LANES), F32)],
        compiler_params=pltpu.CompilerParams(
            dimension_semantics=("arbitrary",), vmem_limit_bytes=V7X_VMEM_LIMIT),
        name="in_proj",
    )(x2, mod3, g.reshape(1, d), w_tok, w_feat_t, w_fg, b_fg, phase, tri, cos, sin_signed,
      cos_t, sin_t)


def _fox_kernel(qt_ref, k_ref, aug_ref, vt_ref, o_ref, *, blk, heads):
    hd = HEAD_DIM
    seq = k_ref.shape[1]
    nblk = seq // blk
    r = lax.broadcasted_iota(jnp.int32, (LANES, blk), 0)
    kk = lax.broadcasted_iota(jnp.int32, (blk, blk), 0)
    qq = lax.broadcasted_iota(jnp.int32, (blk, blk), 1)
    causal = kk <= qq
    onehot_t = []
    for hh in range(heads):
        hidx = pl.program_id(1) * heads + hh
        sel = (r >= CUM_SPLIT * hidx) & (r < CUM_SPLIT * (hidx + 1))
        onehot_t.append(jnp.where(sel, 1.0, 0.0).astype(BF16))

    def scores(hh, i):
        nk = (i + 1) * blk
        cols = slice(hh * hd, (hh + 1) * hd)
        q_aug = jnp.concatenate([qt_ref[cols, i * blk:nk], onehot_t[hh]], axis=0)
        k_aug = jnp.concatenate([k_ref[0, :nk, cols], aug_ref[0, :nk, :]], axis=1)
        return jnp.dot(k_aug, q_aug, preferred_element_type=F32)

    s_next = [scores(hh, 0) for hh in range(heads)]
    for i in range(nblk):
        qs = slice(i * blk, (i + 1) * blk)
        nk = (i + 1) * blk
        s_cur = s_next
        if i + 1 < nblk:
            s_next = [scores(hh, i + 1) for hh in range(heads)]
        for hh in range(heads):
            cols = slice(hh * hd, (hh + 1) * hd)
            s = s_cur[hh]
            s_diag = jnp.where(causal, s[nk - blk:, :], MASK_VALUE)
            s = s_diag if i == 0 else jnp.concatenate([s[:nk - blk, :], s_diag], axis=0)
            m = jnp.max(s, axis=0, keepdims=True)
            p = jnp.exp2(s - m)
            l = jnp.sum(p, axis=0, keepdims=True)
            acc = jnp.dot(vt_ref[cols, :nk], p.astype(BF16), preferred_element_type=F32)
            o_ref[0, qs, cols] = (acc * (1.0 / l)).T.astype(o_ref.dtype)


def _fox(feat_t, tok3, aug3, *, blk=256, heads=4):
    bsz, seq, _ = tok3.shape
    hw = heads * HEAD_DIM
    qb, vb = ROW_FQ // hw, ROW_FV // hw
    kern = functools.partial(_fox_kernel, blk=blk, heads=heads)
    return pl.pallas_call(
        kern,
        grid=(bsz, N_FOX_HEADS // heads),
        in_specs=[
            pl.BlockSpec((hw, seq), lambda b, h: (qb + h, b)),
            pl.BlockSpec((1, seq, hw), lambda b, h: (b, 0, h)),
            pl.BlockSpec((1, seq, LANES), lambda b, h: (b, 0, 0)),
            pl.BlockSpec((hw, seq), lambda b, h: (vb + h, b)),
        ],
        out_specs=pl.BlockSpec((1, seq, hw), lambda b, h: (b, 0, h)),
        out_shape=jax.ShapeDtypeStruct((bsz, seq, FOX_W), BF16),
        compiler_params=pltpu.CompilerParams(
            dimension_semantics=("arbitrary", "arbitrary"),
            vmem_limit_bytes=V7X_VMEM_LIMIT),
        name="fox",
    )(feat_t, tok3, aug3, feat_t)


def _swa_kernel(qt_ref, k_ref, vt_ref, sink_ref, o_ref):
    g = pl.program_id(1)
    w, hd = SWA_WINDOW, HEAD_DIM
    seq = k_ref.shape[1]
    gl = SWA_GROUP * w
    sink = jnp.concatenate(
        [jnp.full((1, w), sink_ref[g * SWA_GROUP + r] * LOG2E, F32) for r in range(SWA_GROUP)],
        axis=1)
    kk = lax.broadcasted_iota(jnp.int32, (2 * w, gl), 0)
    qq = lax.broadcasted_iota(jnp.int32, (2 * w, gl), 1) & (w - 1)
    band = (kk > qq) & (kk - w <= qq)
    band_first = band[w:, :]

    def scores(n):
        q_t = jnp.concatenate(
            [qt_ref[r * hd:(r + 1) * hd, n * w:(n + 1) * w] for r in range(SWA_GROUP)],
            axis=1)
        ks = slice(max(n - 1, 0) * w, (n + 1) * w)
        return jnp.dot(k_ref[0, ks, :], q_t, preferred_element_type=F32)

    nblk = seq // w
    s_next = scores(0)
    for n in range(nblk):
        ks = slice(max(n - 1, 0) * w, (n + 1) * w)
        s = s_next
        if n + 1 < nblk:
            s_next = scores(n + 1)
        s = jnp.where(band if n > 0 else band_first, s, MASK_VALUE)
        m = jnp.maximum(jnp.max(s, axis=0, keepdims=True), sink)
        p = jnp.exp2(s - m)
        denom = jnp.sum(p, axis=0, keepdims=True) + jnp.exp2(sink - m)
        o_t = jnp.dot(vt_ref[:, ks], p.astype(BF16), preferred_element_type=F32)
        o_t = o_t * (1.0 / denom)
        for r in range(SWA_GROUP):
            o_ref[0, n * w:(n + 1) * w, r * hd:(r + 1) * hd] = (
                o_t[:, r * w:(r + 1) * w].T.astype(o_ref.dtype))


def _swa(feat_t, tok3, sinks):
    bsz, seq, _ = tok3.shape
    hd = HEAD_DIM
    gw = SWA_GROUP * hd
    qb, kb, vb = ROW_SQ // gw, FOX_W // hd, ROW_SV // hd
    return pl.pallas_call(
        _swa_kernel,
        grid=(bsz, N_SWA_KV_HEADS),
        in_specs=[
            pl.BlockSpec((gw, seq), lambda b, g: (qb + g, b)),
            pl.BlockSpec((1, seq, hd), lambda b, g: (b, 0, kb + g)),
            pl.BlockSpec((hd, seq), lambda b, g: (vb + g, b)),
            pl.BlockSpec(memory_space=pltpu.SMEM),
        ],
        out_specs=pl.BlockSpec((1, seq, gw), lambda b, g: (b, 0, g)),
        out_shape=jax.ShapeDtypeStruct((bsz, seq, SWA_Q_W), BF16),
        compiler_params=pltpu.CompilerParams(
            dimension_semantics=("arbitrary", "arbitrary"),
            vmem_limit_bytes=V7X_VMEM_LIMIT),
        name="swa",
    )(feat_t, tok3, feat_t, sinks)


def _outproj_kernel(fox_ref, swa_ref, x_ref, mod_ref, g_ref, wf_ref, ws_ref, o_ref):
    mix = jnp.dot(fox_ref[...], wf_ref[...], preferred_element_type=F32)
    mix = mix + jnp.dot(swa_ref[...], ws_ref[...], preferred_element_type=F32)
    o_ref[...] = mix
    _norm_gate_residual(o_ref, x_ref, mod_ref[0, 2:3, :] * g_ref[...])


def _outproj(fox2, swa2, x2, mod3, g, w_fox, w_swa, *, seq, tm=512):
    t, d = x2.shape
    tiles_per_seq = seq // tm
    return pl.pallas_call(
        _outproj_kernel,
        grid=(t // tm,),
        in_specs=[
            pl.BlockSpec((tm, FOX_W), lambda i: (i, 0)),
            pl.BlockSpec((tm, SWA_Q_W), lambda i: (i, 0)),
            pl.BlockSpec((tm, d), lambda i: (i, 0)),
            pl.BlockSpec((1, N_MOD, d), lambda i: (i // tiles_per_seq, 0, 0)),
            pl.BlockSpec((1, d), lambda i: (0, 0)),
            pl.BlockSpec((FOX_W, d), lambda i: (0, 0)),
            pl.BlockSpec((SWA_Q_W, d), lambda i: (0, 0)),
        ],
        out_specs=pl.BlockSpec((tm, d), lambda i: (i, 0)),
        out_shape=jax.ShapeDtypeStruct((t, d), F32),
        compiler_params=pltpu.CompilerParams(
            dimension_semantics=("arbitrary",), vmem_limit_bytes=V7X_VMEM_LIMIT),
        name="out_proj",
    )(fox2, swa2, x2, mod3, g.reshape(1, d), w_fox, w_swa)


MLP_ROW_PARTS = 2


def _mlp_kernel(x_ref, mod_ref, gpre_ref, gpost_ref, wu_ref, wd_ref, o_ref, h_scr):
    j = pl.program_id(1)
    last = pl.num_programs(1) - 1
    part = x_ref.shape[0] // MLP_ROW_PARTS

    def step(first, final):
        if first:
            gain = gpre_ref[...] * (1.0 + mod_ref[0, 4:5, :])
            shift = mod_ref[0, 3:4, :]
        if final:
            gate = mod_ref[0, 5:6, :] * gpost_ref[...]
        ups = []
        for a in range(MLP_ROW_PARTS):
            rows = slice(a * part, (a + 1) * part)
            if first:
                _norm_modulate(x_ref, gain, shift, h_scr, a * part, part)
            ups.append(jnp.dot(h_scr[rows, :], wu_ref[...], preferred_element_type=F32))
        for a in range(MLP_ROW_PARTS):
            rows = slice(a * part, (a + 1) * part)
            act = jnp.square(jnp.maximum(ups[a], 0.0)).astype(BF16)
            y = jnp.dot(act, wd_ref[...], preferred_element_type=F32)
            if first:
                o_ref[rows, :] = y
            else:
                o_ref[rows, :] += y
            if final:
                _norm_gate_residual(o_ref, x_ref, gate, a * part, part)

    pl.when(j == 0)(lambda: step(True, False))
    pl.when((j > 0) & (j < last))(lambda: step(False, False))
    pl.when(j == last)(lambda: step(False, True))


def _mlp(x2, mod3, g_pre, g_post, w_up, w_down, *, seq, tm=1024, tf=1024):
    t, d = x2.shape
    ff = w_up.shape[1]
    tiles_per_seq = seq // tm
    return pl.pallas_call(
        _mlp_kernel,
        grid=(t // tm, ff // tf),
        in_specs=[
            pl.BlockSpec((tm, d), lambda i, j: (i, 0)),
            pl.BlockSpec((1, N_MOD, d), lambda i, j: (i // tiles_per_seq, 0, 0)),
            pl.BlockSpec((1, d), lambda i, j: (0, 0)),
            pl.BlockSpec((1, d), lambda i, j: (0, 0)),
            pl.BlockSpec((d, tf), lambda i, j: (0, j)),
            pl.BlockSpec((tf, d), lambda i, j: (j, 0)),
        ],
        out_specs=pl.BlockSpec((tm, d), lambda i, j: (i, 0)),
        out_shape=jax.ShapeDtypeStruct((t, d), F32),
        scratch_shapes=[pltpu.VMEM((tm, d), BF16)],
        compiler_params=pltpu.CompilerParams(
            dimension_semantics=("arbitrary", "arbitrary"),
            vmem_limit_bytes=V7X_VMEM_LIMIT),
        name="mlp",
    )(x2, mod3, g_pre.reshape(1, d), g_post.reshape(1, d), w_up, w_down)


def _rope_tables(seq):
    half = HEAD_DIM // 2
    inv_freq = 1.0 / (ROPE_THETA ** (jnp.arange(half, dtype=F32) * (2.0 / HEAD_DIM)))
    ang = jnp.arange(seq).astype(F32)[:, None] * inv_freq[None, :]
    cos, sin = jnp.cos(ang), jnp.sin(ang)
    return (jnp.concatenate([cos, cos], axis=-1), jnp.concatenate([-sin, sin], axis=-1),
            cos.T, sin.T)


def kernel(x, c, w_mod, b_mod, g_pre_mix, g_post_mix, w_in, b_forget, swa_sinks,
           w_out, g_pre_mlp, g_post_mlp, w_up, w_down):
    bsz, seq, d = x.shape
    depth = w_mod.shape[0]
    t = bsz * seq
    tm_in = 512
    cos, sin_signed, cos_t, sin_t = _rope_tables(seq)
    phase = (jnp.arange(LANES, dtype=jnp.int32) % CUM_SPLIT).reshape(1, LANES)
    tri = jnp.tril(jnp.ones((tm_in, tm_in), BF16))
    o_fk, o_fv, o_fg = FOX_W, 2 * FOX_W, 3 * FOX_W
    o_sq = o_fg + N_FOX_HEADS
    o_sk = o_sq + SWA_Q_W
    o_sv = o_sk + SWA_KV_W
    n_rep = CUM_SPLIT * N_FOX_HEADS

    x2 = x.reshape(t, d)
    for l in range(depth):
        wi = w_in[l]
        w_tok = jnp.concatenate([wi[:, o_fk:o_fv], wi[:, o_sk:o_sv]], axis=1).astype(BF16)
        w_feat_t = jnp.concatenate(
            [wi[:, :o_fk], wi[:, o_fv:o_fg], wi[:, o_sq:o_sk], wi[:, o_sv:]],
            axis=1).T.astype(BF16)
        w_fg = jnp.zeros((d, LANES), F32).at[:, :n_rep].set(
            jnp.repeat(wi[:, o_fg:o_sq], CUM_SPLIT, axis=1)).astype(BF16)
        b_fg = jnp.zeros((1, LANES), F32).at[0, :n_rep].set(
            jnp.repeat(b_forget[l].astype(F32), CUM_SPLIT))
        w_o = w_out[l].astype(BF16)
        w_u = w_up[l].astype(BF16)
        w_d = w_down[l].astype(BF16)

        mod3 = _mod(c, w_mod[l], b_mod[l]).reshape(bsz, N_MOD, d)
        tok, aug, feat_t = _inproj(x2, mod3, g_pre_mix[l], w_tok, w_feat_t, w_fg, b_fg, phase,
                                   tri, cos, sin_signed, cos_t, sin_t, seq=seq, tm=tm_in)
        tok3 = tok.reshape(bsz, seq, TOK_W)
        fox = _fox(feat_t, tok3, aug.reshape(bsz, seq, LANES))
        swa = _swa(feat_t, tok3, swa_sinks[l])
        x2 = _outproj(fox.reshape(t, FOX_W), swa.reshape(t, SWA_Q_W), x2, mod3,
                      g_post_mix[l], w_o[:FOX_W], w_o[FOX_W:], seq=seq)
        x2 = _mlp(x2, mod3, g_pre_mlp[l], g_post_mlp[l], w_u, w_d, seq=seq)
    return x2.reshape(bsz, seq, d)
```

```python
import functools
import math

import jax
import jax.numpy as jnp
from jax import lax
from jax.experimental import pallas as pl
from jax.experimental.pallas import tpu as pltpu

HEAD_DIM = 128
N_FOX_HEADS = 8
N_SWA_HEADS = 8
N_SWA_KV_HEADS = 2
SWA_GROUP = N_SWA_HEADS // N_SWA_KV_HEADS
SWA_WINDOW = 128
ROPE_THETA = 10000.0
NORM_EPS = 1e-6
N_MOD = 6
MASK_VALUE = -1e30
LOG2E = math.log2(math.e)
Q_SCALE = HEAD_DIM ** -0.5 * LOG2E

FOX_W = N_FOX_HEADS * HEAD_DIM
SWA_Q_W = N_SWA_HEADS * HEAD_DIM
SWA_KV_W = N_SWA_KV_HEADS * HEAD_DIM
TOK_W = FOX_W + SWA_KV_W
ROW_FQ = 0
ROW_FV = FOX_W
ROW_SQ = 2 * FOX_W
ROW_SV = ROW_SQ + SWA_Q_W
FEAT_W = ROW_SV + SWA_KV_W
CUM_SPLIT = 3
LANES = 128

V7X_VMEM_LIMIT = 60 * 1024 * 1024

BF16 = jnp.bfloat16
F32 = jnp.float32
_NT = (((1,), (1,)), ((), ()))


NORM_ROWS = 16


def _inv_rms(x):
    return lax.rsqrt(jnp.mean(x * x, axis=-1, keepdims=True) + NORM_EPS)


def _norm_modulate(x_ref, gain_row, shift_row, out_ref, row0=0, nrows=None):
    nrows = x_ref.shape[0] if nrows is None else nrows
    for c in range(nrows // NORM_ROWS):
        rows = slice(row0 + c * NORM_ROWS, row0 + (c + 1) * NORM_ROWS)
        x = x_ref[rows, :]
        out_ref[rows, :] = ((x * _inv_rms(x)) * gain_row + shift_row).astype(out_ref.dtype)


def _norm_gate_residual(y_ref, x_ref, gate_row, row0=0, nrows=None):
    nrows = x_ref.shape[0] if nrows is None else nrows
    for c in range(nrows // NORM_ROWS):
        rows = slice(row0 + c * NORM_ROWS, row0 + (c + 1) * NORM_ROWS)
        y = y_ref[rows, :]
        y_ref[rows, :] = x_ref[rows, :] + (y * _inv_rms(y)) * gate_row


def _resident(shape, index_map):
    return pl.BlockSpec(shape, index_map, pipeline_mode=pl.Buffered(1))


def _mod_kernel(c_ref, w_ref, b_ref, o_ref):
    c = c_ref[...]
    cond = c * jax.nn.sigmoid(c)
    o_ref[...] = jnp.dot(cond.astype(BF16), w_ref[...].astype(BF16),
                         preferred_element_type=F32) + b_ref[...]


def _mod(c, w, b, *, tn=1024):
    bsz, d = c.shape
    n = w.shape[1]
    return pl.pallas_call(
        _mod_kernel,
        grid=(n // tn,),
        in_specs=[pl.BlockSpec((bsz, d), lambda j: (0, 0)),
                  pl.BlockSpec((d, tn), lambda j: (0, j)),
                  pl.BlockSpec((1, tn), lambda j: (0, j))],
        out_specs=pl.BlockSpec((bsz, tn), lambda j: (0, j)),
        out_shape=jax.ShapeDtypeStruct((bsz, n), F32),
        compiler_params=pltpu.CompilerParams(
            dimension_semantics=("arbitrary",), vmem_limit_bytes=V7X_VMEM_LIMIT),
        name="mod",
    )(c, w, b.reshape(1, n))


def _inproj_kernel(x_ref, mod_ref, g_ref, wtok_ref, wfeat_ref, wfg_ref, bfg_ref, phase_ref,
                   tri_ref, cos_ref, sin_ref, cost_ref, sint_ref, wo_ref, wu_ref, wd_ref,
                   tok_ref, aug_ref, feat_ref, wo_bf_ref, wu_bf_ref, wd_bf_ref,
                   h_scr, carry_scr, *, tiles_per_seq):
    i = pl.program_id(0)
    hd = HEAD_DIM
    half = hd // 2

    wo_bf_ref[...] = wo_ref[...].astype(BF16)
    wu_bf_ref[...] = wu_ref[...].astype(BF16)
    wd_bf_ref[...] = wd_ref[...].astype(BF16)

    _norm_modulate(x_ref, g_ref[...] * (1.0 + mod_ref[0, 1:2, :]), mod_ref[0, 0:1, :], h_scr)

    kt = jnp.dot(h_scr[...], wtok_ref[...], preferred_element_type=F32)
    tok_ref[:, :FOX_W] = kt[:, :FOX_W].astype(BF16)
    cos, sin_signed = cos_ref[...], sin_ref[...]
    for r in range(N_SWA_KV_HEADS):
        blk = kt[:, FOX_W + r * hd:FOX_W + (r + 1) * hd]
        tok_ref[:, FOX_W + r * hd:FOX_W + (r + 1) * hd] = (
            blk * cos + pltpu.roll(blk, half, 1) * sin_signed).astype(BF16)

    z = jnp.dot(h_scr[...], wfg_ref[...], preferred_element_type=F32) + bfg_ref[...]
    logf = jnp.minimum(z, 0.0) - jnp.log1p(jnp.exp(-jnp.abs(z)))
    p0 = logf.astype(BF16)
    r0 = logf - p0.astype(F32)
    p1 = r0.astype(BF16)
    p2 = (r0 - p1.astype(F32)).astype(BF16)
    cs = jnp.dot(tri_ref[...], jnp.concatenate([p0, p1, p2], axis=1),
                 preferred_element_type=F32)

    @pl.when(i % tiles_per_seq == 0)
    def _():
        carry_scr[...] = jnp.zeros_like(carry_scr)

    cum = cs[:, :LANES] + cs[:, LANES:2 * LANES] + cs[:, 2 * LANES:] + carry_scr[...]
    carry_scr[...] = cum[cum.shape[0] - 1:, :]
    negc = cum * (-LOG2E)
    hi = negc.astype(BF16).astype(F32)
    rem = negc - hi
    mid = rem.astype(BF16).astype(F32)
    lo = rem - mid
    ph = phase_ref[...]
    aug_ref[...] = jnp.where(ph == 0, hi, jnp.where(ph == 1, mid, lo)).astype(BF16)

    def feat(lo_row, hi_row):
        return lax.dot_general(wfeat_ref[lo_row:hi_row, :], h_scr[...], _NT,
                               preferred_element_type=F32)

    feat_ref[ROW_FQ:ROW_FQ + FOX_W, :] = (feat(ROW_FQ, ROW_FQ + FOX_W) * Q_SCALE).astype(BF16)
    feat_ref[ROW_FV:ROW_FV + FOX_W, :] = feat(ROW_FV, ROW_FV + FOX_W).astype(BF16)
    sq = feat(ROW_SQ, ROW_SQ + SWA_Q_W)
    cos_t, sin_t = cost_ref[...], sint_ref[...]
    for hh in range(N_SWA_HEADS):
        x1 = sq[hh * hd:hh * hd + half, :]
        x2 = sq[hh * hd + half:(hh + 1) * hd, :]
        base = ROW_SQ + hh * hd
        feat_ref[base:base + half, :] = ((x1 * cos_t - x2 * sin_t) * Q_SCALE).astype(BF16)
        feat_ref[base + half:base + hd, :] = ((x2 * cos_t + x1 * sin_t) * Q_SCALE).astype(BF16)
    feat_ref[ROW_SV:, :] = feat(ROW_SV, FEAT_W).astype(BF16)


def _inproj(x2, mod3, g, w_tok, w_feat_t, w_fg, b_fg, phase, tri, cos, sin_signed,
            cos_t, sin_t, later_weights, *, seq, tm):
    t, d = x2.shape
    steps = t // tm
    tiles_per_seq = seq // tm
    half = HEAD_DIM // 2
    kern = functools.partial(_inproj_kernel, tiles_per_seq=tiles_per_seq)
    const = lambda i: (0, 0)
    pos = lambda i: (i % tiles_per_seq, 0)
    pos_t = lambda i: (0, i % tiles_per_seq)
    slab = lambda i: (i, 0)
    slab_specs = [pl.BlockSpec((w.shape[0] // steps, w.shape[1]), slab) for w in later_weights]
    return pl.pallas_call(
        kern,
        grid=(steps,),
        in_specs=[
            pl.BlockSpec((tm, d), lambda i: (i, 0)),
            pl.BlockSpec((1, N_MOD, d), lambda i: (i // tiles_per_seq, 0, 0)),
            _resident((1, d), const),
            _resident((d, TOK_W), const),
            _resident((FEAT_W, d), const),
            _resident((d, LANES), const),
            _resident((1, LANES), const),
            _resident((1, LANES), const),
            _resident((tm, tm), const),
            pl.BlockSpec((tm, HEAD_DIM), pos),
            pl.BlockSpec((tm, HEAD_DIM), pos),
            pl.BlockSpec((half, tm), pos_t),
            pl.BlockSpec((half, tm), pos_t),
        ] + slab_specs,
        out_specs=[
            pl.BlockSpec((tm, TOK_W), lambda i: (i, 0)),
            pl.BlockSpec((tm, LANES), lambda i: (i, 0)),
            pl.BlockSpec((FEAT_W, tm), lambda i: (0, i)),
        ] + slab_specs,
        out_shape=[jax.ShapeDtypeStruct((t, TOK_W), BF16),
                   jax.ShapeDtypeStruct((t, LANES), BF16),
                   jax.ShapeDtypeStruct((FEAT_W, t), BF16)]
        + [jax.ShapeDtypeStruct(w.shape, BF16) for w in later_weights],
        scratch_shapes=[pltpu.VMEM((tm, d), BF16), pltpu.VMEM((1, LANES), F32)],
        compiler_params=pltpu.CompilerParams(
            dimension_semantics=("arbitrary",), vmem_limit_bytes=V7X_VMEM_LIMIT),
        name="in_proj",
    )(x2, mod3, g.reshape(1, d), w_tok, w_feat_t, w_fg, b_fg, phase, tri, cos, sin_signed,
      cos_t, sin_t, *later_weights)


def _fox_kernel(qt_ref, k_ref, aug_ref, vt_ref, o_ref, *, blk, heads):
    hd = HEAD_DIM
    seq = k_ref.shape[1]
    nblk = seq // blk
    r = lax.broadcasted_iota(jnp.int32, (LANES, blk), 0)
    kk = lax.broadcasted_iota(jnp.int32, (blk, blk), 0)
    qq = lax.broadcasted_iota(jnp.int32, (blk, blk), 1)
    causal = kk <= qq
    onehot_t = []
    for hh in range(heads):
        hidx = pl.program_id(1) * heads + hh
        sel = (r >= CUM_SPLIT * hidx) & (r < CUM_SPLIT * (hidx + 1))
        onehot_t.append(jnp.where(sel, 1.0, 0.0).astype(BF16))

    def scores(hh, i):
        nk = (i + 1) * blk
        cols = slice(hh * hd, (hh + 1) * hd)
        q_aug = jnp.concatenate([qt_ref[cols, i * blk:nk], onehot_t[hh]], axis=0)
        k_aug = jnp.concatenate([k_ref[0, :nk, cols], aug_ref[0, :nk, :]], axis=1)
        return jnp.dot(k_aug, q_aug, preferred_element_type=F32)

    s_next = [scores(hh, 0) for hh in range(heads)]
    for i in range(nblk):
        qs = slice(i * blk, (i + 1) * blk)
        nk = (i + 1) * blk
        s_cur = s_next
        if i + 1 < nblk:
            s_next = [scores(hh, i + 1) for hh in range(heads)]
        for hh in range(heads):
            cols = slice(hh * hd, (hh + 1) * hd)
            s = s_cur[hh]
            s_diag = jnp.where(causal, s[nk - blk:, :], MASK_VALUE)
            s = s_diag if i == 0 else jnp.concatenate([s[:nk - blk, :], s_diag], axis=0)
            m = jnp.max(s, axis=0, keepdims=True)
            p = jnp.exp2(s - m)
            l = jnp.sum(p, axis=0, keepdims=True)
            acc = jnp.dot(vt_ref[cols, :nk], p.astype(BF16), preferred_element_type=F32)
            o_ref[0, qs, cols] = (acc * (1.0 / l)).T.astype(o_ref.dtype)


def _fox(feat_t, tok3, aug3, *, blk=256, heads=4):
    bsz, seq, _ = tok3.shape
    hw = heads * HEAD_DIM
    qb, vb = ROW_FQ // hw, ROW_FV // hw
    kern = functools.partial(_fox_kernel, blk=blk, heads=heads)
    return pl.pallas_call(
        kern,
        grid=(bsz, N_FOX_HEADS // heads),
        in_specs=[
            pl.BlockSpec((hw, seq), lambda b, h: (qb + h, b)),
            pl.BlockSpec((1, seq, hw), lambda b, h: (b, 0, h)),
            pl.BlockSpec((1, seq, LANES), lambda b, h: (b, 0, 0)),
            pl.BlockSpec((hw, seq), lambda b, h: (vb + h, b)),
        ],
        out_specs=pl.BlockSpec((1, seq, hw), lambda b, h: (b, 0, h)),
        out_shape=jax.ShapeDtypeStruct((bsz, seq, FOX_W), BF16),
        compiler_params=pltpu.CompilerParams(
            dimension_semantics=("arbitrary", "arbitrary"),
            vmem_limit_bytes=V7X_VMEM_LIMIT),
        name="fox",
    )(feat_t, tok3, aug3, feat_t)


def _swa_kernel(qt_ref, k_ref, vt_ref, sink_ref, o_ref):
    g = pl.program_id(1)
    w, hd = SWA_WINDOW, HEAD_DIM
    seq = k_ref.shape[1]
    gl = SWA_GROUP * w
    sink = jnp.concatenate(
        [jnp.full((1, w), sink_ref[g * SWA_GROUP + r] * LOG2E, F32) for r in range(SWA_GROUP)],
        axis=1)
    kk = lax.broadcasted_iota(jnp.int32, (2 * w, gl), 0)
    qq = lax.broadcasted_iota(jnp.int32, (2 * w, gl), 1) & (w - 1)
    band = (kk > qq) & (kk - w <= qq)
    band_first = band[w:, :]

    def scores(n):
        q_t = jnp.concatenate(
            [qt_ref[r * hd:(r + 1) * hd, n * w:(n + 1) * w] for r in range(SWA_GROUP)],
            axis=1)
        ks = slice(max(n - 1, 0) * w, (n + 1) * w)
        return jnp.dot(k_ref[0, ks, :], q_t, preferred_element_type=F32)

    nblk = seq // w
    s_next = scores(0)
    for n in range(nblk):
        ks = slice(max(n - 1, 0) * w, (n + 1) * w)
        s = s_next
        if n + 1 < nblk:
            s_next = scores(n + 1)
        s = jnp.where(band if n > 0 else band_first, s, MASK_VALUE)
        m = jnp.maximum(jnp.max(s, axis=0, keepdims=True), sink)
        p = jnp.exp2(s - m)
        denom = jnp.sum(p, axis=0, keepdims=True) + jnp.exp2(sink - m)
        o_t = jnp.dot(vt_ref[:, ks], p.astype(BF16), preferred_element_type=F32)
        o_t = o_t * (1.0 / denom)
        for r in range(SWA_GROUP):
            o_ref[0, n * w:(n + 1) * w, r * hd:(r + 1) * hd] = (
                o_t[:, r * w:(r + 1) * w].T.astype(o_ref.dtype))


def _swa(feat_t, tok3, sinks):
    bsz, seq, _ = tok3.shape
    hd = HEAD_DIM
    gw = SWA_GROUP * hd
    qb, kb, vb = ROW_SQ // gw, FOX_W // hd, ROW_SV // hd
    return pl.pallas_call(
        _swa_kernel,
        grid=(bsz, N_SWA_KV_HEADS),
        in_specs=[
            pl.BlockSpec((gw, seq), lambda b, g: (qb + g, b)),
            pl.BlockSpec((1, seq, hd), lambda b, g: (b, 0, kb + g)),
            pl.BlockSpec((hd, seq), lambda b, g: (vb + g, b)),
            pl.BlockSpec(memory_space=pltpu.SMEM),
        ],
        out_specs=pl.BlockSpec((1, seq, gw), lambda b, g: (b, 0, g)),
        out_shape=jax.ShapeDtypeStruct((bsz, seq, SWA_Q_W), BF16),
        compiler_params=pltpu.CompilerParams(
            dimension_semantics=("arbitrary", "arbitrary"),
            vmem_limit_bytes=V7X_VMEM_LIMIT),
        name="swa",
    )(feat_t, tok3, feat_t, sinks)


OUTPROJ_ROW_PARTS = 2


def _outproj_kernel(fox_ref, swa_ref, x_ref, mod_ref, g_ref, wf_ref, ws_ref, o_ref):
    part = x_ref.shape[0] // OUTPROJ_ROW_PARTS
    gate = mod_ref[0, 2:3, :] * g_ref[...]
    for a in range(OUTPROJ_ROW_PARTS):
        rows = slice(a * part, (a + 1) * part)
        o_ref[rows, :] = (
            jnp.dot(fox_ref[rows, :], wf_ref[...], preferred_element_type=F32)
            + jnp.dot(swa_ref[rows, :], ws_ref[...], preferred_element_type=F32))
    for a in range(OUTPROJ_ROW_PARTS):
        _norm_gate_residual(o_ref, x_ref, gate, a * part, part)


def _outproj(fox2, swa2, x2, mod3, g, w_out_bf, *, seq, tm=1024):
    t, d = x2.shape
    tiles_per_seq = seq // tm
    return pl.pallas_call(
        _outproj_kernel,
        grid=(t // tm,),
        in_specs=[
            pl.BlockSpec((tm, FOX_W), lambda i: (i, 0)),
            pl.BlockSpec((tm, SWA_Q_W), lambda i: (i, 0)),
            pl.BlockSpec((tm, d), lambda i: (i, 0)),
            pl.BlockSpec((1, N_MOD, d), lambda i: (i // tiles_per_seq, 0, 0)),
            _resident((1, d), lambda i: (0, 0)),
            _resident((FOX_W, d), lambda i: (0, 0)),
            _resident((SWA_Q_W, d), lambda i: (FOX_W // SWA_Q_W, 0)),
        ],
        out_specs=pl.BlockSpec((tm, d), lambda i: (i, 0)),
        out_shape=jax.ShapeDtypeStruct((t, d), F32),
        compiler_params=pltpu.CompilerParams(
            dimension_semantics=("arbitrary",), vmem_limit_bytes=V7X_VMEM_LIMIT),
        name="out_proj",
    )(fox2, swa2, x2, mod3, g.reshape(1, d), w_out_bf, w_out_bf)


MLP_ROW_PARTS = 2


def _mlp_kernel(x_ref, mod_ref, gpre_ref, gpost_ref, wu_ref, wd_ref, o_ref, h_scr):
    j = pl.program_id(1)
    last = pl.num_programs(1) - 1
    part = x_ref.shape[0] // MLP_ROW_PARTS

    def step(first, final):
        if first:
            gain = gpre_ref[...] * (1.0 + mod_ref[0, 4:5, :])
            shift = mod_ref[0, 3:4, :]
        if final:
            gate = mod_ref[0, 5:6, :] * gpost_ref[...]
        ups = []
        for a in range(MLP_ROW_PARTS):
            rows = slice(a * part, (a + 1) * part)
            if first:
                _norm_modulate(x_ref, gain, shift, h_scr, a * part, part)
            ups.append(jnp.dot(h_scr[rows, :], wu_ref[...], preferred_element_type=F32))
        for a in range(MLP_ROW_PARTS):
            rows = slice(a * part, (a + 1) * part)
            act = jnp.square(jnp.maximum(ups[a], 0.0)).astype(BF16)
            y = jnp.dot(act, wd_ref[...], preferred_element_type=F32)
            if first:
                o_ref[rows, :] = y
            else:
                o_ref[rows, :] += y
            if final:
                _norm_gate_residual(o_ref, x_ref, gate, a * part, part)

    pl.when(j == 0)(lambda: step(True, False))
    pl.when((j > 0) & (j < last))(lambda: step(False, False))
    pl.when(j == last)(lambda: step(False, True))


def _mlp(x2, mod3, g_pre, g_post, w_up, w_down, *, seq, tm=1024, tf=1024):
    t, d = x2.shape
    ff = w_up.shape[1]
    tiles_per_seq = seq // tm
    return pl.pallas_call(
        _mlp_kernel,
        grid=(t // tm, ff // tf),
        in_specs=[
            pl.BlockSpec((tm, d), lambda i, j: (i, 0)),
            pl.BlockSpec((1, N_MOD, d), lambda i, j: (i // tiles_per_seq, 0, 0)),
            pl.BlockSpec((1, d), lambda i, j: (0, 0)),
            pl.BlockSpec((1, d), lambda i, j: (0, 0)),
            pl.BlockSpec((d, tf), lambda i, j: (0, j)),
            pl.BlockSpec((tf, d), lambda i, j: (j, 0)),
        ],
        out_specs=pl.BlockSpec((tm, d), lambda i, j: (i, 0)),
        out_shape=jax.ShapeDtypeStruct((t, d), F32),
        scratch_shapes=[pltpu.VMEM((tm, d), BF16)],
        compiler_params=pltpu.CompilerParams(
            dimension_semantics=("arbitrary", "arbitrary"),
            vmem_limit_bytes=V7X_VMEM_LIMIT),
        name="mlp",
    )(x2, mod3, g_pre.reshape(1, d), g_post.reshape(1, d), w_up, w_down)


def _rope_tables(seq):
    half = HEAD_DIM // 2
    inv_freq = 1.0 / (ROPE_THETA ** (jnp.arange(half, dtype=F32) * (2.0 / HEAD_DIM)))
    ang = jnp.arange(seq).astype(F32)[:, None] * inv_freq[None, :]
    cos, sin = jnp.cos(ang), jnp.sin(ang)
    return (jnp.concatenate([cos, cos], axis=-1), jnp.concatenate([-sin, sin], axis=-1),
            cos.T, sin.T)


def kernel(x, c, w_mod, b_mod, g_pre_mix, g_post_mix, w_in, b_forget, swa_sinks,
           w_out, g_pre_mlp, g_post_mlp, w_up, w_down):
    bsz, seq, d = x.shape
    depth = w_mod.shape[0]
    t = bsz * seq
    tm_in = 512
    cos, sin_signed, cos_t, sin_t = _rope_tables(seq)
    phase = (jnp.arange(LANES, dtype=jnp.int32) % CUM_SPLIT).reshape(1, LANES)
    tri = jnp.tril(jnp.ones((tm_in, tm_in), BF16))
    o_fk, o_fv, o_fg = FOX_W, 2 * FOX_W, 3 * FOX_W
    o_sq = o_fg + N_FOX_HEADS
    o_sk = o_sq + SWA_Q_W
    o_sv = o_sk + SWA_KV_W
    n_rep = CUM_SPLIT * N_FOX_HEADS

    x2 = x.reshape(t, d)
    for l in range(depth):
        wi = w_in[l]
        w_tok = jnp.concatenate([wi[:, o_fk:o_fv], wi[:, o_sk:o_sv]], axis=1).astype(BF16)
        w_feat_t = jnp.concatenate(
            [wi[:, :o_fk], wi[:, o_fv:o_fg], wi[:, o_sq:o_sk], wi[:, o_sv:]],
            axis=1).T.astype(BF16)
        w_fg = jnp.zeros((d, LANES), F32).at[:, :n_rep].set(
            jnp.repeat(wi[:, o_fg:o_sq], CUM_SPLIT, axis=1)).astype(BF16)
        b_fg = jnp.zeros((1, LANES), F32).at[0, :n_rep].set(
            jnp.repeat(b_forget[l].astype(F32), CUM_SPLIT))

        mod3 = _mod(c, w_mod[l], b_mod[l]).reshape(bsz, N_MOD, d)
        tok, aug, feat_t, w_o, w_u, w_d = _inproj(
            x2, mod3, g_pre_mix[l], w_tok, w_feat_t, w_fg, b_fg, phase, tri, cos, sin_signed,
            cos_t, sin_t, (w_out[l], w_up[l], w_down[l]), seq=seq, tm=tm_in)
        tok3 = tok.reshape(bsz, seq, TOK_W)
        fox = _fox(feat_t, tok3, aug.reshape(bsz, seq, LANES))
        swa = _swa(feat_t, tok3, swa_sinks[l])
        x2 = _outproj(fox.reshape(t, FOX_W), swa.reshape(t, SWA_Q_W), x2, mod3,
                      g_post_mix[l], w_o, seq=seq)
        x2 = _mlp(x2, mod3, g_pre_mlp[l], g_post_mlp[l], w_u, w_d, seq=seq)
    return x2.reshape(bsz, seq, d)
```

```python
import functools
import math

import jax
import jax.numpy as jnp
from jax import lax
from jax.experimental import pallas as pl
from jax.experimental.pallas import tpu as pltpu

HEAD_DIM = 128
N_FOX_HEADS = 8
N_SWA_HEADS = 8
N_SWA_KV_HEADS = 2
SWA_GROUP = N_SWA_HEADS // N_SWA_KV_HEADS
SWA_WINDOW = 128
ROPE_THETA = 10000.0
NORM_EPS = 1e-6
N_MOD = 6
MASK_VALUE = -1e30
LOG2E = math.log2(math.e)
Q_SCALE = HEAD_DIM ** -0.5 * LOG2E

FOX_W = N_FOX_HEADS * HEAD_DIM
SWA_Q_W = N_SWA_HEADS * HEAD_DIM
SWA_KV_W = N_SWA_KV_HEADS * HEAD_DIM
TOK_W = FOX_W + SWA_KV_W
ROW_FQ = 0
ROW_FV = FOX_W
ROW_SQ = 2 * FOX_W
ROW_SV = ROW_SQ + SWA_Q_W
FEAT_W = ROW_SV + SWA_KV_W
CUM_SPLIT = 3
LANES = 128

V7X_VMEM_LIMIT = 60 * 1024 * 1024

BF16 = jnp.bfloat16
F32 = jnp.float32
_NT = (((1,), (1,)), ((), ()))
_TN = (((0,), (0,)), ((), ()))


NORM_ROWS = 16


def _inv_rms(x):
    return lax.rsqrt(jnp.mean(x * x, axis=-1, keepdims=True) + NORM_EPS)


def _norm_modulate(x_ref, gain_row, shift_row, out_ref, row0=0, nrows=None):
    nrows = x_ref.shape[0] if nrows is None else nrows
    for c in range(nrows // NORM_ROWS):
        rows = slice(row0 + c * NORM_ROWS, row0 + (c + 1) * NORM_ROWS)
        x = x_ref[rows, :]
        out_ref[rows, :] = ((x * _inv_rms(x)) * gain_row + shift_row).astype(out_ref.dtype)


def _norm_gate_residual(y_ref, x_ref, gate_row, row0=0, nrows=None):
    nrows = x_ref.shape[0] if nrows is None else nrows
    for c in range(nrows // NORM_ROWS):
        rows = slice(row0 + c * NORM_ROWS, row0 + (c + 1) * NORM_ROWS)
        y = y_ref[rows, :]
        y_ref[rows, :] = x_ref[rows, :] + (y * _inv_rms(y)) * gate_row


def _resident(shape, index_map):
    return pl.BlockSpec(shape, index_map, pipeline_mode=pl.Buffered(1))


def _mod_kernel(c_ref, w_ref, b_ref, o_ref):
    c = c_ref[...]
    cond = c * jax.nn.sigmoid(c)
    o_ref[...] = jnp.dot(cond.astype(BF16), w_ref[...].astype(BF16),
                         preferred_element_type=F32) + b_ref[...]


def _mod(c, w, b, *, tn=1024):
    bsz, d = c.shape
    n = w.shape[1]
    return pl.pallas_call(
        _mod_kernel,
        grid=(n // tn,),
        in_specs=[pl.BlockSpec((bsz, d), lambda j: (0, 0)),
                  pl.BlockSpec((d, tn), lambda j: (0, j)),
                  pl.BlockSpec((1, tn), lambda j: (0, j))],
        out_specs=pl.BlockSpec((bsz, tn), lambda j: (0, j)),
        out_shape=jax.ShapeDtypeStruct((bsz, n), F32),
        compiler_params=pltpu.CompilerParams(
            dimension_semantics=("arbitrary",), vmem_limit_bytes=V7X_VMEM_LIMIT),
        name="mod",
    )(c, w, b.reshape(1, n))


def _inproj_kernel(x_ref, mod_ref, g_ref, wfk_ref, wsk_ref, wfq_ref, wfv_ref, wsq_ref, wsv_ref,
                   wfg_ref, bfg_ref, phase_ref,
                   tri_ref, cos_ref, sin_ref, cost_ref, sint_ref, wo_ref, wu_ref, wd_ref,
                   tok_ref, aug_ref, feat_ref, wo_bf_ref, wu_bf_ref, wd_bf_ref,
                   h_scr, carry_scr, *, tiles_per_seq):
    i = pl.program_id(0)
    hd = HEAD_DIM
    half = hd // 2

    wo_bf_ref[...] = wo_ref[...].astype(BF16)
    wu_bf_ref[...] = wu_ref[...].astype(BF16)
    wd_bf_ref[...] = wd_ref[...].astype(BF16)

    _norm_modulate(x_ref, g_ref[...] * (1.0 + mod_ref[0, 1:2, :]), mod_ref[0, 0:1, :], h_scr)

    tok_ref[:, :FOX_W] = jnp.dot(h_scr[...], wfk_ref[...],
                                 preferred_element_type=F32).astype(BF16)
    sk = jnp.dot(h_scr[...], wsk_ref[...], preferred_element_type=F32)
    cos, sin_signed = cos_ref[...], sin_ref[...]
    for r in range(N_SWA_KV_HEADS):
        blk = sk[:, r * hd:(r + 1) * hd]
        tok_ref[:, FOX_W + r * hd:FOX_W + (r + 1) * hd] = (
            blk * cos + pltpu.roll(blk, half, 1) * sin_signed).astype(BF16)

    z = jnp.dot(h_scr[...], wfg_ref[...], preferred_element_type=F32) + bfg_ref[...]
    logf = jnp.minimum(z, 0.0) - jnp.log1p(jnp.exp(-jnp.abs(z)))
    p0 = logf.astype(BF16)
    r0 = logf - p0.astype(F32)
    p1 = r0.astype(BF16)
    p2 = (r0 - p1.astype(F32)).astype(BF16)
    cs = jnp.dot(tri_ref[...], jnp.concatenate([p0, p1, p2], axis=1),
                 preferred_element_type=F32)

    @pl.when(i % tiles_per_seq == 0)
    def _():
        carry_scr[...] = jnp.zeros_like(carry_scr)

    cum = cs[:, :LANES] + cs[:, LANES:2 * LANES] + cs[:, 2 * LANES:] + carry_scr[...]
    carry_scr[...] = cum[cum.shape[0] - 1:, :]
    negc = cum * (-LOG2E)
    hi = negc.astype(BF16).astype(F32)
    rem = negc - hi
    mid = rem.astype(BF16).astype(F32)
    lo = rem - mid
    ph = phase_ref[...]
    aug_ref[...] = jnp.where(ph == 0, hi, jnp.where(ph == 1, mid, lo)).astype(BF16)

    def feat(w_ref):
        return lax.dot_general(w_ref[...], h_scr[...], _NT, preferred_element_type=F32)

    feat_ref[ROW_FQ:ROW_FQ + FOX_W, :] = (feat(wfq_ref) * Q_SCALE).astype(BF16)
    feat_ref[ROW_FV:ROW_FV + FOX_W, :] = feat(wfv_ref).astype(BF16)
    sq = feat(wsq_ref)
    cos_t, sin_t = cost_ref[...], sint_ref[...]
    for hh in range(N_SWA_HEADS):
        x1 = sq[hh * hd:hh * hd + half, :]
        x2 = sq[hh * hd + half:(hh + 1) * hd, :]
        base = ROW_SQ + hh * hd
        feat_ref[base:base + half, :] = ((x1 * cos_t - x2 * sin_t) * Q_SCALE).astype(BF16)
        feat_ref[base + half:base + hd, :] = ((x2 * cos_t + x1 * sin_t) * Q_SCALE).astype(BF16)
    feat_ref[ROW_SV:, :] = feat(wsv_ref).astype(BF16)


def _inproj(x2, mod3, g, proj_weights, w_fg, b_fg, phase, tri, cos, sin_signed,
            cos_t, sin_t, later_weights, *, seq, tm):
    t, d = x2.shape
    steps = t // tm
    tiles_per_seq = seq // tm
    half = HEAD_DIM // 2
    kern = functools.partial(_inproj_kernel, tiles_per_seq=tiles_per_seq)
    const = lambda i: (0, 0)
    pos = lambda i: (i % tiles_per_seq, 0)
    pos_t = lambda i: (0, i % tiles_per_seq)
    slab = lambda i: (i, 0)
    slab_specs = [pl.BlockSpec((w.shape[0] // steps, w.shape[1]), slab) for w in later_weights]
    return pl.pallas_call(
        kern,
        grid=(steps,),
        in_specs=[
            pl.BlockSpec((tm, d), lambda i: (i, 0)),
            pl.BlockSpec((1, N_MOD, d), lambda i: (i // tiles_per_seq, 0, 0)),
            _resident((1, d), const),
        ] + [_resident(w.shape, const) for w in proj_weights] + [
            _resident((d, LANES), const),
            _resident((1, LANES), const),
            _resident((1, LANES), const),
            _resident((tm, tm), const),
            pl.BlockSpec((tm, HEAD_DIM), pos),
            pl.BlockSpec((tm, HEAD_DIM), pos),
            pl.BlockSpec((half, tm), pos_t),
            pl.BlockSpec((half, tm), pos_t),
        ] + slab_specs,
        out_specs=[
            pl.BlockSpec((tm, TOK_W), lambda i: (i, 0)),
            pl.BlockSpec((tm, LANES), lambda i: (i, 0)),
            pl.BlockSpec((FEAT_W, tm), lambda i: (0, i)),
        ] + slab_specs,
        out_shape=[jax.ShapeDtypeStruct((t, TOK_W), BF16),
                   jax.ShapeDtypeStruct((t, LANES), BF16),
                   jax.ShapeDtypeStruct((FEAT_W, t), BF16)]
        + [jax.ShapeDtypeStruct(w.shape, BF16) for w in later_weights],
        scratch_shapes=[pltpu.VMEM((tm, d), BF16), pltpu.VMEM((1, LANES), F32)],
        compiler_params=pltpu.CompilerParams(
            dimension_semantics=("arbitrary",), vmem_limit_bytes=V7X_VMEM_LIMIT),
        name="in_proj",
    )(x2, mod3, g.reshape(1, d), *proj_weights, w_fg, b_fg, phase, tri, cos, sin_signed,
      cos_t, sin_t, *later_weights)


def _fox_kernel(qt_ref, k_ref, aug_ref, vt_ref, o_ref, *, blk, heads):
    hd = HEAD_DIM
    seq = k_ref.shape[1]
    nblk = seq // blk
    r = lax.broadcasted_iota(jnp.int32, (LANES, blk), 0)
    kk = lax.broadcasted_iota(jnp.int32, (blk, blk), 0)
    qq = lax.broadcasted_iota(jnp.int32, (blk, blk), 1)
    causal = kk <= qq
    onehot_t = []
    for hh in range(heads):
        hidx = pl.program_id(1) * heads + hh
        sel = (r >= CUM_SPLIT * hidx) & (r < CUM_SPLIT * (hidx + 1))
        onehot_t.append(jnp.where(sel, 1.0, 0.0).astype(BF16))

    def scores(hh, i):
        nk = (i + 1) * blk
        cols = slice(hh * hd, (hh + 1) * hd)
        q_aug = jnp.concatenate([qt_ref[cols, i * blk:nk], onehot_t[hh]], axis=0)
        k_aug = jnp.concatenate([k_ref[0, :nk, cols], aug_ref[0, :nk, :]], axis=1)
        return jnp.dot(k_aug, q_aug, preferred_element_type=F32)

    s_next = [scores(hh, 0) for hh in range(heads)]
    for i in range(nblk):
        qs = slice(i * blk, (i + 1) * blk)
        nk = (i + 1) * blk
        s_cur = s_next
        if i + 1 < nblk:
            s_next = [scores(hh, i + 1) for hh in range(heads)]
        for hh in range(heads):
            cols = slice(hh * hd, (hh + 1) * hd)
            s = s_cur[hh]
            s_diag = jnp.where(causal, s[nk - blk:, :], MASK_VALUE)
            s = s_diag if i == 0 else jnp.concatenate([s[:nk - blk, :], s_diag], axis=0)
            m = jnp.max(s, axis=0, keepdims=True)
            p = jnp.exp2(s - m)
            l = jnp.sum(p, axis=0, keepdims=True)
            acc = jnp.dot(vt_ref[cols, :nk], p.astype(BF16), preferred_element_type=F32)
            o_ref[cols, qs] = (acc * (1.0 / l)).astype(o_ref.dtype)


def _fox(feat_t, tok3, aug3, *, blk=256, heads=4):
    bsz, seq, _ = tok3.shape
    hw = heads * HEAD_DIM
    qb, vb = ROW_FQ // hw, ROW_FV // hw
    kern = functools.partial(_fox_kernel, blk=blk, heads=heads)
    return pl.pallas_call(
        kern,
        grid=(bsz, N_FOX_HEADS // heads),
        in_specs=[
            pl.BlockSpec((hw, seq), lambda b, h: (qb + h, b)),
            pl.BlockSpec((1, seq, hw), lambda b, h: (b, 0, h)),
            pl.BlockSpec((1, seq, LANES), lambda b, h: (b, 0, 0)),
            pl.BlockSpec((hw, seq), lambda b, h: (vb + h, b)),
        ],
        out_specs=pl.BlockSpec((hw, seq), lambda b, h: (h, b)),
        out_shape=jax.ShapeDtypeStruct((FOX_W, bsz * seq), BF16),
        compiler_params=pltpu.CompilerParams(
            dimension_semantics=("arbitrary", "arbitrary"),
            vmem_limit_bytes=V7X_VMEM_LIMIT),
        name="fox",
    )(feat_t, tok3, aug3, feat_t)


def _swa_kernel(qt_ref, k_ref, vt_ref, sink_ref, o_ref):
    w, hd = SWA_WINDOW, HEAD_DIM
    seq = k_ref.shape[1]
    gl = SWA_GROUP * w
    gw = SWA_GROUP * hd
    groups = range(N_SWA_KV_HEADS)
    sinks = [jnp.concatenate(
        [jnp.full((1, w), sink_ref[g * SWA_GROUP + r] * LOG2E, F32) for r in range(SWA_GROUP)],
        axis=1) for g in groups]
    kk = lax.broadcasted_iota(jnp.int32, (2 * w, gl), 0)
    qq = lax.broadcasted_iota(jnp.int32, (2 * w, gl), 1) & (w - 1)
    band = (kk > qq) & (kk - w <= qq)
    bias = jnp.where(band, 0.0, MASK_VALUE)
    bias_first = bias[w:, :]

    def scores(g, n):
        q_t = jnp.concatenate(
            [qt_ref[g * gw + r * hd:g * gw + (r + 1) * hd, n * w:(n + 1) * w]
             for r in range(SWA_GROUP)], axis=1)
        ks = slice(max(n - 1, 0) * w, (n + 1) * w)
        s = jnp.dot(k_ref[0, ks, g * hd:(g + 1) * hd], q_t,
                    preferred_element_type=F32)
        return s + (bias if n > 0 else bias_first)

    nblk = seq // w
    s_next = [scores(g, 0) for g in groups]
    for n in range(nblk):
        ks = slice(max(n - 1, 0) * w, (n + 1) * w)
        s_cur = s_next
        if n + 1 < nblk:
            s_next = [scores(g, n + 1) for g in groups]
        for g in groups:
            s = s_cur[g]
            m = jnp.maximum(jnp.max(s, axis=0, keepdims=True), sinks[g])
            p = jnp.exp2(s - m)
            denom = jnp.sum(p, axis=0, keepdims=True) + jnp.exp2(sinks[g] - m)
            o_t = jnp.dot(vt_ref[g * hd:(g + 1) * hd, ks], p.astype(BF16),
                          preferred_element_type=F32)
            o_t = o_t * (1.0 / denom)
            for r in range(SWA_GROUP):
                col = g * gw + r * hd
                o_ref[col:col + hd, n * w:(n + 1) * w] = (
                    o_t[:, r * w:(r + 1) * w].astype(o_ref.dtype))


def _swa(feat_t, tok3, sinks):
    bsz, seq, _ = tok3.shape
    qb, kb, vb = ROW_SQ // SWA_Q_W, FOX_W // SWA_KV_W, ROW_SV // SWA_KV_W
    return pl.pallas_call(
        _swa_kernel,
        grid=(bsz,),
        in_specs=[
            pl.BlockSpec((SWA_Q_W, seq), lambda b: (qb, b)),
            pl.BlockSpec((1, seq, SWA_KV_W), lambda b: (b, 0, kb)),
            pl.BlockSpec((SWA_KV_W, seq), lambda b: (vb, b)),
            pl.BlockSpec(memory_space=pltpu.SMEM),
        ],
        out_specs=pl.BlockSpec((SWA_Q_W, seq), lambda b: (0, b)),
        out_shape=jax.ShapeDtypeStruct((SWA_Q_W, bsz * seq), BF16),
        compiler_params=pltpu.CompilerParams(
            dimension_semantics=("arbitrary",), vmem_limit_bytes=V7X_VMEM_LIMIT),
        name="swa",
    )(feat_t, tok3, feat_t, sinks)


OUTPROJ_ROW_PARTS = 2


def _outproj_kernel(fox_ref, swa_ref, x_ref, mod_ref, g_ref, wf_ref, ws_ref, o_ref):
    part = x_ref.shape[0] // OUTPROJ_ROW_PARTS
    gate = mod_ref[0, 2:3, :] * g_ref[...]
    for a in range(OUTPROJ_ROW_PARTS):
        rows = slice(a * part, (a + 1) * part)
        o_ref[rows, :] = (
            lax.dot_general(fox_ref[:, rows], wf_ref[...], _TN, preferred_element_type=F32)
            + lax.dot_general(swa_ref[:, rows], ws_ref[...], _TN, preferred_element_type=F32))
    for a in range(OUTPROJ_ROW_PARTS):
        _norm_gate_residual(o_ref, x_ref, gate, a * part, part)


def _outproj(fox2, swa2, x2, mod3, g, w_out_bf, *, seq, tm=1024):
    t, d = x2.shape
    tiles_per_seq = seq // tm
    return pl.pallas_call(
        _outproj_kernel,
        grid=(t // tm,),
        in_specs=[
            pl.BlockSpec((FOX_W, tm), lambda i: (0, i)),
            pl.BlockSpec((SWA_Q_W, tm), lambda i: (0, i)),
            pl.BlockSpec((tm, d), lambda i: (i, 0)),
            pl.BlockSpec((1, N_MOD, d), lambda i: (i // tiles_per_seq, 0, 0)),
            _resident((1, d), lambda i: (0, 0)),
            _resident((FOX_W, d), lambda i: (0, 0)),
            _resident((SWA_Q_W, d), lambda i: (FOX_W // SWA_Q_W, 0)),
        ],
        out_specs=pl.BlockSpec((tm, d), lambda i: (i, 0)),
        out_shape=jax.ShapeDtypeStruct((t, d), F32),
        compiler_params=pltpu.CompilerParams(
            dimension_semantics=("arbitrary",), vmem_limit_bytes=V7X_VMEM_LIMIT),
        name="out_proj",
    )(fox2, swa2, x2, mod3, g.reshape(1, d), w_out_bf, w_out_bf)


MLP_ROW_PARTS = 2


def _mlp_kernel(x_ref, mod_ref, gpre_ref, gpost_ref, wu_ref, wd_ref, o_ref, h_scr):
    j = pl.program_id(1)
    last = pl.num_programs(1) - 1
    part = x_ref.shape[0] // MLP_ROW_PARTS

    def step(first, final):
        if first:
            gain = gpre_ref[...] * (1.0 + mod_ref[0, 4:5, :])
            shift = mod_ref[0, 3:4, :]
        if final:
            gate = mod_ref[0, 5:6, :] * gpost_ref[...]
        ups = []
        for a in range(MLP_ROW_PARTS):
            rows = slice(a * part, (a + 1) * part)
            if first:
                _norm_modulate(x_ref, gain, shift, h_scr, a * part, part)
            ups.append(jnp.dot(h_scr[rows, :], wu_ref[...], preferred_element_type=F32))
        for a in range(MLP_ROW_PARTS):
            rows = slice(a * part, (a + 1) * part)
            act = jnp.square(jnp.maximum(ups[a], 0.0)).astype(BF16)
            y = jnp.dot(act, wd_ref[...], preferred_element_type=F32)
            if first:
                o_ref[rows, :] = y
            else:
                o_ref[rows, :] += y
            if final:
                _norm_gate_residual(o_ref, x_ref, gate, a * part, part)

    pl.when(j == 0)(lambda: step(True, False))
    pl.when((j > 0) & (j < last))(lambda: step(False, False))
    pl.when(j == last)(lambda: step(False, True))


def _mlp(x2, mod3, g_pre, g_post, w_up, w_down, *, seq, tm=1024, tf=1024):
    t, d = x2.shape
    ff = w_up.shape[1]
    tiles_per_seq = seq // tm
    return pl.pallas_call(
        _mlp_kernel,
        grid=(t // tm, ff // tf),
        in_specs=[
            pl.BlockSpec((tm, d), lambda i, j: (i, 0)),
            pl.BlockSpec((1, N_MOD, d), lambda i, j: (i // tiles_per_seq, 0, 0)),
            pl.BlockSpec((1, d), lambda i, j: (0, 0)),
            pl.BlockSpec((1, d), lambda i, j: (0, 0)),
            pl.BlockSpec((d, tf), lambda i, j: (0, j)),
            pl.BlockSpec((tf, d), lambda i, j: (j, 0)),
        ],
        out_specs=pl.BlockSpec((tm, d), lambda i, j: (i, 0)),
        out_shape=jax.ShapeDtypeStruct((t, d), F32),
        scratch_shapes=[pltpu.VMEM((tm, d), BF16)],
        compiler_params=pltpu.CompilerParams(
            dimension_semantics=("arbitrary", "arbitrary"),
            vmem_limit_bytes=V7X_VMEM_LIMIT),
        name="mlp",
    )(x2, mod3, g_pre.reshape(1, d), g_post.reshape(1, d), w_up, w_down)


def _rope_tables(seq):
    half = HEAD_DIM // 2
    inv_freq = 1.0 / (ROPE_THETA ** (jnp.arange(half, dtype=F32) * (2.0 / HEAD_DIM)))
    ang = jnp.arange(seq).astype(F32)[:, None] * inv_freq[None, :]
    cos, sin = jnp.cos(ang), jnp.sin(ang)
    return (jnp.concatenate([cos, cos], axis=-1), jnp.concatenate([-sin, sin], axis=-1),
            cos.T, sin.T)


def kernel(x, c, w_mod, b_mod, g_pre_mix, g_post_mix, w_in, b_forget, swa_sinks,
           w_out, g_pre_mlp, g_post_mlp, w_up, w_down):
    bsz, seq, d = x.shape
    depth = w_mod.shape[0]
    t = bsz * seq
    tm_in = 512
    cos, sin_signed, cos_t, sin_t = _rope_tables(seq)
    phase = (jnp.arange(LANES, dtype=jnp.int32) % CUM_SPLIT).reshape(1, LANES)
    tri = jnp.tril(jnp.ones((tm_in, tm_in), BF16))
    o_fk, o_fv, o_fg = FOX_W, 2 * FOX_W, 3 * FOX_W
    o_sq = o_fg + N_FOX_HEADS
    o_sk = o_sq + SWA_Q_W
    o_sv = o_sk + SWA_KV_W
    n_rep = CUM_SPLIT * N_FOX_HEADS

    x2 = x.reshape(t, d)
    for l in range(depth):
        wi = w_in[l]
        proj_weights = (
            wi[:, o_fk:o_fv].astype(BF16), wi[:, o_sk:o_sv].astype(BF16),
            wi[:, :o_fk].T.astype(BF16), wi[:, o_fv:o_fg].T.astype(BF16),
            wi[:, o_sq:o_sk].T.astype(BF16), wi[:, o_sv:].T.astype(BF16))
        w_fg = jnp.zeros((d, LANES), F32).at[:, :n_rep].set(
            jnp.repeat(wi[:, o_fg:o_sq], CUM_SPLIT, axis=1)).astype(BF16)
        b_fg = jnp.zeros((1, LANES), F32).at[0, :n_rep].set(
            jnp.repeat(b_forget[l].astype(F32), CUM_SPLIT))

        mod3 = _mod(c, w_mod[l], b_mod[l]).reshape(bsz, N_MOD, d)
        tok, aug, feat_t, w_o, w_u, w_d = _inproj(
            x2, mod3, g_pre_mix[l], proj_weights, w_fg, b_fg, phase, tri, cos, sin_signed,
            cos_t, sin_t, (w_out[l], w_up[l], w_down[l]), seq=seq, tm=tm_in)
        tok3 = tok.reshape(bsz, seq, TOK_W)
        fox = _fox(feat_t, tok3, aug.reshape(bsz, seq, LANES))
        swa = _swa(feat_t, tok3, swa_sinks[l])
        x2 = _outproj(fox, swa, x2, mod3,
                      g_post_mix[l], w_o, seq=seq)
        x2 = _mlp(x2, mod3, g_pre_mlp[l], g_post_mlp[l], w_u, w_d, seq=seq)
    return x2.reshape(bsz, seq, d)
```

```python
import functools
import math

import jax
import jax.numpy as jnp
from jax import lax
from jax.experimental import pallas as pl
from jax.experimental.pallas import tpu as pltpu

HEAD_DIM = 128
N_FOX_HEADS = 8
N_SWA_HEADS = 8
N_SWA_KV_HEADS = 2
SWA_GROUP = N_SWA_HEADS // N_SWA_KV_HEADS
SWA_WINDOW = 128
ROPE_THETA = 10000.0
NORM_EPS = 1e-6
N_MOD = 6
MASK_VALUE = -1e30
LOG2E = math.log2(math.e)
Q_SCALE = HEAD_DIM ** -0.5 * LOG2E

FOX_W = N_FOX_HEADS * HEAD_DIM
SWA_Q_W = N_SWA_HEADS * HEAD_DIM
SWA_KV_W = N_SWA_KV_HEADS * HEAD_DIM
TOK_W = FOX_W + SWA_KV_W
ROW_FQ = 0
ROW_FV = FOX_W
ROW_SQ = 2 * FOX_W
ROW_SV = ROW_SQ + SWA_Q_W
FEAT_W = ROW_SV + SWA_KV_W
CUM_SPLIT = 3
FG_ROWS = 32
LANES = 128

V7X_VMEM_LIMIT = 60 * 1024 * 1024

BF16 = jnp.bfloat16
F32 = jnp.float32
_NT = (((1,), (1,)), ((), ()))
_TN = (((0,), (0,)), ((), ()))


NORM_ROWS = 16


def _inv_rms(x):
    return lax.rsqrt(jnp.mean(x * x, axis=-1, keepdims=True) + NORM_EPS)


def _norm_modulate(x_ref, gain_row, shift_row, out_ref, row0=0, nrows=None):
    nrows = x_ref.shape[0] if nrows is None else nrows
    for c in range(nrows // NORM_ROWS):
        rows = slice(row0 + c * NORM_ROWS, row0 + (c + 1) * NORM_ROWS)
        x = x_ref[rows, :]
        out_ref[rows, :] = ((x * _inv_rms(x)) * gain_row + shift_row).astype(out_ref.dtype)


def _norm_gate_residual(y_ref, x_ref, gate_row, row0=0, nrows=None):
    nrows = x_ref.shape[0] if nrows is None else nrows
    for c in range(nrows // NORM_ROWS):
        rows = slice(row0 + c * NORM_ROWS, row0 + (c + 1) * NORM_ROWS)
        y = y_ref[rows, :]
        y_ref[rows, :] = x_ref[rows, :] + (y * _inv_rms(y)) * gate_row


def _resident(shape, index_map):
    return pl.BlockSpec(shape, index_map, pipeline_mode=pl.Buffered(1))


def _mod_kernel(c_ref, w_ref, b_ref, o_ref):
    c = c_ref[...]
    cond = c * jax.nn.sigmoid(c)
    o_ref[...] = jnp.dot(cond.astype(BF16), w_ref[...].astype(BF16),
                         preferred_element_type=F32) + b_ref[...]


def _mod(c, w, b, *, tn=1024):
    bsz, d = c.shape
    n = w.shape[1]
    return pl.pallas_call(
        _mod_kernel,
        grid=(n // tn,),
        in_specs=[pl.BlockSpec((bsz, d), lambda j: (0, 0)),
                  pl.BlockSpec((d, tn), lambda j: (0, j)),
                  pl.BlockSpec((1, tn), lambda j: (0, j))],
        out_specs=pl.BlockSpec((bsz, tn), lambda j: (0, j)),
        out_shape=jax.ShapeDtypeStruct((bsz, n), F32),
        compiler_params=pltpu.CompilerParams(
            dimension_semantics=("arbitrary",), vmem_limit_bytes=V7X_VMEM_LIMIT),
        name="mod",
    )(c, w, b.reshape(1, n))


def _inproj_kernel(x_ref, mod_ref, g_ref, wfk_ref, wsk_ref, wfq_ref, wfv_ref, wsq_ref, wsv_ref,
                   wfg_ref, bfg_ref, phase_ref,
                   tri_ref, cos_ref, sin_ref, cost_ref, sint_ref, wo_ref, wu_ref, wd_ref,
                   tok_ref, aug_ref, feat_ref, wo_bf_ref, wu_bf_ref, wd_bf_ref,
                   h_scr, carry_scr, *, tiles_per_seq):
    i = pl.program_id(0)
    hd = HEAD_DIM
    half = hd // 2

    wo_bf_ref[...] = wo_ref[...].astype(BF16)
    wu_bf_ref[...] = wu_ref[...].astype(BF16)
    wd_bf_ref[...] = wd_ref[...].astype(BF16)

    _norm_modulate(x_ref, g_ref[...] * (1.0 + mod_ref[0, 1:2, :]), mod_ref[0, 0:1, :], h_scr)

    tok_ref[:, :FOX_W] = jnp.dot(h_scr[...], wfk_ref[...],
                                 preferred_element_type=F32).astype(BF16)
    sk = jnp.dot(h_scr[...], wsk_ref[...], preferred_element_type=F32)
    cos, sin_signed = cos_ref[...], sin_ref[...]
    for r in range(N_SWA_KV_HEADS):
        blk = sk[:, r * hd:(r + 1) * hd]
        tok_ref[:, FOX_W + r * hd:FOX_W + (r + 1) * hd] = (
            blk * cos + pltpu.roll(blk, half, 1) * sin_signed).astype(BF16)

    z = lax.dot_general(wfg_ref[...], h_scr[...], _NT,
                        preferred_element_type=F32) + bfg_ref[...]
    logf = jnp.minimum(z, 0.0) - jnp.log1p(jnp.exp(-jnp.abs(z)))
    p0 = logf.astype(BF16)
    r0 = logf - p0.astype(F32)
    p1 = r0.astype(BF16)
    p2 = (r0 - p1.astype(F32)).astype(BF16)
    cs = jnp.dot(jnp.concatenate([p0, p1, p2], axis=0), tri_ref[...],
                 preferred_element_type=F32)

    @pl.when(i % tiles_per_seq == 0)
    def _():
        carry_scr[...] = jnp.zeros_like(carry_scr)

    cum = cs[:FG_ROWS] + cs[FG_ROWS:2 * FG_ROWS] + cs[2 * FG_ROWS:] + carry_scr[...]
    carry_scr[...] = cum[:, cum.shape[1] - 1:]
    negc = cum * (-LOG2E)
    hi = negc.astype(BF16).astype(F32)
    rem = negc - hi
    mid = rem.astype(BF16).astype(F32)
    lo = rem - mid
    ph = phase_ref[...]
    pieces = jnp.where(ph == 0, hi, jnp.where(ph == 1, mid, lo))
    pad = jnp.zeros((LANES - FG_ROWS, pieces.shape[1]), F32)
    aug_ref[...] = jnp.concatenate([pieces, pad], axis=0).T.astype(BF16)

    def feat(w_ref):
        return lax.dot_general(w_ref[...], h_scr[...], _NT, preferred_element_type=F32)

    feat_ref[ROW_FQ:ROW_FQ + FOX_W, :] = (feat(wfq_ref) * Q_SCALE).astype(BF16)
    feat_ref[ROW_FV:ROW_FV + FOX_W, :] = feat(wfv_ref).astype(BF16)
    sq = feat(wsq_ref)
    cos_t, sin_t = cost_ref[...], sint_ref[...]
    for hh in range(N_SWA_HEADS):
        x1 = sq[hh * hd:hh * hd + half, :]
        x2 = sq[hh * hd + half:(hh + 1) * hd, :]
        base = ROW_SQ + hh * hd
        feat_ref[base:base + half, :] = ((x1 * cos_t - x2 * sin_t) * Q_SCALE).astype(BF16)
        feat_ref[base + half:base + hd, :] = ((x2 * cos_t + x1 * sin_t) * Q_SCALE).astype(BF16)
    feat_ref[ROW_SV:, :] = feat(wsv_ref).astype(BF16)


def _inproj(x2, mod3, g, proj_weights, w_fg, b_fg, phase, tri, cos, sin_signed,
            cos_t, sin_t, later_weights, *, seq, tm):
    t, d = x2.shape
    steps = t // tm
    tiles_per_seq = seq // tm
    half = HEAD_DIM // 2
    kern = functools.partial(_inproj_kernel, tiles_per_seq=tiles_per_seq)
    const = lambda i: (0, 0)
    pos = lambda i: (i % tiles_per_seq, 0)
    pos_t = lambda i: (0, i % tiles_per_seq)
    slab = lambda i: (i, 0)
    slab_specs = [pl.BlockSpec((w.shape[0] // steps, w.shape[1]), slab) for w in later_weights]
    return pl.pallas_call(
        kern,
        grid=(steps,),
        in_specs=[
            pl.BlockSpec((tm, d), lambda i: (i, 0)),
            pl.BlockSpec((1, N_MOD, d), lambda i: (i // tiles_per_seq, 0, 0)),
            _resident((1, d), const),
        ] + [_resident(w.shape, const) for w in proj_weights] + [
            _resident((FG_ROWS, d), const),
            _resident((FG_ROWS, 1), const),
            _resident((FG_ROWS, 1), const),
            _resident((tm, tm), const),
            pl.BlockSpec((tm, HEAD_DIM), pos),
            pl.BlockSpec((tm, HEAD_DIM), pos),
            pl.BlockSpec((half, tm), pos_t),
            pl.BlockSpec((half, tm), pos_t),
        ] + slab_specs,
        out_specs=[
            pl.BlockSpec((tm, TOK_W), lambda i: (i, 0)),
            pl.BlockSpec((tm, LANES), lambda i: (i, 0)),
            pl.BlockSpec((FEAT_W, tm), lambda i: (0, i)),
        ] + slab_specs,
        out_shape=[jax.ShapeDtypeStruct((t, TOK_W), BF16),
                   jax.ShapeDtypeStruct((t, LANES), BF16),
                   jax.ShapeDtypeStruct((FEAT_W, t), BF16)]
        + [jax.ShapeDtypeStruct(w.shape, BF16) for w in later_weights],
        scratch_shapes=[pltpu.VMEM((tm, d), BF16), pltpu.VMEM((FG_ROWS, 1), F32)],
        compiler_params=pltpu.CompilerParams(
            dimension_semantics=("arbitrary",), vmem_limit_bytes=V7X_VMEM_LIMIT),
        name="in_proj",
    )(x2, mod3, g.reshape(1, d), *proj_weights, w_fg, b_fg, phase, tri, cos, sin_signed,
      cos_t, sin_t, *later_weights)


def _fox_kernel(qt_ref, k_ref, aug_ref, vt_ref, o_ref, *, blk, heads):
    hd = HEAD_DIM
    seq = k_ref.shape[1]
    nblk = seq // blk
    r = lax.broadcasted_iota(jnp.int32, (LANES, blk), 0)
    kk = lax.broadcasted_iota(jnp.int32, (blk, blk), 0)
    qq = lax.broadcasted_iota(jnp.int32, (blk, blk), 1)
    causal = kk <= qq
    onehot_t = []
    for hh in range(heads):
        hidx = pl.program_id(1) * heads + hh
        sel = (r >= CUM_SPLIT * hidx) & (r < CUM_SPLIT * (hidx + 1))
        onehot_t.append(jnp.where(sel, 1.0, 0.0).astype(BF16))

    def scores(hh, i):
        nk = (i + 1) * blk
        cols = slice(hh * hd, (hh + 1) * hd)
        q_aug = jnp.concatenate([qt_ref[cols, i * blk:nk], onehot_t[hh]], axis=0)
        k_aug = jnp.concatenate([k_ref[0, :nk, cols], aug_ref[0, :nk, :]], axis=1)
        s = jnp.dot(k_aug, q_aug, preferred_element_type=F32)
        s_diag = jnp.where(causal, s[nk - blk:, :], MASK_VALUE)
        s = s_diag if i == 0 else jnp.concatenate([s[:nk - blk, :], s_diag], axis=0)
        return s, jnp.max(s, axis=0, keepdims=True)

    s_next = [scores(hh, 0) for hh in range(heads)]
    for i in range(nblk):
        qs = slice(i * blk, (i + 1) * blk)
        nk = (i + 1) * blk
        s_cur = s_next
        if i + 1 < nblk:
            s_next = [scores(hh, i + 1) for hh in range(heads)]
        for hh in range(heads):
            cols = slice(hh * hd, (hh + 1) * hd)
            s, m = s_cur[hh]
            p = jnp.exp2(s - m)
            l = jnp.sum(p, axis=0, keepdims=True)
            acc = jnp.dot(vt_ref[cols, :nk], p.astype(BF16), preferred_element_type=F32)
            o_ref[cols, qs] = (acc * (1.0 / l)).astype(o_ref.dtype)


def _fox(feat_t, tok3, aug3, *, blk=256, heads=4):
    bsz, seq, _ = tok3.shape
    hw = heads * HEAD_DIM
    qb, vb = ROW_FQ // hw, ROW_FV // hw
    kern = functools.partial(_fox_kernel, blk=blk, heads=heads)
    return pl.pallas_call(
        kern,
        grid=(bsz, N_FOX_HEADS // heads),
        in_specs=[
            pl.BlockSpec((hw, seq), lambda b, h: (qb + h, b)),
            pl.BlockSpec((1, seq, hw), lambda b, h: (b, 0, h)),
            pl.BlockSpec((1, seq, LANES), lambda b, h: (b, 0, 0)),
            pl.BlockSpec((hw, seq), lambda b, h: (vb + h, b)),
        ],
        out_specs=pl.BlockSpec((hw, seq), lambda b, h: (h, b)),
        out_shape=jax.ShapeDtypeStruct((FOX_W, bsz * seq), BF16),
        compiler_params=pltpu.CompilerParams(
            dimension_semantics=("arbitrary", "arbitrary"),
            vmem_limit_bytes=V7X_VMEM_LIMIT),
        name="fox",
    )(feat_t, tok3, aug3, feat_t)


def _swa_kernel(qt_ref, k_ref, vt_ref, sink_ref, o_ref):
    w, hd = SWA_WINDOW, HEAD_DIM
    seq = k_ref.shape[1]
    gl = SWA_GROUP * w
    gw = SWA_GROUP * hd
    groups = range(N_SWA_KV_HEADS)
    sinks = [jnp.concatenate(
        [jnp.full((1, w), sink_ref[g * SWA_GROUP + r] * LOG2E, F32) for r in range(SWA_GROUP)],
        axis=1) for g in groups]
    kk = lax.broadcasted_iota(jnp.int32, (2 * w, gl), 0)
    qq = lax.broadcasted_iota(jnp.int32, (2 * w, gl), 1) & (w - 1)
    band = (kk > qq) & (kk - w <= qq)
    bias = jnp.where(band, 0.0, MASK_VALUE)
    bias_first = bias[w:, :]

    def scores(g, n):
        q_t = jnp.concatenate(
            [qt_ref[g * gw + r * hd:g * gw + (r + 1) * hd, n * w:(n + 1) * w]
             for r in range(SWA_GROUP)], axis=1)
        ks = slice(max(n - 1, 0) * w, (n + 1) * w)
        s = jnp.dot(k_ref[0, ks, g * hd:(g + 1) * hd], q_t,
                    preferred_element_type=F32)
        s = s + (bias if n > 0 else bias_first)
        return s, jnp.maximum(jnp.max(s, axis=0, keepdims=True), sinks[g])

    nblk = seq // w
    s_next = [scores(g, 0) for g in groups]
    for n in range(nblk):
        ks = slice(max(n - 1, 0) * w, (n + 1) * w)
        s_cur = s_next
        if n + 1 < nblk:
            s_next = [scores(g, n + 1) for g in groups]
        for g in groups:
            s, m = s_cur[g]
            p = jnp.exp2(s - m)
            denom = jnp.sum(p, axis=0, keepdims=True) + jnp.exp2(sinks[g] - m)
            o_t = jnp.dot(vt_ref[g * hd:(g + 1) * hd, ks], p.astype(BF16),
                          preferred_element_type=F32)
            o_t = o_t * (1.0 / denom)
            for r in range(SWA_GROUP):
                col = g * gw + r * hd
                o_ref[col:col + hd, n * w:(n + 1) * w] = (
                    o_t[:, r * w:(r + 1) * w].astype(o_ref.dtype))


def _swa(feat_t, tok3, sinks):
    bsz, seq, _ = tok3.shape
    qb, kb, vb = ROW_SQ // SWA_Q_W, FOX_W // SWA_KV_W, ROW_SV // SWA_KV_W
    return pl.pallas_call(
        _swa_kernel,
        grid=(bsz,),
        in_specs=[
            pl.BlockSpec((SWA_Q_W, seq), lambda b: (qb, b)),
            pl.BlockSpec((1, seq, SWA_KV_W), lambda b: (b, 0, kb)),
            pl.BlockSpec((SWA_KV_W, seq), lambda b: (vb, b)),
            pl.BlockSpec(memory_space=pltpu.SMEM),
        ],
        out_specs=pl.BlockSpec((SWA_Q_W, seq), lambda b: (0, b)),
        out_shape=jax.ShapeDtypeStruct((SWA_Q_W, bsz * seq), BF16),
        compiler_params=pltpu.CompilerParams(
            dimension_semantics=("arbitrary",), vmem_limit_bytes=V7X_VMEM_LIMIT),
        name="swa",
    )(feat_t, tok3, feat_t, sinks)


OUTPROJ_ROW_PARTS = 2


def _outproj_kernel(fox_ref, swa_ref, x_ref, mod_ref, g_ref, wf_ref, ws_ref, o_ref):
    part = x_ref.shape[0] // OUTPROJ_ROW_PARTS
    gate = mod_ref[0, 2:3, :] * g_ref[...]
    for a in range(OUTPROJ_ROW_PARTS):
        rows = slice(a * part, (a + 1) * part)
        o_ref[rows, :] = (
            lax.dot_general(fox_ref[:, rows], wf_ref[...], _TN, preferred_element_type=F32)
            + lax.dot_general(swa_ref[:, rows], ws_ref[...], _TN, preferred_element_type=F32))
    for a in range(OUTPROJ_ROW_PARTS):
        _norm_gate_residual(o_ref, x_ref, gate, a * part, part)


def _outproj(fox2, swa2, x2, mod3, g, w_out_bf, *, seq, tm=1024):
    t, d = x2.shape
    tiles_per_seq = seq // tm
    return pl.pallas_call(
        _outproj_kernel,
        grid=(t // tm,),
        in_specs=[
            pl.BlockSpec((FOX_W, tm), lambda i: (0, i)),
            pl.BlockSpec((SWA_Q_W, tm), lambda i: (0, i)),
            pl.BlockSpec((tm, d), lambda i: (i, 0)),
            pl.BlockSpec((1, N_MOD, d), lambda i: (i // tiles_per_seq, 0, 0)),
            _resident((1, d), lambda i: (0, 0)),
            _resident((FOX_W, d), lambda i: (0, 0)),
            _resident((SWA_Q_W, d), lambda i: (FOX_W // SWA_Q_W, 0)),
        ],
        out_specs=pl.BlockSpec((tm, d), lambda i: (i, 0)),
        out_shape=jax.ShapeDtypeStruct((t, d), F32),
        compiler_params=pltpu.CompilerParams(
            dimension_semantics=("arbitrary",), vmem_limit_bytes=V7X_VMEM_LIMIT),
        name="out_proj",
    )(fox2, swa2, x2, mod3, g.reshape(1, d), w_out_bf, w_out_bf)


MLP_ROW_PARTS = 2


def _mlp_kernel(x_ref, mod_ref, gpre_ref, gpost_ref, wu_ref, wd_ref, o_ref, h_scr):
    j = pl.program_id(1)
    last = pl.num_programs(1) - 1
    part = x_ref.shape[0] // MLP_ROW_PARTS

    def step(first, final):
        if first:
            gain = gpre_ref[...] * (1.0 + mod_ref[0, 4:5, :])
            shift = mod_ref[0, 3:4, :]
        if final:
            gate = mod_ref[0, 5:6, :] * gpost_ref[...]
        ups = []
        for a in range(MLP_ROW_PARTS):
            rows = slice(a * part, (a + 1) * part)
            if first:
                _norm_modulate(x_ref, gain, shift, h_scr, a * part, part)
            ups.append(jnp.dot(h_scr[rows, :], wu_ref[...], preferred_element_type=F32))
        for a in range(MLP_ROW_PARTS):
            rows = slice(a * part, (a + 1) * part)
            act = jnp.square(jnp.maximum(ups[a], 0.0)).astype(BF16)
            y = jnp.dot(act, wd_ref[...], preferred_element_type=F32)
            if first:
                o_ref[rows, :] = y
            else:
                o_ref[rows, :] += y
            if final:
                _norm_gate_residual(o_ref, x_ref, gate, a * part, part)

    pl.when(j == 0)(lambda: step(True, False))
    pl.when((j > 0) & (j < last))(lambda: step(False, False))
    pl.when(j == last)(lambda: step(False, True))


def _mlp(x2, mod3, g_pre, g_post, w_up, w_down, *, seq, tm=1024, tf=1024):
    t, d = x2.shape
    ff = w_up.shape[1]
    tiles_per_seq = seq // tm
    return pl.pallas_call(
        _mlp_kernel,
        grid=(t // tm, ff // tf),
        in_specs=[
            pl.BlockSpec((tm, d), lambda i, j: (i, 0)),
            pl.BlockSpec((1, N_MOD, d), lambda i, j: (i // tiles_per_seq, 0, 0)),
            pl.BlockSpec((1, d), lambda i, j: (0, 0)),
            pl.BlockSpec((1, d), lambda i, j: (0, 0)),
            pl.BlockSpec((d, tf), lambda i, j: (0, j)),
            pl.BlockSpec((tf, d), lambda i, j: (j, 0)),
        ],
        out_specs=pl.BlockSpec((tm, d), lambda i, j: (i, 0)),
        out_shape=jax.ShapeDtypeStruct((t, d), F32),
        scratch_shapes=[pltpu.VMEM((tm, d), BF16)],
        compiler_params=pltpu.CompilerParams(
            dimension_semantics=("arbitrary", "arbitrary"),
            vmem_limit_bytes=V7X_VMEM_LIMIT),
        name="mlp",
    )(x2, mod3, g_pre.reshape(1, d), g_post.reshape(1, d), w_up, w_down)


def _rope_tables(seq):
    half = HEAD_DIM // 2
    inv_freq = 1.0 / (ROPE_THETA ** (jnp.arange(half, dtype=F32) * (2.0 / HEAD_DIM)))
    ang = jnp.arange(seq).astype(F32)[:, None] * inv_freq[None, :]
    cos, sin = jnp.cos(ang), jnp.sin(ang)
    return (jnp.concatenate([cos, cos], axis=-1), jnp.concatenate([-sin, sin], axis=-1),
            cos.T, sin.T)


def kernel(x, c, w_mod, b_mod, g_pre_mix, g_post_mix, w_in, b_forget, swa_sinks,
           w_out, g_pre_mlp, g_post_mlp, w_up, w_down):
    bsz, seq, d = x.shape
    depth = w_mod.shape[0]
    t = bsz * seq
    tm_in = 512
    cos, sin_signed, cos_t, sin_t = _rope_tables(seq)
    phase = (jnp.arange(FG_ROWS, dtype=jnp.int32) % CUM_SPLIT).reshape(FG_ROWS, 1)
    tri = jnp.triu(jnp.ones((tm_in, tm_in), BF16))
    o_fk, o_fv, o_fg = FOX_W, 2 * FOX_W, 3 * FOX_W
    o_sq = o_fg + N_FOX_HEADS
    o_sk = o_sq + SWA_Q_W
    o_sv = o_sk + SWA_KV_W
    n_rep = CUM_SPLIT * N_FOX_HEADS

    x2 = x.reshape(t, d)
    for l in range(depth):
        wi = w_in[l]
        proj_weights = (
            wi[:, o_fk:o_fv].astype(BF16), wi[:, o_sk:o_sv].astype(BF16),
            wi[:, :o_fk].T.astype(BF16), wi[:, o_fv:o_fg].T.astype(BF16),
            wi[:, o_sq:o_sk].T.astype(BF16), wi[:, o_sv:].T.astype(BF16))
        w_fg = jnp.zeros((FG_ROWS, d), F32).at[:n_rep].set(
            jnp.repeat(wi[:, o_fg:o_sq].T, CUM_SPLIT, axis=0)).astype(BF16)
        b_fg = jnp.zeros((FG_ROWS, 1), F32).at[:n_rep, 0].set(
            jnp.repeat(b_forget[l].astype(F32), CUM_SPLIT))

        mod3 = _mod(c, w_mod[l], b_mod[l]).reshape(bsz, N_MOD, d)
        tok, aug, feat_t, w_o, w_u, w_d = _inproj(
            x2, mod3, g_pre_mix[l], proj_weights, w_fg, b_fg, phase, tri, cos, sin_signed,
            cos_t, sin_t, (w_out[l], w_up[l], w_down[l]), seq=seq, tm=tm_in)
        tok3 = tok.reshape(bsz, seq, TOK_W)
        fox = _fox(feat_t, tok3, aug.reshape(bsz, seq, LANES))
        swa = _swa(feat_t, tok3, swa_sinks[l])
        x2 = _outproj(fox, swa, x2, mod3,
                      g_post_mix[l], w_o, seq=seq)
        x2 = _mlp(x2, mod3, g_pre_mlp[l], g_post_mlp[l], w_u, w_d, seq=seq)
    return x2.reshape(bsz, seq, d)
```

```python
import functools
import math

import jax
import jax.numpy as jnp
from jax import lax
from jax.experimental import pallas as pl
from jax.experimental.pallas import tpu as pltpu

HEAD_DIM = 128
N_FOX_HEADS = 8
N_SWA_HEADS = 8
N_SWA_KV_HEADS = 2
SWA_GROUP = N_SWA_HEADS // N_SWA_KV_HEADS
SWA_WINDOW = 128
ROPE_THETA = 10000.0
NORM_EPS = 1e-6
N_MOD = 6
MASK_VALUE = -1e30
LOG2E = math.log2(math.e)
Q_SCALE = HEAD_DIM ** -0.5 * LOG2E

FOX_W = N_FOX_HEADS * HEAD_DIM
SWA_Q_W = N_SWA_HEADS * HEAD_DIM
SWA_KV_W = N_SWA_KV_HEADS * HEAD_DIM
TOK_W = FOX_W + SWA_KV_W
ROW_FQ = 0
ROW_FV = FOX_W
ROW_SQ = 2 * FOX_W
ROW_SV = ROW_SQ + SWA_Q_W
FEAT_W = ROW_SV + SWA_KV_W
CUM_SPLIT = 3
FG_ROWS = 32
LANES = 128

V7X_VMEM_LIMIT = 60 * 1024 * 1024

BF16 = jnp.bfloat16
F32 = jnp.float32
_NT = (((1,), (1,)), ((), ()))
_TN = (((0,), (0,)), ((), ()))


NORM_ROWS = 16


def _inv_rms(x):
    return lax.rsqrt(jnp.mean(x * x, axis=-1, keepdims=True) + NORM_EPS)


def _norm_modulate(x_ref, gain_row, shift_row, out_ref, row0=0, nrows=None):
    nrows = x_ref.shape[0] if nrows is None else nrows
    for c in range(nrows // NORM_ROWS):
        rows = slice(row0 + c * NORM_ROWS, row0 + (c + 1) * NORM_ROWS)
        x = x_ref[rows, :]
        out_ref[rows, :] = ((x * _inv_rms(x)) * gain_row + shift_row).astype(out_ref.dtype)


def _norm_gate_residual(y_ref, x_ref, gate_row, row0=0, nrows=None):
    nrows = x_ref.shape[0] if nrows is None else nrows
    for c in range(nrows // NORM_ROWS):
        rows = slice(row0 + c * NORM_ROWS, row0 + (c + 1) * NORM_ROWS)
        y = y_ref[rows, :]
        y_ref[rows, :] = x_ref[rows, :] + (y * _inv_rms(y)) * gate_row


def _resident(shape, index_map):
    return pl.BlockSpec(shape, index_map, pipeline_mode=pl.Buffered(1))


def _mod_kernel(c_ref, w_ref, b_ref, o_ref):
    c = c_ref[...]
    cond = c * jax.nn.sigmoid(c)
    o_ref[...] = jnp.dot(cond.astype(BF16), w_ref[...].astype(BF16),
                         preferred_element_type=F32) + b_ref[...]


def _mod(c, w, b, *, tn=1024):
    bsz, d = c.shape
    n = w.shape[1]
    return pl.pallas_call(
        _mod_kernel,
        grid=(n // tn,),
        in_specs=[pl.BlockSpec((bsz, d), lambda j: (0, 0)),
                  pl.BlockSpec((d, tn), lambda j: (0, j)),
                  pl.BlockSpec((1, tn), lambda j: (0, j))],
        out_specs=pl.BlockSpec((bsz, tn), lambda j: (0, j)),
        out_shape=jax.ShapeDtypeStruct((bsz, n), F32),
        compiler_params=pltpu.CompilerParams(
            dimension_semantics=("arbitrary",), vmem_limit_bytes=V7X_VMEM_LIMIT),
        name="mod",
    )(c, w, b.reshape(1, n))


def _inproj_kernel(x_ref, mod_ref, g_ref, wfk_ref, wfq_ref, wfv_ref, wsqk_ref, wsvfg_ref,
                   bfg_ref, phase_ref,
                   tri_ref, cost_ref, sint_ref, wo_ref, wu_ref, wd_ref,
                   tok_ref, aug_ref, feat_ref, wo_bf_ref, wu_bf_ref, wd_bf_ref,
                   h_scr, carry_scr, *, tiles_per_seq):
    i = pl.program_id(0)
    hd = HEAD_DIM
    half = hd // 2

    _norm_modulate(x_ref, g_ref[...] * (1.0 + mod_ref[0, 1:2, :]), mod_ref[0, 0:1, :], h_scr)

    half_rows = h_scr.shape[0] // 2
    for a in range(2):
        rows = slice(a * half_rows, (a + 1) * half_rows)
        tok_ref[rows, :FOX_W] = jnp.dot(h_scr[rows, :], wfk_ref[...],
                                        preferred_element_type=F32).astype(BF16)

    def feat(w_ref):
        return lax.dot_general(w_ref[...], h_scr[...], _NT, preferred_element_type=F32)

    svfg = feat(wsvfg_ref)
    feat_ref[ROW_SV:, :] = svfg[:SWA_KV_W].astype(BF16)
    z = svfg[SWA_KV_W:] + bfg_ref[...]
    logf = jnp.minimum(z, 0.0) - jnp.log1p(jnp.exp(-jnp.abs(z)))
    p0 = logf.astype(BF16)
    r0 = logf - p0.astype(F32)
    p1 = r0.astype(BF16)
    p2 = (r0 - p1.astype(F32)).astype(BF16)

    feat_ref[ROW_FQ:ROW_FQ + FOX_W, :] = (feat(wfq_ref) * Q_SCALE).astype(BF16)
    wo_bf_ref[...] = wo_ref[...].astype(BF16)
    wu_bf_ref[...] = wu_ref[...].astype(BF16)

    cs = jnp.dot(jnp.concatenate([p0, p1, p2], axis=0), tri_ref[...],
                 preferred_element_type=F32)

    @pl.when(i % tiles_per_seq == 0)
    def _():
        carry_scr[...] = jnp.zeros_like(carry_scr)

    cum = cs[:FG_ROWS] + cs[FG_ROWS:2 * FG_ROWS] + cs[2 * FG_ROWS:] + carry_scr[...]
    carry_scr[...] = cum[:, cum.shape[1] - 1:]
    negc = cum * (-LOG2E)
    hi = negc.astype(BF16).astype(F32)
    rem = negc - hi
    mid = rem.astype(BF16).astype(F32)
    lo = rem - mid
    ph = phase_ref[...]
    pieces = jnp.where(ph == 0, hi, jnp.where(ph == 1, mid, lo))
    pad = jnp.zeros((LANES - FG_ROWS, pieces.shape[1]), F32)
    aug_ref[...] = jnp.concatenate([pieces, pad], axis=0).T.astype(BF16)

    wd_bf_ref[...] = wd_ref[...].astype(BF16)
    sqk = feat(wsqk_ref)
    feat_ref[ROW_FV:ROW_FV + FOX_W, :] = feat(wfv_ref).astype(BF16)
    cos_t, sin_t = cost_ref[...], sint_ref[...]
    for hh in range(N_SWA_HEADS + N_SWA_KV_HEADS):
        x1 = sqk[hh * hd:hh * hd + half, :]
        x2 = sqk[hh * hd + half:(hh + 1) * hd, :]
        lo_half = x1 * cos_t - x2 * sin_t
        hi_half = x2 * cos_t + x1 * sin_t
        if hh < N_SWA_HEADS:
            base = ROW_SQ + hh * hd
            feat_ref[base:base + half, :] = (lo_half * Q_SCALE).astype(BF16)
            feat_ref[base + half:base + hd, :] = (hi_half * Q_SCALE).astype(BF16)
        else:
            col = FOX_W + (hh - N_SWA_HEADS) * hd
            tok_ref[:, col:col + hd] = jnp.concatenate(
                [lo_half, hi_half], axis=0).T.astype(BF16)


def _inproj(x2, mod3, g, proj_weights, b_fg, phase, tri, cos_t, sin_t, later_weights,
            *, seq, tm):
    t, d = x2.shape
    steps = t // tm
    tiles_per_seq = seq // tm
    half = HEAD_DIM // 2
    kern = functools.partial(_inproj_kernel, tiles_per_seq=tiles_per_seq)
    const = lambda i: (0, 0)
    pos_t = lambda i: (0, i % tiles_per_seq)
    slab = lambda i: (i, 0)
    slab_specs = [pl.BlockSpec((w.shape[0] // steps, w.shape[1]), slab) for w in later_weights]
    return pl.pallas_call(
        kern,
        grid=(steps,),
        in_specs=[
            pl.BlockSpec((tm, d), lambda i: (i, 0)),
            pl.BlockSpec((1, N_MOD, d), lambda i: (i // tiles_per_seq, 0, 0)),
            _resident((1, d), const),
        ] + [_resident(w.shape, const) for w in proj_weights] + [
            _resident((FG_ROWS, 1), const),
            _resident((FG_ROWS, 1), const),
            _resident((tm, tm), const),
            pl.BlockSpec((half, tm), pos_t),
            pl.BlockSpec((half, tm), pos_t),
        ] + slab_specs,
        out_specs=[
            pl.BlockSpec((tm, TOK_W), lambda i: (i, 0)),
            pl.BlockSpec((tm, LANES), lambda i: (i, 0)),
            pl.BlockSpec((FEAT_W, tm), lambda i: (0, i)),
        ] + slab_specs,
        out_shape=[jax.ShapeDtypeStruct((t, TOK_W), BF16),
                   jax.ShapeDtypeStruct((t, LANES), BF16),
                   jax.ShapeDtypeStruct((FEAT_W, t), BF16)]
        + [jax.ShapeDtypeStruct(w.shape, BF16) for w in later_weights],
        scratch_shapes=[pltpu.VMEM((tm, d), BF16), pltpu.VMEM((FG_ROWS, 1), F32)],
        compiler_params=pltpu.CompilerParams(
            dimension_semantics=("arbitrary",), vmem_limit_bytes=V7X_VMEM_LIMIT),
        name="in_proj",
    )(x2, mod3, g.reshape(1, d), *proj_weights, b_fg, phase, tri, cos_t, sin_t,
      *later_weights)


def _fox_kernel(qt_ref, k_ref, aug_ref, vt_ref, o_ref, *, blk, heads):
    hd = HEAD_DIM
    seq = k_ref.shape[1]
    nblk = seq // blk
    r = lax.broadcasted_iota(jnp.int32, (LANES, blk), 0)
    kk = lax.broadcasted_iota(jnp.int32, (blk, blk), 0)
    qq = lax.broadcasted_iota(jnp.int32, (blk, blk), 1)
    causal = kk <= qq
    onehot_t = []
    for hh in range(heads):
        hidx = pl.program_id(1) * heads + hh
        sel = (r >= CUM_SPLIT * hidx) & (r < CUM_SPLIT * (hidx + 1))
        onehot_t.append(jnp.where(sel, 1.0, 0.0).astype(BF16))

    def scores(hh, i):
        nk = (i + 1) * blk
        cols = slice(hh * hd, (hh + 1) * hd)
        q_aug = jnp.concatenate([qt_ref[cols, i * blk:nk], onehot_t[hh]], axis=0)
        k_aug = jnp.concatenate([k_ref[0, :nk, cols], aug_ref[0, :nk, :]], axis=1)
        s = jnp.dot(k_aug, q_aug, preferred_element_type=F32)
        s_diag = jnp.where(causal, s[nk - blk:, :], MASK_VALUE)
        s = s_diag if i == 0 else jnp.concatenate([s[:nk - blk, :], s_diag], axis=0)
        return s, jnp.max(s, axis=0, keepdims=True)

    s_next = [scores(hh, 0) for hh in range(heads)]
    for i in range(nblk):
        qs = slice(i * blk, (i + 1) * blk)
        nk = (i + 1) * blk
        s_cur = s_next
        if i + 1 < nblk:
            s_next = [scores(hh, i + 1) for hh in range(heads)]
        for hh in range(heads):
            cols = slice(hh * hd, (hh + 1) * hd)
            s, m = s_cur[hh]
            p = jnp.exp2(s - m)
            l = jnp.sum(p, axis=0, keepdims=True)
            acc = jnp.dot(vt_ref[cols, :nk], p.astype(BF16), preferred_element_type=F32)
            o_ref[cols, qs] = (acc * (1.0 / l)).astype(o_ref.dtype)


def _fox(feat_t, tok3, aug3, *, blk=256, heads=4):
    bsz, seq, _ = tok3.shape
    hw = heads * HEAD_DIM
    qb, vb = ROW_FQ // hw, ROW_FV // hw
    kern = functools.partial(_fox_kernel, blk=blk, heads=heads)
    return pl.pallas_call(
        kern,
        grid=(bsz, N_FOX_HEADS // heads),
        in_specs=[
            pl.BlockSpec((hw, seq), lambda b, h: (qb + h, b)),
            pl.BlockSpec((1, seq, hw), lambda b, h: (b, 0, h)),
            pl.BlockSpec((1, seq, LANES), lambda b, h: (b, 0, 0)),
            pl.BlockSpec((hw, seq), lambda b, h: (vb + h, b)),
        ],
        out_specs=pl.BlockSpec((hw, seq), lambda b, h: (h, b)),
        out_shape=jax.ShapeDtypeStruct((FOX_W, bsz * seq), BF16),
        compiler_params=pltpu.CompilerParams(
            dimension_semantics=("arbitrary", "arbitrary"),
            vmem_limit_bytes=V7X_VMEM_LIMIT),
        name="fox",
    )(feat_t, tok3, aug3, feat_t)


def _swa_kernel(qt_ref, k_ref, vt_ref, sink_ref, o_ref):
    w, hd = SWA_WINDOW, HEAD_DIM
    seq = k_ref.shape[1]
    gl = SWA_GROUP * w
    gw = SWA_GROUP * hd
    groups = range(N_SWA_KV_HEADS)
    sinks = [jnp.concatenate(
        [jnp.full((1, w), sink_ref[g * SWA_GROUP + r] * LOG2E, F32) for r in range(SWA_GROUP)],
        axis=1) for g in groups]
    kk = lax.broadcasted_iota(jnp.int32, (2 * w, gl), 0)
    qq = lax.broadcasted_iota(jnp.int32, (2 * w, gl), 1) & (w - 1)
    band = (kk > qq) & (kk - w <= qq)
    bias = jnp.where(band, 0.0, MASK_VALUE)
    bias_first = bias[w:, :]

    def scores(g, n):
        q_t = jnp.concatenate(
            [qt_ref[g * gw + r * hd:g * gw + (r + 1) * hd, n * w:(n + 1) * w]
             for r in range(SWA_GROUP)], axis=1)
        ks = slice(max(n - 1, 0) * w, (n + 1) * w)
        s = jnp.dot(k_ref[0, ks, g * hd:(g + 1) * hd], q_t,
                    preferred_element_type=F32)
        s = s + (bias if n > 0 else bias_first)
        return s, jnp.maximum(jnp.max(s, axis=0, keepdims=True), sinks[g])

    nblk = seq // w
    s_next = [scores(g, 0) for g in groups]
    for n in range(nblk):
        ks = slice(max(n - 1, 0) * w, (n + 1) * w)
        s_cur = s_next
        if n + 1 < nblk:
            s_next = [scores(g, n + 1) for g in groups]
        for g in groups:
            s, m = s_cur[g]
            p = jnp.exp2(s - m)
            denom = jnp.sum(p, axis=0, keepdims=True) + jnp.exp2(sinks[g] - m)
            o_t = jnp.dot(vt_ref[g * hd:(g + 1) * hd, ks], p.astype(BF16),
                          preferred_element_type=F32)
            o_t = o_t * (1.0 / denom)
            for r in range(SWA_GROUP):
                col = g * gw + r * hd
                o_ref[col:col + hd, n * w:(n + 1) * w] = (
                    o_t[:, r * w:(r + 1) * w].astype(o_ref.dtype))


def _swa(feat_t, tok3, sinks):
    bsz, seq, _ = tok3.shape
    qb, kb, vb = ROW_SQ // SWA_Q_W, FOX_W // SWA_KV_W, ROW_SV // SWA_KV_W
    return pl.pallas_call(
        _swa_kernel,
        grid=(bsz,),
        in_specs=[
            pl.BlockSpec((SWA_Q_W, seq), lambda b: (qb, b)),
            pl.BlockSpec((1, seq, SWA_KV_W), lambda b: (b, 0, kb)),
            pl.BlockSpec((SWA_KV_W, seq), lambda b: (vb, b)),
            pl.BlockSpec(memory_space=pltpu.SMEM),
        ],
        out_specs=pl.BlockSpec((SWA_Q_W, seq), lambda b: (0, b)),
        out_shape=jax.ShapeDtypeStruct((SWA_Q_W, bsz * seq), BF16),
        compiler_params=pltpu.CompilerParams(
            dimension_semantics=("arbitrary",), vmem_limit_bytes=V7X_VMEM_LIMIT),
        name="swa",
    )(feat_t, tok3, feat_t, sinks)


OUTPROJ_ROW_PARTS = 2


def _outproj_kernel(fox_ref, swa_ref, x_ref, mod_ref, g_ref, wf_ref, ws_ref, o_ref):
    part = x_ref.shape[0] // OUTPROJ_ROW_PARTS
    gate = mod_ref[0, 2:3, :] * g_ref[...]
    for a in range(OUTPROJ_ROW_PARTS):
        rows = slice(a * part, (a + 1) * part)
        o_ref[rows, :] = (
            lax.dot_general(fox_ref[:, rows], wf_ref[...], _TN, preferred_element_type=F32)
            + lax.dot_general(swa_ref[:, rows], ws_ref[...], _TN, preferred_element_type=F32))
    for a in range(OUTPROJ_ROW_PARTS):
        _norm_gate_residual(o_ref, x_ref, gate, a * part, part)


def _outproj(fox2, swa2, x2, mod3, g, w_out_bf, *, seq, tm=1024):
    t, d = x2.shape
    tiles_per_seq = seq // tm
    return pl.pallas_call(
        _outproj_kernel,
        grid=(t // tm,),
        in_specs=[
            pl.BlockSpec((FOX_W, tm), lambda i: (0, i)),
            pl.BlockSpec((SWA_Q_W, tm), lambda i: (0, i)),
            pl.BlockSpec((tm, d), lambda i: (i, 0)),
            pl.BlockSpec((1, N_MOD, d), lambda i: (i // tiles_per_seq, 0, 0)),
            _resident((1, d), lambda i: (0, 0)),
            _resident((FOX_W, d), lambda i: (0, 0)),
            _resident((SWA_Q_W, d), lambda i: (FOX_W // SWA_Q_W, 0)),
        ],
        out_specs=pl.BlockSpec((tm, d), lambda i: (i, 0)),
        out_shape=jax.ShapeDtypeStruct((t, d), F32),
        compiler_params=pltpu.CompilerParams(
            dimension_semantics=("arbitrary",), vmem_limit_bytes=V7X_VMEM_LIMIT),
        name="out_proj",
    )(fox2, swa2, x2, mod3, g.reshape(1, d), w_out_bf, w_out_bf)


MLP_ROW_PARTS = 2


def _mlp_kernel(x_ref, mod_ref, gpre_ref, gpost_ref, wu_ref, wd_ref, o_ref, h_scr):
    j = pl.program_id(1)
    last = pl.num_programs(1) - 1
    part = x_ref.shape[0] // MLP_ROW_PARTS

    def step(first, final):
        if first:
            gain = gpre_ref[...] * (1.0 + mod_ref[0, 4:5, :])
            shift = mod_ref[0, 3:4, :]
        if final:
            gate = mod_ref[0, 5:6, :] * gpost_ref[...]
        ups = []
        for a in range(MLP_ROW_PARTS):
            rows = slice(a * part, (a + 1) * part)
            if first:
                _norm_modulate(x_ref, gain, shift, h_scr, a * part, part)
            ups.append(jnp.dot(h_scr[rows, :], wu_ref[...], preferred_element_type=F32))
        for a in range(MLP_ROW_PARTS):
            rows = slice(a * part, (a + 1) * part)
            act = jnp.square(jnp.maximum(ups[a], 0.0)).astype(BF16)
            y = jnp.dot(act, wd_ref[...], preferred_element_type=F32)
            if first:
                o_ref[rows, :] = y
            else:
                o_ref[rows, :] += y
            if final:
                _norm_gate_residual(o_ref, x_ref, gate, a * part, part)

    pl.when(j == 0)(lambda: step(True, False))
    pl.when((j > 0) & (j < last))(lambda: step(False, False))
    pl.when(j == last)(lambda: step(False, True))


def _mlp(x2, mod3, g_pre, g_post, w_up, w_down, *, seq, tm=1024, tf=1024):
    t, d = x2.shape
    ff = w_up.shape[1]
    tiles_per_seq = seq // tm
    return pl.pallas_call(
        _mlp_kernel,
        grid=(t // tm, ff // tf),
        in_specs=[
            pl.BlockSpec((tm, d), lambda i, j: (i, 0)),
            pl.BlockSpec((1, N_MOD, d), lambda i, j: (i // tiles_per_seq, 0, 0)),
            pl.BlockSpec((1, d), lambda i, j: (0, 0)),
            pl.BlockSpec((1, d), lambda i, j: (0, 0)),
            pl.BlockSpec((d, tf), lambda i, j: (0, j)),
            pl.BlockSpec((tf, d), lambda i, j: (j, 0)),
        ],
        out_specs=pl.BlockSpec((tm, d), lambda i, j: (i, 0)),
        out_shape=jax.ShapeDtypeStruct((t, d), F32),
        scratch_shapes=[pltpu.VMEM((tm, d), BF16)],
        compiler_params=pltpu.CompilerParams(
            dimension_semantics=("arbitrary", "arbitrary"),
            vmem_limit_bytes=V7X_VMEM_LIMIT),
        name="mlp",
    )(x2, mod3, g_pre.reshape(1, d), g_post.reshape(1, d), w_up, w_down)


def _rope_tables(seq):
    half = HEAD_DIM // 2
    inv_freq = 1.0 / (ROPE_THETA ** (jnp.arange(half, dtype=F32) * (2.0 / HEAD_DIM)))
    ang = jnp.arange(seq).astype(F32)[:, None] * inv_freq[None, :]
    return jnp.cos(ang).T, jnp.sin(ang).T


def kernel(x, c, w_mod, b_mod, g_pre_mix, g_post_mix, w_in, b_forget, swa_sinks,
           w_out, g_pre_mlp, g_post_mlp, w_up, w_down):
    bsz, seq, d = x.shape
    depth = w_mod.shape[0]
    t = bsz * seq
    tm_in = 512
    cos_t, sin_t = _rope_tables(seq)
    phase = (jnp.arange(FG_ROWS, dtype=jnp.int32) % CUM_SPLIT).reshape(FG_ROWS, 1)
    tri = jnp.triu(jnp.ones((tm_in, tm_in), BF16))
    o_fk, o_fv, o_fg = FOX_W, 2 * FOX_W, 3 * FOX_W
    o_sq = o_fg + N_FOX_HEADS
    o_sk = o_sq + SWA_Q_W
    o_sv = o_sk + SWA_KV_W
    n_rep = CUM_SPLIT * N_FOX_HEADS

    x2 = x.reshape(t, d)
    for l in range(depth):
        wi = w_in[l]
        w_fg = jnp.zeros((FG_ROWS, d), F32).at[:n_rep].set(
            jnp.repeat(wi[:, o_fg:o_sq].T, CUM_SPLIT, axis=0))
        proj_weights = (
            wi[:, o_fk:o_fv].astype(BF16),
            wi[:, :o_fk].T.astype(BF16), wi[:, o_fv:o_fg].T.astype(BF16),
            wi[:, o_sq:o_sv].T.astype(BF16),
            jnp.concatenate([wi[:, o_sv:].T, w_fg], axis=0).astype(BF16))
        b_fg = jnp.zeros((FG_ROWS, 1), F32).at[:n_rep, 0].set(
            jnp.repeat(b_forget[l].astype(F32), CUM_SPLIT))

        mod3 = _mod(c, w_mod[l], b_mod[l]).reshape(bsz, N_MOD, d)
        tok, aug, feat_t, w_o, w_u, w_d = _inproj(
            x2, mod3, g_pre_mix[l], proj_weights, b_fg, phase, tri, cos_t, sin_t,
            (w_out[l], w_up[l], w_down[l]), seq=seq, tm=tm_in)
        tok3 = tok.reshape(bsz, seq, TOK_W)
        fox = _fox(feat_t, tok3, aug.reshape(bsz, seq, LANES))
        swa = _swa(feat_t, tok3, swa_sinks[l])
        x2 = _outproj(fox, swa, x2, mod3,
                      g_post_mix[l], w_o, seq=seq)
        x2 = _mlp(x2, mod3, g_pre_mlp[l], g_post_mlp[l], w_u, w_d, seq=seq)
    return x2.reshape(bsz, seq, d)
```

```python
import functools
import math

import jax
import jax.numpy as jnp
from jax import lax
from jax.experimental import pallas as pl
from jax.experimental.pallas import tpu as pltpu

HEAD_DIM = 128
N_FOX_HEADS = 8
N_SWA_HEADS = 8
N_SWA_KV_HEADS = 2
SWA_GROUP = N_SWA_HEADS // N_SWA_KV_HEADS
SWA_WINDOW = 128
ROPE_THETA = 10000.0
NORM_EPS = 1e-6
N_MOD = 6
MASK_VALUE = -1e30
LOG2E = math.log2(math.e)
Q_SCALE = HEAD_DIM ** -0.5 * LOG2E

FOX_W = N_FOX_HEADS * HEAD_DIM
SWA_Q_W = N_SWA_HEADS * HEAD_DIM
SWA_KV_W = N_SWA_KV_HEADS * HEAD_DIM
TOK_W = FOX_W + SWA_KV_W
ROW_FQ = 0
ROW_FV = FOX_W
ROW_SQ = 2 * FOX_W
ROW_SV = ROW_SQ + SWA_Q_W
FEAT_W = ROW_SV + SWA_KV_W
CUM_SPLIT = 3
FG_ROWS = 32
GATE_PAD = 16
LANES = 128

V7X_VMEM_LIMIT = 60 * 1024 * 1024

BF16 = jnp.bfloat16
F32 = jnp.float32
_NT = (((1,), (1,)), ((), ()))
_TN = (((0,), (0,)), ((), ()))


NORM_ROWS = 16


def _inv_rms(x):
    return lax.rsqrt(jnp.mean(x * x, axis=-1, keepdims=True) + NORM_EPS)


def _norm_modulate(x_ref, gain_row, shift_row, out_ref, row0=0, nrows=None):
    nrows = x_ref.shape[0] if nrows is None else nrows
    for c in range(nrows // NORM_ROWS):
        rows = slice(row0 + c * NORM_ROWS, row0 + (c + 1) * NORM_ROWS)
        x = x_ref[rows, :]
        out_ref[rows, :] = ((x * _inv_rms(x)) * gain_row + shift_row).astype(out_ref.dtype)


def _norm_gate_residual(y_ref, x_ref, gate_row, row0=0, nrows=None):
    nrows = x_ref.shape[0] if nrows is None else nrows
    for c in range(nrows // NORM_ROWS):
        rows = slice(row0 + c * NORM_ROWS, row0 + (c + 1) * NORM_ROWS)
        y = y_ref[rows, :]
        y_ref[rows, :] = x_ref[rows, :] + (y * _inv_rms(y)) * gate_row


def _resident(shape, index_map):
    return pl.BlockSpec(shape, index_map, pipeline_mode=pl.Buffered(1))


def _mod_kernel(c_ref, w_ref, b_ref, o_ref):
    c = c_ref[...]
    cond = c * jax.nn.sigmoid(c)
    o_ref[...] = jnp.dot(cond.astype(BF16), w_ref[...].astype(BF16),
                         preferred_element_type=F32) + b_ref[...]


def _mod(c, w, b, *, tn=1024):
    bsz, d = c.shape
    n = w.shape[1]
    return pl.pallas_call(
        _mod_kernel,
        grid=(n // tn,),
        in_specs=[pl.BlockSpec((bsz, d), lambda j: (0, 0)),
                  pl.BlockSpec((d, tn), lambda j: (0, j)),
                  pl.BlockSpec((1, tn), lambda j: (0, j))],
        out_specs=pl.BlockSpec((bsz, tn), lambda j: (0, j)),
        out_shape=jax.ShapeDtypeStruct((bsz, n), F32),
        compiler_params=pltpu.CompilerParams(
            dimension_semantics=("arbitrary",), vmem_limit_bytes=V7X_VMEM_LIMIT),
        name="mod",
    )(c, w, b.reshape(1, n))


def _inproj_kernel(x_ref, mod_ref, g_ref, wfk_ref, wa_ref, wb_ref, bfg_ref, phase_ref,
                   tri_ref, cost_ref, sint_ref, wo_ref, wu_ref, wd_ref,
                   tok_ref, aug_ref, feat_ref, wo_bf_ref, wu_bf_ref, wd_bf_ref,
                   h_scr, carry_scr, *, tiles_per_seq):
    i = pl.program_id(0)
    hd = HEAD_DIM
    half = hd // 2

    _norm_modulate(x_ref, g_ref[...] * (1.0 + mod_ref[0, 1:2, :]), mod_ref[0, 0:1, :], h_scr)

    half_rows = h_scr.shape[0] // 2
    for a in range(2):
        rows = slice(a * half_rows, (a + 1) * half_rows)
        tok_ref[rows, :FOX_W] = jnp.dot(h_scr[rows, :], wfk_ref[...],
                                        preferred_element_type=F32).astype(BF16)

    def feat(w):
        return lax.dot_general(w, h_scr[...], _NT, preferred_element_type=F32)

    gate_row0 = 3 * FOX_W
    svfg = feat(jnp.concatenate(
        [wb_ref[SWA_Q_W + SWA_KV_W:, :], wa_ref[gate_row0:, :]], axis=0))
    feat_ref[ROW_SV:, :] = svfg[:SWA_KV_W].astype(BF16)
    z8 = svfg[SWA_KV_W:SWA_KV_W + N_FOX_HEADS]
    z = jnp.concatenate([z8] * CUM_SPLIT + [jnp.zeros_like(z8)], axis=0) + bfg_ref[...]
    logf = jnp.minimum(z, 0.0) - jnp.log1p(jnp.exp(-jnp.abs(z)))
    p0 = logf.astype(BF16)
    r0 = logf - p0.astype(F32)
    p1 = r0.astype(BF16)
    p2 = (r0 - p1.astype(F32)).astype(BF16)

    feat_ref[ROW_FQ:ROW_FQ + FOX_W, :] = (feat(wa_ref[:FOX_W, :]) * Q_SCALE).astype(BF16)
    wo_bf_ref[...] = wo_ref[...].astype(BF16)
    wu_bf_ref[...] = wu_ref[...].astype(BF16)

    cs = jnp.dot(jnp.concatenate([p0, p1, p2], axis=0), tri_ref[...],
                 preferred_element_type=F32)

    @pl.when(i % tiles_per_seq == 0)
    def _():
        carry_scr[...] = jnp.zeros_like(carry_scr)

    cum = cs[:FG_ROWS] + cs[FG_ROWS:2 * FG_ROWS] + cs[2 * FG_ROWS:] + carry_scr[...]
    carry_scr[...] = cum[:, cum.shape[1] - 1:]
    negc = cum * (-LOG2E)
    hi = negc.astype(BF16).astype(F32)
    rem = negc - hi
    mid = rem.astype(BF16).astype(F32)
    lo = rem - mid
    ph = phase_ref[...]
    pieces = jnp.where(ph == 0, hi, jnp.where(ph == 1, mid, lo))
    pad = jnp.zeros((LANES - FG_ROWS, pieces.shape[1]), F32)
    aug_ref[...] = jnp.concatenate([pieces, pad], axis=0).T.astype(BF16)

    wd_bf_ref[...] = wd_ref[...].astype(BF16)
    sqk = feat(wb_ref[:SWA_Q_W + SWA_KV_W, :])
    feat_ref[ROW_FV:ROW_FV + FOX_W, :] = feat(wa_ref[2 * FOX_W:3 * FOX_W, :]).astype(BF16)
    cos_t, sin_t = cost_ref[...], sint_ref[...]
    for hh in range(N_SWA_HEADS + N_SWA_KV_HEADS):
        x1 = sqk[hh * hd:hh * hd + half, :]
        x2 = sqk[hh * hd + half:(hh + 1) * hd, :]
        lo_half = x1 * cos_t - x2 * sin_t
        hi_half = x2 * cos_t + x1 * sin_t
        if hh < N_SWA_HEADS:
            base = ROW_SQ + hh * hd
            feat_ref[base:base + half, :] = (lo_half * Q_SCALE).astype(BF16)
            feat_ref[base + half:base + hd, :] = (hi_half * Q_SCALE).astype(BF16)
        else:
            col = FOX_W + (hh - N_SWA_HEADS) * hd
            tok_ref[:, col:col + hd] = jnp.concatenate(
                [lo_half, hi_half], axis=0).T.astype(BF16)


def _inproj(x2, mod3, g, proj_weights, b_fg, phase, tri, cos_t, sin_t, later_weights,
            *, seq, tm):
    t, d = x2.shape
    steps = t // tm
    tiles_per_seq = seq // tm
    half = HEAD_DIM // 2
    kern = functools.partial(_inproj_kernel, tiles_per_seq=tiles_per_seq)
    const = lambda i: (0, 0)
    pos_t = lambda i: (0, i % tiles_per_seq)
    slab = lambda i: (i, 0)
    slab_specs = [pl.BlockSpec((w.shape[0] // steps, w.shape[1]), slab) for w in later_weights]
    return pl.pallas_call(
        kern,
        grid=(steps,),
        in_specs=[
            pl.BlockSpec((tm, d), lambda i: (i, 0)),
            pl.BlockSpec((1, N_MOD, d), lambda i: (i // tiles_per_seq, 0, 0)),
            _resident((1, d), const),
        ] + [_resident(w.shape, const) for w in proj_weights] + [
            _resident((FG_ROWS, 1), const),
            _resident((FG_ROWS, 1), const),
            _resident((tm, tm), const),
            pl.BlockSpec((half, tm), pos_t),
            pl.BlockSpec((half, tm), pos_t),
        ] + slab_specs,
        out_specs=[
            pl.BlockSpec((tm, TOK_W), lambda i: (i, 0)),
            pl.BlockSpec((tm, LANES), lambda i: (i, 0)),
            pl.BlockSpec((FEAT_W, tm), lambda i: (0, i)),
        ] + slab_specs,
        out_shape=[jax.ShapeDtypeStruct((t, TOK_W), BF16),
                   jax.ShapeDtypeStruct((t, LANES), BF16),
                   jax.ShapeDtypeStruct((FEAT_W, t), BF16)]
        + [jax.ShapeDtypeStruct(w.shape, BF16) for w in later_weights],
        scratch_shapes=[pltpu.VMEM((tm, d), BF16), pltpu.VMEM((FG_ROWS, 1), F32)],
        compiler_params=pltpu.CompilerParams(
            dimension_semantics=("arbitrary",), vmem_limit_bytes=V7X_VMEM_LIMIT),
        name="in_proj",
    )(x2, mod3, g.reshape(1, d), *proj_weights, b_fg, phase, tri, cos_t, sin_t,
      *later_weights)


def _fox_kernel(qt_ref, k_ref, aug_ref, vt_ref, o_ref, *, blk, heads):
    hd = HEAD_DIM
    seq = k_ref.shape[1]
    nblk = seq // blk
    r = lax.broadcasted_iota(jnp.int32, (LANES, blk), 0)
    kk = lax.broadcasted_iota(jnp.int32, (blk, blk), 0)
    qq = lax.broadcasted_iota(jnp.int32, (blk, blk), 1)
    causal = kk <= qq
    onehot_t = []
    for hh in range(heads):
        hidx = pl.program_id(1) * heads + hh
        sel = ((r & (N_FOX_HEADS - 1)) == hidx) & (r < CUM_SPLIT * N_FOX_HEADS)
        onehot_t.append(jnp.where(sel, 1.0, 0.0).astype(BF16))

    def scores(hh, i):
        nk = (i + 1) * blk
        cols = slice(hh * hd, (hh + 1) * hd)
        q_aug = jnp.concatenate([qt_ref[cols, i * blk:nk], onehot_t[hh]], axis=0)
        k_aug = jnp.concatenate([k_ref[0, :nk, cols], aug_ref[0, :nk, :]], axis=1)
        s = jnp.dot(k_aug, q_aug, preferred_element_type=F32)
        s_diag = jnp.where(causal, s[nk - blk:, :], MASK_VALUE)
        s = s_diag if i == 0 else jnp.concatenate([s[:nk - blk, :], s_diag], axis=0)
        return s, jnp.max(s, axis=0, keepdims=True)

    s_next = [scores(hh, 0) for hh in range(heads)]
    for i in range(nblk):
        qs = slice(i * blk, (i + 1) * blk)
        nk = (i + 1) * blk
        s_cur = s_next
        if i + 1 < nblk:
            s_next = [scores(hh, i + 1) for hh in range(heads)]
        for hh in range(heads):
            cols = slice(hh * hd, (hh + 1) * hd)
            s, m = s_cur[hh]
            p = jnp.exp2(s - m)
            l = jnp.sum(p, axis=0, keepdims=True)
            acc = jnp.dot(vt_ref[cols, :nk], p.astype(BF16), preferred_element_type=F32)
            o_ref[cols, qs] = (acc * (1.0 / l)).astype(o_ref.dtype)


def _fox(feat_t, tok3, aug3, *, blk=256, heads=4):
    bsz, seq, _ = tok3.shape
    hw = heads * HEAD_DIM
    qb, vb = ROW_FQ // hw, ROW_FV // hw
    kern = functools.partial(_fox_kernel, blk=blk, heads=heads)
    return pl.pallas_call(
        kern,
        grid=(bsz, N_FOX_HEADS // heads),
        in_specs=[
            pl.BlockSpec((hw, seq), lambda b, h: (qb + h, b)),
            pl.BlockSpec((1, seq, hw), lambda b, h: (b, 0, h)),
            pl.BlockSpec((1, seq, LANES), lambda b, h: (b, 0, 0)),
            pl.BlockSpec((hw, seq), lambda b, h: (vb + h, b)),
        ],
        out_specs=pl.BlockSpec((hw, seq), lambda b, h: (h, b)),
        out_shape=jax.ShapeDtypeStruct((FOX_W, bsz * seq), BF16),
        compiler_params=pltpu.CompilerParams(
            dimension_semantics=("arbitrary", "arbitrary"),
            vmem_limit_bytes=V7X_VMEM_LIMIT),
        name="fox",
    )(feat_t, tok3, aug3, feat_t)


def _swa_kernel(qt_ref, k_ref, vt_ref, sink_ref, o_ref):
    w, hd = SWA_WINDOW, HEAD_DIM
    seq = k_ref.shape[1]
    gl = SWA_GROUP * w
    gw = SWA_GROUP * hd
    groups = range(N_SWA_KV_HEADS)
    sinks = [jnp.concatenate(
        [jnp.full((1, w), sink_ref[g * SWA_GROUP + r] * LOG2E, F32) for r in range(SWA_GROUP)],
        axis=1) for g in groups]
    kk = lax.broadcasted_iota(jnp.int32, (2 * w, gl), 0)
    qq = lax.broadcasted_iota(jnp.int32, (2 * w, gl), 1) & (w - 1)
    band = (kk > qq) & (kk - w <= qq)
    bias = jnp.where(band, 0.0, MASK_VALUE)
    bias_first = bias[w:, :]

    def scores(g, n):
        q_t = jnp.concatenate(
            [qt_ref[g * gw + r * hd:g * gw + (r + 1) * hd, n * w:(n + 1) * w]
             for r in range(SWA_GROUP)], axis=1)
        ks = slice(max(n - 1, 0) * w, (n + 1) * w)
        s = jnp.dot(k_ref[0, ks, g * hd:(g + 1) * hd], q_t,
                    preferred_element_type=F32)
        s = s + (bias if n > 0 else bias_first)
        return s, jnp.maximum(jnp.max(s, axis=0, keepdims=True), sinks[g])

    nblk = seq // w
    s_next = [scores(g, 0) for g in groups]
    for n in range(nblk):
        ks = slice(max(n - 1, 0) * w, (n + 1) * w)
        s_cur = s_next
        if n + 1 < nblk:
            s_next = [scores(g, n + 1) for g in groups]
        for g in groups:
            s, m = s_cur[g]
            p = jnp.exp2(s - m)
            denom = jnp.sum(p, axis=0, keepdims=True) + jnp.exp2(sinks[g] - m)
            o_t = jnp.dot(vt_ref[g * hd:(g + 1) * hd, ks], p.astype(BF16),
                          preferred_element_type=F32)
            o_t = o_t * (1.0 / denom)
            for r in range(SWA_GROUP):
                col = g * gw + r * hd
                o_ref[col:col + hd, n * w:(n + 1) * w] = (
                    o_t[:, r * w:(r + 1) * w].astype(o_ref.dtype))


def _swa(feat_t, tok3, sinks):
    bsz, seq, _ = tok3.shape
    qb, kb, vb = ROW_SQ // SWA_Q_W, FOX_W // SWA_KV_W, ROW_SV // SWA_KV_W
    return pl.pallas_call(
        _swa_kernel,
        grid=(bsz,),
        in_specs=[
            pl.BlockSpec((SWA_Q_W, seq), lambda b: (qb, b)),
            pl.BlockSpec((1, seq, SWA_KV_W), lambda b: (b, 0, kb)),
            pl.BlockSpec((SWA_KV_W, seq), lambda b: (vb, b)),
            pl.BlockSpec(memory_space=pltpu.SMEM),
        ],
        out_specs=pl.BlockSpec((SWA_Q_W, seq), lambda b: (0, b)),
        out_shape=jax.ShapeDtypeStruct((SWA_Q_W, bsz * seq), BF16),
        compiler_params=pltpu.CompilerParams(
            dimension_semantics=("arbitrary",), vmem_limit_bytes=V7X_VMEM_LIMIT),
        name="swa",
    )(feat_t, tok3, feat_t, sinks)


OUTPROJ_ROW_PARTS = 2


def _outproj_kernel(fox_ref, swa_ref, x_ref, mod_ref, g_ref, w_ref, o_ref):
    part = x_ref.shape[0] // OUTPROJ_ROW_PARTS
    gate = mod_ref[0, 2:3, :] * g_ref[...]
    for a in range(OUTPROJ_ROW_PARTS):
        rows = slice(a * part, (a + 1) * part)
        mix_t = jnp.concatenate([fox_ref[:, rows], swa_ref[:, rows]], axis=0)
        o_ref[rows, :] = lax.dot_general(mix_t, w_ref[...], _TN, preferred_element_type=F32)
    for a in range(OUTPROJ_ROW_PARTS):
        _norm_gate_residual(o_ref, x_ref, gate, a * part, part)


def _outproj(fox2, swa2, x2, mod3, g, w_out_bf, *, seq, tm=1024):
    t, d = x2.shape
    tiles_per_seq = seq // tm
    return pl.pallas_call(
        _outproj_kernel,
        grid=(t // tm,),
        in_specs=[
            pl.BlockSpec((FOX_W, tm), lambda i: (0, i)),
            pl.BlockSpec((SWA_Q_W, tm), lambda i: (0, i)),
            pl.BlockSpec((tm, d), lambda i: (i, 0)),
            pl.BlockSpec((1, N_MOD, d), lambda i: (i // tiles_per_seq, 0, 0)),
            _resident((1, d), lambda i: (0, 0)),
            _resident((FOX_W + SWA_Q_W, d), lambda i: (0, 0)),
        ],
        out_specs=pl.BlockSpec((tm, d), lambda i: (i, 0)),
        out_shape=jax.ShapeDtypeStruct((t, d), F32),
        compiler_params=pltpu.CompilerParams(
            dimension_semantics=("arbitrary",), vmem_limit_bytes=V7X_VMEM_LIMIT),
        name="out_proj",
    )(fox2, swa2, x2, mod3, g.reshape(1, d), w_out_bf)


MLP_ROW_PARTS = 2


def _mlp_kernel(x_ref, mod_ref, gpre_ref, gpost_ref, wu_ref, wd_ref, o_ref, h_scr):
    j = pl.program_id(1)
    last = pl.num_programs(1) - 1
    part = x_ref.shape[0] // MLP_ROW_PARTS

    def step(first, final):
        if first:
            gain = gpre_ref[...] * (1.0 + mod_ref[0, 4:5, :])
            shift = mod_ref[0, 3:4, :]
        if final:
            gate = mod_ref[0, 5:6, :] * gpost_ref[...]
        ups = []
        for a in range(MLP_ROW_PARTS):
            rows = slice(a * part, (a + 1) * part)
            if first:
                _norm_modulate(x_ref, gain, shift, h_scr, a * part, part)
            ups.append(jnp.dot(h_scr[rows, :], wu_ref[...], preferred_element_type=F32))
        for a in range(MLP_ROW_PARTS):
            rows = slice(a * part, (a + 1) * part)
            act = jnp.square(jnp.maximum(ups[a], 0.0)).astype(BF16)
            y = jnp.dot(act, wd_ref[...], preferred_element_type=F32)
            if first:
                o_ref[rows, :] = y
            else:
                o_ref[rows, :] += y
            if final:
                _norm_gate_residual(o_ref, x_ref, gate, a * part, part)

    pl.when(j == 0)(lambda: step(True, False))
    pl.when((j > 0) & (j < last))(lambda: step(False, False))
    pl.when(j == last)(lambda: step(False, True))


def _mlp(x2, mod3, g_pre, g_post, w_up, w_down, *, seq, tm=1024, tf=1024):
    t, d = x2.shape
    ff = w_up.shape[1]
    tiles_per_seq = seq // tm
    return pl.pallas_call(
        _mlp_kernel,
        grid=(t // tm, ff // tf),
        in_specs=[
            pl.BlockSpec((tm, d), lambda i, j: (i, 0)),
            pl.BlockSpec((1, N_MOD, d), lambda i, j: (i // tiles_per_seq, 0, 0)),
            pl.BlockSpec((1, d), lambda i, j: (0, 0)),
            pl.BlockSpec((1, d), lambda i, j: (0, 0)),
            pl.BlockSpec((d, tf), lambda i, j: (0, j)),
            pl.BlockSpec((tf, d), lambda i, j: (j, 0)),
        ],
        out_specs=pl.BlockSpec((tm, d), lambda i, j: (i, 0)),
        out_shape=jax.ShapeDtypeStruct((t, d), F32),
        scratch_shapes=[pltpu.VMEM((tm, d), BF16)],
        compiler_params=pltpu.CompilerParams(
            dimension_semantics=("arbitrary", "arbitrary"),
            vmem_limit_bytes=V7X_VMEM_LIMIT),
        name="mlp",
    )(x2, mod3, g_pre.reshape(1, d), g_post.reshape(1, d), w_up, w_down)


def _rope_tables(seq):
    half = HEAD_DIM // 2
    inv_freq = 1.0 / (ROPE_THETA ** (jnp.arange(half, dtype=F32) * (2.0 / HEAD_DIM)))
    ang = jnp.arange(seq).astype(F32)[:, None] * inv_freq[None, :]
    return jnp.cos(ang).T, jnp.sin(ang).T


def kernel(x, c, w_mod, b_mod, g_pre_mix, g_post_mix, w_in, b_forget, swa_sinks,
           w_out, g_pre_mlp, g_post_mlp, w_up, w_down):
    bsz, seq, d = x.shape
    depth = w_mod.shape[0]
    t = bsz * seq
    tm_in = 512
    cos_t, sin_t = _rope_tables(seq)
    phase = (jnp.arange(FG_ROWS, dtype=jnp.int32) // N_FOX_HEADS).reshape(FG_ROWS, 1)
    tri = jnp.triu(jnp.ones((tm_in, tm_in), BF16))
    o_fk, o_fv, o_fg = FOX_W, 2 * FOX_W, 3 * FOX_W
    o_sq = o_fg + N_FOX_HEADS

    x2 = x.reshape(t, d)
    for l in range(depth):
        wi = w_in[l]
        w_t = wi.T.astype(BF16)
        proj_weights = (wi[:, o_fk:o_fv].astype(BF16), w_t[:o_fg + GATE_PAD], w_t[o_sq:])
        b_fg = jnp.concatenate(
            [b_forget[l].astype(F32)] * CUM_SPLIT
            + [jnp.zeros((FG_ROWS - CUM_SPLIT * N_FOX_HEADS,), F32)]).reshape(FG_ROWS, 1)

        mod3 = _mod(c, w_mod[l], b_mod[l]).reshape(bsz, N_MOD, d)
        tok, aug, feat_t, w_o, w_u, w_d = _inproj(
            x2, mod3, g_pre_mix[l], proj_weights, b_fg, phase, tri, cos_t, sin_t,
            (w_out[l], w_up[l], w_down[l]), seq=seq, tm=tm_in)
        tok3 = tok.reshape(bsz, seq, TOK_W)
        fox = _fox(feat_t, tok3, aug.reshape(bsz, seq, LANES))
        swa = _swa(feat_t, tok3, swa_sinks[l])
        x2 = _outproj(fox, swa, x2, mod3,
                      g_post_mix[l], w_o, seq=seq)
        x2 = _mlp(x2, mod3, g_pre_mlp[l], g_post_mlp[l], w_u, w_d, seq=seq)
    return x2.reshape(bsz, seq, d)
```

```python
import functools
import math

import jax
import jax.numpy as jnp
from jax import lax
from jax.experimental import pallas as pl
from jax.experimental.pallas import tpu as pltpu

HEAD_DIM = 128
N_FOX_HEADS = 8
N_SWA_HEADS = 8
N_SWA_KV_HEADS = 2
SWA_GROUP = N_SWA_HEADS // N_SWA_KV_HEADS
SWA_WINDOW = 128
ROPE_THETA = 10000.0
NORM_EPS = 1e-6
N_MOD = 6
MASK_VALUE = -1e30
LOG2E = math.log2(math.e)
Q_SCALE = HEAD_DIM ** -0.5 * LOG2E

FOX_W = N_FOX_HEADS * HEAD_DIM
SWA_Q_W = N_SWA_HEADS * HEAD_DIM
SWA_KV_W = N_SWA_KV_HEADS * HEAD_DIM
TOK_W = FOX_W + SWA_KV_W
ROW_FQ = 0
ROW_FV = FOX_W
ROW_SQ = 2 * FOX_W
ROW_SV = ROW_SQ + SWA_Q_W
FEAT_W = ROW_SV + SWA_KV_W
CUM_SPLIT = 3
FG_ROWS = 32
LANES = 128

V7X_VMEM_LIMIT = 60 * 1024 * 1024

BF16 = jnp.bfloat16
F32 = jnp.float32
_NT = (((1,), (1,)), ((), ()))
_TN = (((0,), (0,)), ((), ()))


NORM_ROWS = 16


def _inv_rms(x):
    return lax.rsqrt(jnp.mean(x * x, axis=-1, keepdims=True) + NORM_EPS)


def _norm_modulate(x_ref, gain_row, shift_row, out_ref, row0=0, nrows=None):
    nrows = x_ref.shape[0] if nrows is None else nrows
    for c in range(nrows // NORM_ROWS):
        rows = slice(row0 + c * NORM_ROWS, row0 + (c + 1) * NORM_ROWS)
        x = x_ref[rows, :]
        out_ref[rows, :] = ((x * _inv_rms(x)) * gain_row + shift_row).astype(out_ref.dtype)


def _norm_gate_residual(y_ref, x_ref, gate_row, row0=0, nrows=None):
    nrows = x_ref.shape[0] if nrows is None else nrows
    for c in range(nrows // NORM_ROWS):
        rows = slice(row0 + c * NORM_ROWS, row0 + (c + 1) * NORM_ROWS)
        y = y_ref[rows, :]
        y_ref[rows, :] = x_ref[rows, :] + (y * _inv_rms(y)) * gate_row


def _resident(shape, index_map):
    return pl.BlockSpec(shape, index_map, pipeline_mode=pl.Buffered(1))


def _mod_kernel(c_ref, w_ref, b_ref, o_ref):
    c = c_ref[...]
    cond = c * jax.nn.sigmoid(c)
    o_ref[...] = jnp.dot(cond.astype(BF16), w_ref[...].astype(BF16),
                         preferred_element_type=F32) + b_ref[...]


def _mod(c, w, b, *, tn=1024):
    bsz, d = c.shape
    n = w.shape[1]
    return pl.pallas_call(
        _mod_kernel,
        grid=(n // tn,),
        in_specs=[pl.BlockSpec((bsz, d), lambda j: (0, 0)),
                  pl.BlockSpec((d, tn), lambda j: (0, j)),
                  pl.BlockSpec((1, tn), lambda j: (0, j))],
        out_specs=pl.BlockSpec((bsz, tn), lambda j: (0, j)),
        out_shape=jax.ShapeDtypeStruct((bsz, n), F32),
        compiler_params=pltpu.CompilerParams(
            dimension_semantics=("arbitrary",), vmem_limit_bytes=V7X_VMEM_LIMIT),
        name="mod",
    )(c, w, b.reshape(1, n))


def _inproj_kernel(x_ref, mod_ref, g_ref, wfk_ref, wfq_ref, wfv_ref, wsqk_ref, wsvfg_ref,
                   bfg_ref, phase_ref,
                   tri_ref, cost_ref, sint_ref, wo_ref, wu_ref, wd_ref,
                   tok_ref, aug_ref, feat_ref, wo_bf_ref, wu_bf_ref, wd_bf_ref,
                   h_scr, carry_scr, *, tiles_per_seq):
    i = pl.program_id(0)
    hd = HEAD_DIM
    half = hd // 2

    _norm_modulate(x_ref, g_ref[...] * (1.0 + mod_ref[0, 1:2, :]), mod_ref[0, 0:1, :], h_scr)

    half_rows = h_scr.shape[0] // 2
    for a in range(2):
        rows = slice(a * half_rows, (a + 1) * half_rows)
        tok_ref[rows, :FOX_W] = jnp.dot(h_scr[rows, :], wfk_ref[...],
                                        preferred_element_type=F32).astype(BF16)

    def feat(w_ref):
        return lax.dot_general(w_ref[...], h_scr[...], _NT, preferred_element_type=F32)

    svfg = feat(wsvfg_ref)
    feat_ref[ROW_SV:, :] = svfg[:SWA_KV_W].astype(BF16)
    z = svfg[SWA_KV_W:] + bfg_ref[...]
    logf = jnp.minimum(z, 0.0) - jnp.log1p(jnp.exp(-jnp.abs(z)))
    p0 = logf.astype(BF16)
    r0 = logf - p0.astype(F32)
    p1 = r0.astype(BF16)
    p2 = (r0 - p1.astype(F32)).astype(BF16)

    feat_ref[ROW_FQ:ROW_FQ + FOX_W, :] = (feat(wfq_ref) * Q_SCALE).astype(BF16)
    wo_bf_ref[...] = wo_ref[...].astype(BF16)
    wu_bf_ref[...] = wu_ref[...].astype(BF16)

    cs = jnp.dot(jnp.concatenate([p0, p1, p2], axis=0), tri_ref[...],
                 preferred_element_type=F32)

    @pl.when(i % tiles_per_seq == 0)
    def _():
        carry_scr[...] = jnp.zeros_like(carry_scr)

    cum = cs[:FG_ROWS] + cs[FG_ROWS:2 * FG_ROWS] + cs[2 * FG_ROWS:] + carry_scr[...]
    carry_scr[...] = cum[:, cum.shape[1] - 1:]
    negc = cum * (-LOG2E)
    hi = negc.astype(BF16).astype(F32)
    rem = negc - hi
    mid = rem.astype(BF16).astype(F32)
    lo = rem - mid
    ph = phase_ref[...]
    pieces = jnp.where(ph == 0, hi, jnp.where(ph == 1, mid, lo))
    pad = jnp.zeros((LANES - FG_ROWS, pieces.shape[1]), F32)
    aug_ref[...] = jnp.concatenate([pieces, pad], axis=0).T.astype(BF16)

    wd_bf_ref[...] = wd_ref[...].astype(BF16)
    sqk = feat(wsqk_ref)
    feat_ref[ROW_FV:ROW_FV + FOX_W, :] = feat(wfv_ref).astype(BF16)
    cos_t, sin_t = cost_ref[...], sint_ref[...]
    for hh in range(N_SWA_HEADS + N_SWA_KV_HEADS):
        x1 = sqk[hh * hd:hh * hd + half, :]
        x2 = sqk[hh * hd + half:(hh + 1) * hd, :]
        lo_half = x1 * cos_t - x2 * sin_t
        hi_half = x2 * cos_t + x1 * sin_t
        if hh < N_SWA_HEADS:
            base = ROW_SQ + hh * hd
            feat_ref[base:base + half, :] = (lo_half * Q_SCALE).astype(BF16)
            feat_ref[base + half:base + hd, :] = (hi_half * Q_SCALE).astype(BF16)
        else:
            col = FOX_W + (hh - N_SWA_HEADS) * hd
            tok_ref[:, col:col + hd] = jnp.concatenate(
                [lo_half, hi_half], axis=0).T.astype(BF16)


def _inproj(x2, mod3, g, proj_weights, b_fg, phase, tri, cos_t, sin_t, later_weights,
            *, seq, tm):
    t, d = x2.shape
    steps = t // tm
    tiles_per_seq = seq // tm
    half = HEAD_DIM // 2
    kern = functools.partial(_inproj_kernel, tiles_per_seq=tiles_per_seq)
    const = lambda i: (0, 0)
    pos_t = lambda i: (0, i % tiles_per_seq)
    slab = lambda i: (i, 0)
    slab_specs = [pl.BlockSpec((w.shape[0] // steps, w.shape[1]), slab) for w in later_weights]
    return pl.pallas_call(
        kern,
        grid=(steps,),
        in_specs=[
            pl.BlockSpec((tm, d), lambda i: (i, 0)),
            pl.BlockSpec((1, N_MOD, d), lambda i: (i // tiles_per_seq, 0, 0)),
            _resident((1, d), const),
        ] + [_resident(w.shape, const) for w in proj_weights] + [
            _resident((FG_ROWS, 1), const),
            _resident((FG_ROWS, 1), const),
            _resident((tm, tm), const),
            pl.BlockSpec((half, tm), pos_t),
            pl.BlockSpec((half, tm), pos_t),
        ] + slab_specs,
        out_specs=[
            pl.BlockSpec((tm, TOK_W), lambda i: (i, 0)),
            pl.BlockSpec((tm, LANES), lambda i: (i, 0)),
            pl.BlockSpec((FEAT_W, tm), lambda i: (0, i)),
        ] + slab_specs,
        out_shape=[jax.ShapeDtypeStruct((t, TOK_W), BF16),
                   jax.ShapeDtypeStruct((t, LANES), BF16),
                   jax.ShapeDtypeStruct((FEAT_W, t), BF16)]
        + [jax.ShapeDtypeStruct(w.shape, BF16) for w in later_weights],
        scratch_shapes=[pltpu.VMEM((tm, d), BF16), pltpu.VMEM((FG_ROWS, 1), F32)],
        compiler_params=pltpu.CompilerParams(
            dimension_semantics=("arbitrary",), vmem_limit_bytes=V7X_VMEM_LIMIT),
        name="in_proj",
    )(x2, mod3, g.reshape(1, d), *proj_weights, b_fg, phase, tri, cos_t, sin_t,
      *later_weights)


def _fox_kernel(qt_ref, k_ref, aug_ref, vt_ref, o_ref, *, blk, heads):
    hd = HEAD_DIM
    seq = k_ref.shape[1]
    nblk = seq // blk
    r = lax.broadcasted_iota(jnp.int32, (LANES, blk), 0)
    kk = lax.broadcasted_iota(jnp.int32, (blk, blk), 0)
    qq = lax.broadcasted_iota(jnp.int32, (blk, blk), 1)
    causal = kk <= qq
    onehot_t = []
    for hh in range(heads):
        hidx = pl.program_id(1) * heads + hh
        sel = (r >= CUM_SPLIT * hidx) & (r < CUM_SPLIT * (hidx + 1))
        onehot_t.append(jnp.where(sel, 1.0, 0.0).astype(BF16))

    def scores(hh, i):
        nk = (i + 1) * blk
        cols = slice(hh * hd, (hh + 1) * hd)
        q_aug = jnp.concatenate([qt_ref[cols, i * blk:nk], onehot_t[hh]], axis=0)
        k_aug = jnp.concatenate([k_ref[0, :nk, cols], aug_ref[0, :nk, :]], axis=1)
        s = jnp.dot(k_aug, q_aug, preferred_element_type=F32)
        s_diag = jnp.where(causal, s[nk - blk:, :], MASK_VALUE)
        s = s_diag if i == 0 else jnp.concatenate([s[:nk - blk, :], s_diag], axis=0)
        return s, jnp.max(s, axis=0, keepdims=True)

    s_next = [scores(hh, 0) for hh in range(heads)]
    for i in range(nblk):
        qs = slice(i * blk, (i + 1) * blk)
        nk = (i + 1) * blk
        s_cur = s_next
        if i + 1 < nblk:
            s_next = [scores(hh, i + 1) for hh in range(heads)]
        for hh in range(heads):
            cols = slice(hh * hd, (hh + 1) * hd)
            s, m = s_cur[hh]
            p = jnp.exp2(s - m)
            l = jnp.sum(p, axis=0, keepdims=True)
            acc = jnp.dot(vt_ref[cols, :nk], p.astype(BF16), preferred_element_type=F32)
            o_ref[cols, qs] = (acc * (1.0 / l)).astype(o_ref.dtype)


def _fox(feat_t, tok3, aug3, *, blk=256, heads=4):
    bsz, seq, _ = tok3.shape
    hw = heads * HEAD_DIM
    qb, vb = ROW_FQ // hw, ROW_FV // hw
    kern = functools.partial(_fox_kernel, blk=blk, heads=heads)
    return pl.pallas_call(
        kern,
        grid=(bsz, N_FOX_HEADS // heads),
        in_specs=[
            pl.BlockSpec((hw, seq), lambda b, h: (qb + h, b)),
            pl.BlockSpec((1, seq, hw), lambda b, h: (b, 0, h)),
            pl.BlockSpec((1, seq, LANES), lambda b, h: (b, 0, 0)),
            pl.BlockSpec((hw, seq), lambda b, h: (vb + h, b)),
        ],
        out_specs=pl.BlockSpec((hw, seq), lambda b, h: (h, b)),
        out_shape=jax.ShapeDtypeStruct((FOX_W, bsz * seq), BF16),
        compiler_params=pltpu.CompilerParams(
            dimension_semantics=("arbitrary", "arbitrary"),
            vmem_limit_bytes=V7X_VMEM_LIMIT),
        name="fox",
    )(feat_t, tok3, aug3, feat_t)


def _swa_kernel(qt_ref, k_ref, vt_ref, sink_ref, o_ref):
    w, hd = SWA_WINDOW, HEAD_DIM
    seq = k_ref.shape[1]
    gl = SWA_GROUP * w
    gw = SWA_GROUP * hd
    groups = range(N_SWA_KV_HEADS)
    sinks = [jnp.concatenate(
        [jnp.full((1, w), sink_ref[g * SWA_GROUP + r] * LOG2E, F32) for r in range(SWA_GROUP)],
        axis=1) for g in groups]
    kk = lax.broadcasted_iota(jnp.int32, (w, gl), 0)
    qq = lax.broadcasted_iota(jnp.int32, (w, gl), 1) & (w - 1)
    from_prev = kk > qq

    def scores(g, n):
        q_t = jnp.concatenate(
            [qt_ref[g * gw + r * hd:g * gw + (r + 1) * hd, n * w:(n + 1) * w]
             for r in range(SWA_GROUP)], axis=1)
        ks = slice(max(n - 1, 0) * w, (n + 1) * w)
        s = jnp.dot(k_ref[0, ks, g * hd:(g + 1) * hd], q_t,
                    preferred_element_type=F32)
        packed = jnp.where(from_prev, s[:w, :] if n > 0 else MASK_VALUE, s[s.shape[0] - w:, :])
        return packed, jnp.maximum(jnp.max(packed, axis=0, keepdims=True), sinks[g])

    nblk = seq // w
    s_next = [scores(g, 0) for g in groups]
    for n in range(nblk):
        ks = slice(max(n - 1, 0) * w, (n + 1) * w)
        s_cur = s_next
        if n + 1 < nblk:
            s_next = [scores(g, n + 1) for g in groups]
        for g in groups:
            s, m = s_cur[g]
            p = jnp.exp2(s - m)
            denom = jnp.sum(p, axis=0, keepdims=True) + jnp.exp2(sinks[g] - m)
            p_own = jnp.where(from_prev, 0.0, p).astype(BF16)
            if n > 0:
                p_own = jnp.concatenate(
                    [jnp.where(from_prev, p, 0.0).astype(BF16), p_own], axis=0)
            o_t = jnp.dot(vt_ref[g * hd:(g + 1) * hd, ks], p_own, preferred_element_type=F32)
            o_t = o_t * (1.0 / denom)
            for r in range(SWA_GROUP):
                col = g * gw + r * hd
                o_ref[col:col + hd, n * w:(n + 1) * w] = (
                    o_t[:, r * w:(r + 1) * w].astype(o_ref.dtype))


def _swa(feat_t, tok3, sinks):
    bsz, seq, _ = tok3.shape
    qb, kb, vb = ROW_SQ // SWA_Q_W, FOX_W // SWA_KV_W, ROW_SV // SWA_KV_W
    return pl.pallas_call(
        _swa_kernel,
        grid=(bsz,),
        in_specs=[
            pl.BlockSpec((SWA_Q_W, seq), lambda b: (qb, b)),
            pl.BlockSpec((1, seq, SWA_KV_W), lambda b: (b, 0, kb)),
            pl.BlockSpec((SWA_KV_W, seq), lambda b: (vb, b)),
            pl.BlockSpec(memory_space=pltpu.SMEM),
        ],
        out_specs=pl.BlockSpec((SWA_Q_W, seq), lambda b: (0, b)),
        out_shape=jax.ShapeDtypeStruct((SWA_Q_W, bsz * seq), BF16),
        compiler_params=pltpu.CompilerParams(
            dimension_semantics=("arbitrary",), vmem_limit_bytes=V7X_VMEM_LIMIT),
        name="swa",
    )(feat_t, tok3, feat_t, sinks)


OUTPROJ_ROW_PARTS = 2


def _outproj_kernel(fox_ref, swa_ref, x_ref, mod_ref, g_ref, wf_ref, ws_ref, o_ref):
    part = x_ref.shape[0] // OUTPROJ_ROW_PARTS
    gate = mod_ref[0, 2:3, :] * g_ref[...]
    for a in range(OUTPROJ_ROW_PARTS):
        rows = slice(a * part, (a + 1) * part)
        o_ref[rows, :] = (
            lax.dot_general(fox_ref[:, rows], wf_ref[...], _TN, preferred_element_type=F32)
            + lax.dot_general(swa_ref[:, rows], ws_ref[...], _TN, preferred_element_type=F32))
    for a in range(OUTPROJ_ROW_PARTS):
        _norm_gate_residual(o_ref, x_ref, gate, a * part, part)


def _outproj(fox2, swa2, x2, mod3, g, w_out_bf, *, seq, tm=1024):
    t, d = x2.shape
    tiles_per_seq = seq // tm
    return pl.pallas_call(
        _outproj_kernel,
        grid=(t // tm,),
        in_specs=[
            pl.BlockSpec((FOX_W, tm), lambda i: (0, i)),
            pl.BlockSpec((SWA_Q_W, tm), lambda i: (0, i)),
            pl.BlockSpec((tm, d), lambda i: (i, 0)),
            pl.BlockSpec((1, N_MOD, d), lambda i: (i // tiles_per_seq, 0, 0)),
            _resident((1, d), lambda i: (0, 0)),
            _resident((FOX_W, d), lambda i: (0, 0)),
            _resident((SWA_Q_W, d), lambda i: (FOX_W // SWA_Q_W, 0)),
        ],
        out_specs=pl.BlockSpec((tm, d), lambda i: (i, 0)),
        out_shape=jax.ShapeDtypeStruct((t, d), F32),
        compiler_params=pltpu.CompilerParams(
            dimension_semantics=("arbitrary",), vmem_limit_bytes=V7X_VMEM_LIMIT),
        name="out_proj",
    )(fox2, swa2, x2, mod3, g.reshape(1, d), w_out_bf, w_out_bf)


MLP_ROW_PARTS = 2


def _mlp_kernel(x_ref, mod_ref, gpre_ref, gpost_ref, wu_ref, wd_ref, o_ref, h_scr):
    j = pl.program_id(1)
    last = pl.num_programs(1) - 1
    part = x_ref.shape[0] // MLP_ROW_PARTS

    def step(first, final):
        if first:
            gain = gpre_ref[...] * (1.0 + mod_ref[0, 4:5, :])
            shift = mod_ref[0, 3:4, :]
        if final:
            gate = mod_ref[0, 5:6, :] * gpost_ref[...]
        ups = []
        for a in range(MLP_ROW_PARTS):
            rows = slice(a * part, (a + 1) * part)
            if first:
                _norm_modulate(x_ref, gain, shift, h_scr, a * part, part)
            ups.append(jnp.dot(h_scr[rows, :], wu_ref[...], preferred_element_type=F32))
        for a in range(MLP_ROW_PARTS):
            rows = slice(a * part, (a + 1) * part)
            act = jnp.square(jnp.maximum(ups[a], 0.0)).astype(BF16)
            y = jnp.dot(act, wd_ref[...], preferred_element_type=F32)
            if first:
                o_ref[rows, :] = y
            else:
                o_ref[rows, :] += y
            if final:
                _norm_gate_residual(o_ref, x_ref, gate, a * part, part)

    pl.when(j == 0)(lambda: step(True, False))
    pl.when((j > 0) & (j < last))(lambda: step(False, False))
    pl.when(j == last)(lambda: step(False, True))


def _mlp(x2, mod3, g_pre, g_post, w_up, w_down, *, seq, tm=1024, tf=1024):
    t, d = x2.shape
    ff = w_up.shape[1]
    tiles_per_seq = seq // tm
    return pl.pallas_call(
        _mlp_kernel,
        grid=(t // tm, ff // tf),
        in_specs=[
            pl.BlockSpec((tm, d), lambda i, j: (i, 0)),
            pl.BlockSpec((1, N_MOD, d), lambda i, j: (i // tiles_per_seq, 0, 0)),
            pl.BlockSpec((1, d), lambda i, j: (0, 0)),
            pl.BlockSpec((1, d), lambda i, j: (0, 0)),
            pl.BlockSpec((d, tf), lambda i, j: (0, j)),
            pl.BlockSpec((tf, d), lambda i, j: (j, 0)),
        ],
        out_specs=pl.BlockSpec((tm, d), lambda i, j: (i, 0)),
        out_shape=jax.ShapeDtypeStruct((t, d), F32),
        scratch_shapes=[pltpu.VMEM((tm, d), BF16)],
        compiler_params=pltpu.CompilerParams(
            dimension_semantics=("arbitrary", "arbitrary"),
            vmem_limit_bytes=V7X_VMEM_LIMIT),
        name="mlp",
    )(x2, mod3, g_pre.reshape(1, d), g_post.reshape(1, d), w_up, w_down)


def _rope_tables(seq):
    half = HEAD_DIM // 2
    inv_freq = 1.0 / (ROPE_THETA ** (jnp.arange(half, dtype=F32) * (2.0 / HEAD_DIM)))
    ang = jnp.arange(seq).astype(F32)[:, None] * inv_freq[None, :]
    return jnp.cos(ang).T, jnp.sin(ang).T


def kernel(x, c, w_mod, b_mod, g_pre_mix, g_post_mix, w_in, b_forget, swa_sinks,
           w_out, g_pre_mlp, g_post_mlp, w_up, w_down):
    bsz, seq, d = x.shape
    depth = w_mod.shape[0]
    t = bsz * seq
    tm_in = 512
    cos_t, sin_t = _rope_tables(seq)
    phase = (jnp.arange(FG_ROWS, dtype=jnp.int32) % CUM_SPLIT).reshape(FG_ROWS, 1)
    tri = jnp.triu(jnp.ones((tm_in, tm_in), BF16))
    o_fk, o_fv, o_fg = FOX_W, 2 * FOX_W, 3 * FOX_W
    o_sq = o_fg + N_FOX_HEADS
    o_sk = o_sq + SWA_Q_W
    o_sv = o_sk + SWA_KV_W
    n_rep = CUM_SPLIT * N_FOX_HEADS

    x2 = x.reshape(t, d)
    for l in range(depth):
        wi = w_in[l]
        w_fg = jnp.zeros((FG_ROWS, d), F32).at[:n_rep].set(
            jnp.repeat(wi[:, o_fg:o_sq].T, CUM_SPLIT, axis=0))
        proj_weights = (
            wi[:, o_fk:o_fv].astype(BF16),
            wi[:, :o_fk].T.astype(BF16), wi[:, o_fv:o_fg].T.astype(BF16),
            wi[:, o_sq:o_sv].T.astype(BF16),
            jnp.concatenate([wi[:, o_sv:].T, w_fg], axis=0).astype(BF16))
        b_fg = jnp.zeros((FG_ROWS, 1), F32).at[:n_rep, 0].set(
            jnp.repeat(b_forget[l].astype(F32), CUM_SPLIT))

        mod3 = _mod(c, w_mod[l], b_mod[l]).reshape(bsz, N_MOD, d)
        tok, aug, feat_t, w_o, w_u, w_d = _inproj(
            x2, mod3, g_pre_mix[l], proj_weights, b_fg, phase, tri, cos_t, sin_t,
            (w_out[l], w_up[l], w_down[l]), seq=seq, tm=tm_in)
        tok3 = tok.reshape(bsz, seq, TOK_W)
        fox = _fox(feat_t, tok3, aug.reshape(bsz, seq, LANES))
        swa = _swa(feat_t, tok3, swa_sinks[l])
        x2 = _outproj(fox, swa, x2, mod3,
                      g_post_mix[l], w_o, seq=seq)
        x2 = _mlp(x2, mod3, g_pre_mlp[l], g_post_mlp[l], w_u, w_d, seq=seq)
    return x2.reshape(bsz, seq, d)
```

```python
import functools
import math

import jax
import jax.numpy as jnp
from jax import lax
from jax.experimental import pallas as pl
from jax.experimental.pallas import tpu as pltpu

HEAD_DIM = 128
N_FOX_HEADS = 8
N_SWA_HEADS = 8
N_SWA_KV_HEADS = 2
SWA_GROUP = N_SWA_HEADS // N_SWA_KV_HEADS
SWA_WINDOW = 128
ROPE_THETA = 10000.0
NORM_EPS = 1e-6
N_MOD = 6
MASK_VALUE = -1e30
LOG2E = math.log2(math.e)
Q_SCALE = HEAD_DIM ** -0.5 * LOG2E

FOX_W = N_FOX_HEADS * HEAD_DIM
SWA_Q_W = N_SWA_HEADS * HEAD_DIM
SWA_KV_W = N_SWA_KV_HEADS * HEAD_DIM
TOK_W = FOX_W + SWA_KV_W
ROW_FQ = 0
ROW_FV = FOX_W
ROW_SQ = 2 * FOX_W
ROW_SV = ROW_SQ + SWA_Q_W
FEAT_W = ROW_SV + SWA_KV_W
CUM_SPLIT = 3
FG_ROWS = 32
LANES = 128

V7X_VMEM_LIMIT = 60 * 1024 * 1024

BF16 = jnp.bfloat16
F32 = jnp.float32
_NT = (((1,), (1,)), ((), ()))
_TN = (((0,), (0,)), ((), ()))


NORM_ROWS = 16


def _inv_rms(x):
    return lax.rsqrt(jnp.mean(x * x, axis=-1, keepdims=True) + NORM_EPS)


def _norm_modulate(x_ref, gain_row, shift_row, out_ref, row0=0, nrows=None):
    nrows = x_ref.shape[0] if nrows is None else nrows
    for c in range(nrows // NORM_ROWS):
        rows = slice(row0 + c * NORM_ROWS, row0 + (c + 1) * NORM_ROWS)
        x = x_ref[rows, :]
        out_ref[rows, :] = ((x * _inv_rms(x)) * gain_row + shift_row).astype(out_ref.dtype)


def _norm_gate_residual(y_ref, x_ref, gate_row, row0=0, nrows=None):
    nrows = x_ref.shape[0] if nrows is None else nrows
    for c in range(nrows // NORM_ROWS):
        rows = slice(row0 + c * NORM_ROWS, row0 + (c + 1) * NORM_ROWS)
        y = y_ref[rows, :]
        y_ref[rows, :] = x_ref[rows, :] + (y * _inv_rms(y)) * gate_row


def _resident(shape, index_map):
    return pl.BlockSpec(shape, index_map, pipeline_mode=pl.Buffered(1))


def _mod_kernel(c_ref, w_ref, b_ref, o_ref):
    c = c_ref[...]
    cond = c * jax.nn.sigmoid(c)
    o_ref[...] = jnp.dot(cond.astype(BF16), w_ref[...].astype(BF16),
                         preferred_element_type=F32) + b_ref[...]


def _mod(c, w, b, *, tn=1024):
    bsz, d = c.shape
    n = w.shape[1]
    return pl.pallas_call(
        _mod_kernel,
        grid=(n // tn,),
        in_specs=[pl.BlockSpec((bsz, d), lambda j: (0, 0)),
                  pl.BlockSpec((d, tn), lambda j: (0, j)),
                  pl.BlockSpec((1, tn), lambda j: (0, j))],
        out_specs=pl.BlockSpec((bsz, tn), lambda j: (0, j)),
        out_shape=jax.ShapeDtypeStruct((bsz, n), F32),
        compiler_params=pltpu.CompilerParams(
            dimension_semantics=("arbitrary",), vmem_limit_bytes=V7X_VMEM_LIMIT),
        name="mod",
    )(c, w, b.reshape(1, n))


def _inproj_kernel(x_ref, mod_ref, g_ref, wfk_ref, wfq_ref, wfv_ref, wsqk_ref, wsvfg_ref,
                   bfg_ref, phase_ref,
                   tri_ref, cost_ref, sint_ref, wo_ref,
                   tok_ref, aug_ref, feat_ref, wo_bf_ref,
                   h_scr, carry_scr, *, tiles_per_seq):
    i = pl.program_id(0)
    hd = HEAD_DIM
    half = hd // 2

    @pl.when(i % tiles_per_seq == 0)
    def _():
        carry_scr[...] = jnp.zeros_like(carry_scr)

    _norm_modulate(x_ref, g_ref[...] * (1.0 + mod_ref[0, 1:2, :]), mod_ref[0, 0:1, :], h_scr)

    half_rows = h_scr.shape[0] // 2
    for a in range(2):
        rows = slice(a * half_rows, (a + 1) * half_rows)
        tok_ref[rows, :FOX_W] = jnp.dot(h_scr[rows, :], wfk_ref[...],
                                        preferred_element_type=F32).astype(BF16)

    def feat(w_ref):
        return lax.dot_general(w_ref[...], h_scr[...], _NT, preferred_element_type=F32)

    svfg = feat(wsvfg_ref)
    feat_ref[ROW_SV:, :] = svfg[:SWA_KV_W].astype(BF16)
    z = svfg[SWA_KV_W:] + bfg_ref[...]
    logf = jnp.minimum(z, 0.0) - jnp.log1p(jnp.exp(-jnp.abs(z)))
    p0 = logf.astype(BF16)
    r0 = logf - p0.astype(F32)
    p1 = r0.astype(BF16)
    p2 = (r0 - p1.astype(F32)).astype(BF16)

    feat_ref[ROW_FQ:ROW_FQ + FOX_W, :] = (feat(wfq_ref) * Q_SCALE).astype(BF16)
    wo_bf_ref[...] = wo_ref[...].astype(BF16)

    cs = jnp.dot(jnp.concatenate([p0, p1, p2], axis=0), tri_ref[...],
                 preferred_element_type=F32)

    cum = cs[:FG_ROWS] + cs[FG_ROWS:2 * FG_ROWS] + cs[2 * FG_ROWS:] + carry_scr[...]
    carry_scr[...] = cum[:, cum.shape[1] - 1:]
    negc = cum * (-LOG2E)
    hi = negc.astype(BF16).astype(F32)
    rem = negc - hi
    mid = rem.astype(BF16).astype(F32)
    lo = rem - mid
    ph = phase_ref[...]
    pieces = jnp.where(ph == 0, hi, jnp.where(ph == 1, mid, lo))
    pad = jnp.zeros((LANES - FG_ROWS, pieces.shape[1]), F32)
    aug_ref[...] = jnp.concatenate([pieces, pad], axis=0).T.astype(BF16)

    sqk = feat(wsqk_ref)
    feat_ref[ROW_FV:ROW_FV + FOX_W, :] = feat(wfv_ref).astype(BF16)
    cos_t, sin_t = cost_ref[...], sint_ref[...]
    for hh in range(N_SWA_HEADS + N_SWA_KV_HEADS):
        x1 = sqk[hh * hd:hh * hd + half, :]
        x2 = sqk[hh * hd + half:(hh + 1) * hd, :]
        lo_half = x1 * cos_t - x2 * sin_t
        hi_half = x2 * cos_t + x1 * sin_t
        if hh < N_SWA_HEADS:
            base = ROW_SQ + hh * hd
            feat_ref[base:base + half, :] = (lo_half * Q_SCALE).astype(BF16)
            feat_ref[base + half:base + hd, :] = (hi_half * Q_SCALE).astype(BF16)
        else:
            col = FOX_W + (hh - N_SWA_HEADS) * hd
            tok_ref[:, col:col + hd] = jnp.concatenate(
                [lo_half, hi_half], axis=0).T.astype(BF16)


def _inproj(x2, mod3, g, proj_weights, b_fg, phase, tri, cos_t, sin_t, later_weights,
            *, seq, tm):
    t, d = x2.shape
    steps = t // tm
    tiles_per_seq = seq // tm
    half = HEAD_DIM // 2
    kern = functools.partial(_inproj_kernel, tiles_per_seq=tiles_per_seq)
    const = lambda i: (0, 0)
    pos_t = lambda i: (0, i % tiles_per_seq)
    slab = lambda i: (i, 0)
    slab_specs = [pl.BlockSpec((w.shape[0] // steps, w.shape[1]), slab) for w in later_weights]
    return pl.pallas_call(
        kern,
        grid=(steps,),
        in_specs=[
            pl.BlockSpec((tm, d), lambda i: (i, 0)),
            pl.BlockSpec((1, N_MOD, d), lambda i: (i // tiles_per_seq, 0, 0)),
            _resident((1, d), const),
        ] + [_resident(w.shape, const) for w in proj_weights] + [
            _resident((FG_ROWS, 1), const),
            _resident((FG_ROWS, 1), const),
            _resident((tm, tm), const),
            pl.BlockSpec((half, tm), pos_t),
            pl.BlockSpec((half, tm), pos_t),
        ] + slab_specs,
        out_specs=[
            pl.BlockSpec((tm, TOK_W), lambda i: (i, 0)),
            pl.BlockSpec((tm, LANES), lambda i: (i, 0)),
            pl.BlockSpec((FEAT_W, tm), lambda i: (0, i)),
        ] + slab_specs,
        out_shape=[jax.ShapeDtypeStruct((t, TOK_W), BF16),
                   jax.ShapeDtypeStruct((t, LANES), BF16),
                   jax.ShapeDtypeStruct((FEAT_W, t), BF16)]
        + [jax.ShapeDtypeStruct(w.shape, BF16) for w in later_weights],
        scratch_shapes=[pltpu.VMEM((tm, d), BF16), pltpu.VMEM((FG_ROWS, 1), F32)],
        compiler_params=pltpu.CompilerParams(
            dimension_semantics=("arbitrary",), vmem_limit_bytes=V7X_VMEM_LIMIT),
        name="in_proj",
    )(x2, mod3, g.reshape(1, d), *proj_weights, b_fg, phase, tri, cos_t, sin_t,
      *later_weights)


def _fox_kernel(qt_ref, k_ref, aug_ref, vt_ref, wu_ref, wd_ref, o_ref, wu_bf_ref, wd_bf_ref,
                *, blk, heads):
    hd = HEAD_DIM
    seq = k_ref.shape[1]
    nblk = seq // blk
    wu_bf_ref[...] = wu_ref[...].astype(BF16)
    wd_bf_ref[...] = wd_ref[...].astype(BF16)
    r = lax.broadcasted_iota(jnp.int32, (LANES, blk), 0)
    kk = lax.broadcasted_iota(jnp.int32, (blk, blk), 0)
    qq = lax.broadcasted_iota(jnp.int32, (blk, blk), 1)
    causal = kk <= qq
    onehot_t = []
    for hh in range(heads):
        hidx = pl.program_id(1) * heads + hh
        sel = (r >= CUM_SPLIT * hidx) & (r < CUM_SPLIT * (hidx + 1))
        onehot_t.append(jnp.where(sel, 1.0, 0.0).astype(BF16))

    def scores(hh, i):
        nk = (i + 1) * blk
        cols = slice(hh * hd, (hh + 1) * hd)
        q_aug = jnp.concatenate([qt_ref[cols, i * blk:nk], onehot_t[hh]], axis=0)
        k_aug = jnp.concatenate([k_ref[0, :nk, cols], aug_ref[0, :nk, :]], axis=1)
        s = jnp.dot(k_aug, q_aug, preferred_element_type=F32)
        s_diag = jnp.where(causal, s[nk - blk:, :], MASK_VALUE)
        s = s_diag if i == 0 else jnp.concatenate([s[:nk - blk, :], s_diag], axis=0)
        return s, jnp.max(s, axis=0, keepdims=True)

    s_next = [scores(hh, 0) for hh in range(heads)]
    for i in range(nblk):
        qs = slice(i * blk, (i + 1) * blk)
        nk = (i + 1) * blk
        s_cur = s_next
        if i + 1 < nblk:
            s_next = [scores(hh, i + 1) for hh in range(heads)]
        for hh in range(heads):
            cols = slice(hh * hd, (hh + 1) * hd)
            s, m = s_cur[hh]
            p = jnp.exp2(s - m)
            l = jnp.sum(p, axis=0, keepdims=True)
            acc = jnp.dot(vt_ref[cols, :nk], p.astype(BF16), preferred_element_type=F32)
            o_ref[cols, qs] = (acc * (1.0 / l)).astype(o_ref.dtype)


def _fox(feat_t, tok3, aug3, later_weights, *, blk=256, heads=4):
    bsz, seq, _ = tok3.shape
    hw = heads * HEAD_DIM
    hsteps = N_FOX_HEADS // heads
    qb, vb = ROW_FQ // hw, ROW_FV // hw
    kern = functools.partial(_fox_kernel, blk=blk, heads=heads)
    slab = lambda b, h: (b * hsteps + h, 0)
    slab_specs = [pl.BlockSpec((w.shape[0] // (bsz * hsteps), w.shape[1]), slab)
                  for w in later_weights]
    return pl.pallas_call(
        kern,
        grid=(bsz, hsteps),
        in_specs=[
            pl.BlockSpec((hw, seq), lambda b, h: (qb + h, b)),
            pl.BlockSpec((1, seq, hw), lambda b, h: (b, 0, h)),
            pl.BlockSpec((1, seq, LANES), lambda b, h: (b, 0, 0)),
            pl.BlockSpec((hw, seq), lambda b, h: (vb + h, b)),
        ] + slab_specs,
        out_specs=[pl.BlockSpec((hw, seq), lambda b, h: (h, b))] + slab_specs,
        out_shape=[jax.ShapeDtypeStruct((FOX_W, bsz * seq), BF16)]
        + [jax.ShapeDtypeStruct(w.shape, BF16) for w in later_weights],
        compiler_params=pltpu.CompilerParams(
            dimension_semantics=("arbitrary", "arbitrary"),
            vmem_limit_bytes=V7X_VMEM_LIMIT),
        name="fox",
    )(feat_t, tok3, aug3, feat_t, *later_weights)


def _swa_kernel(qt_ref, k_ref, vt_ref, sink_ref, o_ref):
    w, hd = SWA_WINDOW, HEAD_DIM
    seq = k_ref.shape[1]
    gl = SWA_GROUP * w
    gw = SWA_GROUP * hd
    groups = range(N_SWA_KV_HEADS)
    sinks = [jnp.concatenate(
        [jnp.full((1, w), sink_ref[g * SWA_GROUP + r] * LOG2E, F32) for r in range(SWA_GROUP)],
        axis=1) for g in groups]
    kk = lax.broadcasted_iota(jnp.int32, (w, gl), 0)
    qq = lax.broadcasted_iota(jnp.int32, (w, gl), 1) & (w - 1)
    from_prev = kk > qq

    def scores(g, n):
        q_t = jnp.concatenate(
            [qt_ref[g * gw + r * hd:g * gw + (r + 1) * hd, n * w:(n + 1) * w]
             for r in range(SWA_GROUP)], axis=1)
        ks = slice(max(n - 1, 0) * w, (n + 1) * w)
        s = jnp.dot(k_ref[0, ks, g * hd:(g + 1) * hd], q_t,
                    preferred_element_type=F32)
        packed = jnp.where(from_prev, s[:w, :] if n > 0 else MASK_VALUE, s[s.shape[0] - w:, :])
        return packed, jnp.maximum(jnp.max(packed, axis=0, keepdims=True), sinks[g])

    nblk = seq // w
    s_next = [scores(g, 0) for g in groups]
    for n in range(nblk):
        ks = slice(max(n - 1, 0) * w, (n + 1) * w)
        s_cur = s_next
        if n + 1 < nblk:
            s_next = [scores(g, n + 1) for g in groups]
        for g in groups:
            s, m = s_cur[g]
            p = jnp.exp2(s - m)
            denom = jnp.sum(p, axis=0, keepdims=True) + jnp.exp2(sinks[g] - m)
            p_own = jnp.where(from_prev, 0.0, p).astype(BF16)
            if n > 0:
                p_own = jnp.concatenate(
                    [jnp.where(from_prev, p, 0.0).astype(BF16), p_own], axis=0)
            o_t = jnp.dot(vt_ref[g * hd:(g + 1) * hd, ks], p_own, preferred_element_type=F32)
            o_t = o_t * (1.0 / denom)
            for r in range(SWA_GROUP):
                col = g * gw + r * hd
                o_ref[col:col + hd, n * w:(n + 1) * w] = (
                    o_t[:, r * w:(r + 1) * w].astype(o_ref.dtype))


def _swa(feat_t, tok3, sinks):
    bsz, seq, _ = tok3.shape
    qb, kb, vb = ROW_SQ // SWA_Q_W, FOX_W // SWA_KV_W, ROW_SV // SWA_KV_W
    return pl.pallas_call(
        _swa_kernel,
        grid=(bsz,),
        in_specs=[
            pl.BlockSpec((SWA_Q_W, seq), lambda b: (qb, b)),
            pl.BlockSpec((1, seq, SWA_KV_W), lambda b: (b, 0, kb)),
            pl.BlockSpec((SWA_KV_W, seq), lambda b: (vb, b)),
            pl.BlockSpec(memory_space=pltpu.SMEM),
        ],
        out_specs=pl.BlockSpec((SWA_Q_W, seq), lambda b: (0, b)),
        out_shape=jax.ShapeDtypeStruct((SWA_Q_W, bsz * seq), BF16),
        compiler_params=pltpu.CompilerParams(
            dimension_semantics=("arbitrary",), vmem_limit_bytes=V7X_VMEM_LIMIT),
        name="swa",
    )(feat_t, tok3, feat_t, sinks)


OUTPROJ_ROW_PARTS = 2


def _outproj_kernel(fox_ref, swa_ref, x_ref, mod_ref, g_ref, wf_ref, ws_ref, o_ref):
    part = x_ref.shape[0] // OUTPROJ_ROW_PARTS
    gate = mod_ref[0, 2:3, :] * g_ref[...]
    for a in range(OUTPROJ_ROW_PARTS):
        rows = slice(a * part, (a + 1) * part)
        o_ref[rows, :] = (
            lax.dot_general(fox_ref[:, rows], wf_ref[...], _TN, preferred_element_type=F32)
            + lax.dot_general(swa_ref[:, rows], ws_ref[...], _TN, preferred_element_type=F32))
    for a in range(OUTPROJ_ROW_PARTS):
        _norm_gate_residual(o_ref, x_ref, gate, a * part, part)


def _outproj(fox2, swa2, x2, mod3, g, w_out_bf, *, seq, tm=1024):
    t, d = x2.shape
    tiles_per_seq = seq // tm
    return pl.pallas_call(
        _outproj_kernel,
        grid=(t // tm,),
        in_specs=[
            pl.BlockSpec((FOX_W, tm), lambda i: (0, i)),
            pl.BlockSpec((SWA_Q_W, tm), lambda i: (0, i)),
            pl.BlockSpec((tm, d), lambda i: (i, 0)),
            pl.BlockSpec((1, N_MOD, d), lambda i: (i // tiles_per_seq, 0, 0)),
            _resident((1, d), lambda i: (0, 0)),
            _resident((FOX_W, d), lambda i: (0, 0)),
            _resident((SWA_Q_W, d), lambda i: (FOX_W // SWA_Q_W, 0)),
        ],
        out_specs=pl.BlockSpec((tm, d), lambda i: (i, 0)),
        out_shape=jax.ShapeDtypeStruct((t, d), F32),
        compiler_params=pltpu.CompilerParams(
            dimension_semantics=("arbitrary",), vmem_limit_bytes=V7X_VMEM_LIMIT),
        name="out_proj",
    )(fox2, swa2, x2, mod3, g.reshape(1, d), w_out_bf, w_out_bf)


MLP_ROW_PARTS = 2


def _mlp_kernel(x_ref, mod_ref, gpre_ref, gpost_ref, wu_ref, wd_ref, o_ref, h_scr):
    j = pl.program_id(1)
    last = pl.num_programs(1) - 1
    part = x_ref.shape[0] // MLP_ROW_PARTS

    def step(first, final):
        if first:
            gain = gpre_ref[...] * (1.0 + mod_ref[0, 4:5, :])
            shift = mod_ref[0, 3:4, :]
        if final:
            gate = mod_ref[0, 5:6, :] * gpost_ref[...]
        ups = []
        for a in range(MLP_ROW_PARTS):
            rows = slice(a * part, (a + 1) * part)
            if first:
                _norm_modulate(x_ref, gain, shift, h_scr, a * part, part)
            ups.append(jnp.dot(h_scr[rows, :], wu_ref[...], preferred_element_type=F32))
        for a in range(MLP_ROW_PARTS):
            rows = slice(a * part, (a + 1) * part)
            act = jnp.square(jnp.maximum(ups[a], 0.0)).astype(BF16)
            y = jnp.dot(act, wd_ref[...], preferred_element_type=F32)
            if first:
                o_ref[rows, :] = y
            else:
                o_ref[rows, :] += y
            if final:
                _norm_gate_residual(o_ref, x_ref, gate, a * part, part)

    pl.when(j == 0)(lambda: step(True, False))
    pl.when((j > 0) & (j < last))(lambda: step(False, False))
    pl.when(j == last)(lambda: step(False, True))


def _mlp(x2, mod3, g_pre, g_post, w_up, w_down, *, seq, tm=1024, tf=1024):
    t, d = x2.shape
    ff = w_up.shape[1]
    tiles_per_seq = seq // tm
    return pl.pallas_call(
        _mlp_kernel,
        grid=(t // tm, ff // tf),
        in_specs=[
            pl.BlockSpec((tm, d), lambda i, j: (i, 0)),
            pl.BlockSpec((1, N_MOD, d), lambda i, j: (i // tiles_per_seq, 0, 0)),
            pl.BlockSpec((1, d), lambda i, j: (0, 0)),
            pl.BlockSpec((1, d), lambda i, j: (0, 0)),
            pl.BlockSpec((d, tf), lambda i, j: (0, j)),
            pl.BlockSpec((tf, d), lambda i, j: (j, 0)),
        ],
        out_specs=pl.BlockSpec((tm, d), lambda i, j: (i, 0)),
        out_shape=jax.ShapeDtypeStruct((t, d), F32),
        scratch_shapes=[pltpu.VMEM((tm, d), BF16)],
        compiler_params=pltpu.CompilerParams(
            dimension_semantics=("arbitrary", "arbitrary"),
            vmem_limit_bytes=V7X_VMEM_LIMIT),
        name="mlp",
    )(x2, mod3, g_pre.reshape(1, d), g_post.reshape(1, d), w_up, w_down)


def _rope_tables(seq):
    half = HEAD_DIM // 2
    inv_freq = 1.0 / (ROPE_THETA ** (jnp.arange(half, dtype=F32) * (2.0 / HEAD_DIM)))
    ang = jnp.arange(seq).astype(F32)[:, None] * inv_freq[None, :]
    return jnp.cos(ang).T, jnp.sin(ang).T


def kernel(x, c, w_mod, b_mod, g_pre_mix, g_post_mix, w_in, b_forget, swa_sinks,
           w_out, g_pre_mlp, g_post_mlp, w_up, w_down):
    bsz, seq, d = x.shape
    depth = w_mod.shape[0]
    t = bsz * seq
    tm_in = 512
    cos_t, sin_t = _rope_tables(seq)
    phase = (jnp.arange(FG_ROWS, dtype=jnp.int32) % CUM_SPLIT).reshape(FG_ROWS, 1)
    tri = jnp.triu(jnp.ones((tm_in, tm_in), BF16))
    o_fk, o_fv, o_fg = FOX_W, 2 * FOX_W, 3 * FOX_W
    o_sq = o_fg + N_FOX_HEADS
    o_sk = o_sq + SWA_Q_W
    o_sv = o_sk + SWA_KV_W
    n_rep = CUM_SPLIT * N_FOX_HEADS

    x2 = x.reshape(t, d)
    for l in range(depth):
        wi = w_in[l]
        w_fg = jnp.zeros((FG_ROWS, d), F32).at[:n_rep].set(
            jnp.repeat(wi[:, o_fg:o_sq].T, CUM_SPLIT, axis=0))
        proj_weights = (
            wi[:, o_fk:o_fv].astype(BF16),
            wi[:, :o_fk].T.astype(BF16), wi[:, o_fv:o_fg].T.astype(BF16),
            wi[:, o_sq:o_sv].T.astype(BF16),
            jnp.concatenate([wi[:, o_sv:].T, w_fg], axis=0).astype(BF16))
        b_fg = jnp.zeros((FG_ROWS, 1), F32).at[:n_rep, 0].set(
            jnp.repeat(b_forget[l].astype(F32), CUM_SPLIT))

        mod3 = _mod(c, w_mod[l], b_mod[l]).reshape(bsz, N_MOD, d)
        tok, aug, feat_t, w_o = _inproj(
            x2, mod3, g_pre_mix[l], proj_weights, b_fg, phase, tri, cos_t, sin_t,
            (w_out[l],), seq=seq, tm=tm_in)
        tok3 = tok.reshape(bsz, seq, TOK_W)
        fox, w_u, w_d = _fox(feat_t, tok3, aug.reshape(bsz, seq, LANES), (w_up[l], w_down[l]))
        swa = _swa(feat_t, tok3, swa_sinks[l])
        x2 = _outproj(fox, swa, x2, mod3,
                      g_post_mix[l], w_o, seq=seq)
        x2 = _mlp(x2, mod3, g_pre_mlp[l], g_post_mlp[l], w_u, w_d, seq=seq)
    return x2.reshape(bsz, seq, d)
```

```python
import functools
import math

import jax
import jax.numpy as jnp
from jax import lax
from jax.experimental import pallas as pl
from jax.experimental.pallas import tpu as pltpu

HEAD_DIM = 128
N_FOX_HEADS = 8
N_SWA_HEADS = 8
N_SWA_KV_HEADS = 2
SWA_GROUP = N_SWA_HEADS // N_SWA_KV_HEADS
SWA_WINDOW = 128
ROPE_THETA = 10000.0
NORM_EPS = 1e-6
N_MOD = 6
MASK_VALUE = -1e30
LOG2E = math.log2(math.e)
Q_SCALE = HEAD_DIM ** -0.5 * LOG2E

FOX_W = N_FOX_HEADS * HEAD_DIM
SWA_Q_W = N_SWA_HEADS * HEAD_DIM
SWA_KV_W = N_SWA_KV_HEADS * HEAD_DIM
FOX_HEADS_PER_STEP = 4
FOX_GROUP_W = FOX_HEADS_PER_STEP * HEAD_DIM
ROW_FQ = 0
ROW_FV = FOX_W
ROW_SQ = 2 * FOX_W
ROW_SV = ROW_SQ + SWA_Q_W
FEAT_W = ROW_SV + SWA_KV_W
CUM_SPLIT = 3
FG_ROWS = 32
LANES = 128

V7X_VMEM_LIMIT = 60 * 1024 * 1024

BF16 = jnp.bfloat16
F32 = jnp.float32
_NT = (((1,), (1,)), ((), ()))
_TN = (((0,), (0,)), ((), ()))


NORM_ROWS = 16


def _inv_rms(x):
    return lax.rsqrt(jnp.mean(x * x, axis=-1, keepdims=True) + NORM_EPS)


def _norm_modulate(x_ref, gain_row, shift_row, out_ref, row0=0, nrows=None):
    nrows = x_ref.shape[0] if nrows is None else nrows
    for c in range(nrows // NORM_ROWS):
        rows = slice(row0 + c * NORM_ROWS, row0 + (c + 1) * NORM_ROWS)
        x = x_ref[rows, :]
        out_ref[rows, :] = ((x * _inv_rms(x)) * gain_row + shift_row).astype(out_ref.dtype)


def _norm_gate_residual(y_ref, x_ref, gate_row, row0=0, nrows=None):
    nrows = x_ref.shape[0] if nrows is None else nrows
    for c in range(nrows // NORM_ROWS):
        rows = slice(row0 + c * NORM_ROWS, row0 + (c + 1) * NORM_ROWS)
        y = y_ref[rows, :]
        y_ref[rows, :] = x_ref[rows, :] + (y * _inv_rms(y)) * gate_row


def _resident(shape, index_map):
    return pl.BlockSpec(shape, index_map, pipeline_mode=pl.Buffered(1))


def _mod_kernel(c_ref, w_ref, b_ref, o_ref):
    c = c_ref[...]
    cond = c * jax.nn.sigmoid(c)
    o_ref[...] = jnp.dot(cond.astype(BF16), w_ref[...].astype(BF16),
                         preferred_element_type=F32) + b_ref[...]


def _mod(c, w, b, *, tn=1024):
    bsz, d = c.shape
    n = w.shape[1]
    return pl.pallas_call(
        _mod_kernel,
        grid=(n // tn,),
        in_specs=[pl.BlockSpec((bsz, d), lambda j: (0, 0)),
                  pl.BlockSpec((d, tn), lambda j: (0, j)),
                  pl.BlockSpec((1, tn), lambda j: (0, j))],
        out_specs=pl.BlockSpec((bsz, tn), lambda j: (0, j)),
        out_shape=jax.ShapeDtypeStruct((bsz, n), F32),
        compiler_params=pltpu.CompilerParams(
            dimension_semantics=("arbitrary",), vmem_limit_bytes=V7X_VMEM_LIMIT),
        name="mod",
    )(c, w, b.reshape(1, n))


def _inproj_kernel(x_ref, mod_ref, g_ref, wfk_ref, wfq_ref, wfv_ref, wsqk_ref, wsvfg_ref,
                   bfg_ref, phase_ref,
                   tri_ref, cost_ref, sint_ref, wo_ref,
                   kf_ref, ks_ref, aug_ref, feat_ref, wo_bf_ref,
                   h_scr, carry_scr, *, tiles_per_seq):
    i = pl.program_id(0)
    hd = HEAD_DIM
    half = hd // 2

    @pl.when(i % tiles_per_seq == 0)
    def _():
        carry_scr[...] = jnp.zeros_like(carry_scr)

    _norm_modulate(x_ref, g_ref[...] * (1.0 + mod_ref[0, 1:2, :]), mod_ref[0, 0:1, :], h_scr)

    half_rows = h_scr.shape[0] // 2
    for a in range(2):
        rows = slice(a * half_rows, (a + 1) * half_rows)
        fk = jnp.dot(h_scr[rows, :], wfk_ref[...], preferred_element_type=F32).astype(BF16)
        for grp in range(FOX_W // FOX_GROUP_W):
            kf_ref[grp, rows, :] = fk[:, grp * FOX_GROUP_W:(grp + 1) * FOX_GROUP_W]

    def feat(w_ref):
        return lax.dot_general(w_ref[...], h_scr[...], _NT, preferred_element_type=F32)

    svfg = feat(wsvfg_ref)
    feat_ref[ROW_SV:, :] = svfg[:SWA_KV_W].astype(BF16)
    z = svfg[SWA_KV_W:] + bfg_ref[...]
    logf = jnp.minimum(z, 0.0) - jnp.log1p(jnp.exp(-jnp.abs(z)))
    p0 = logf.astype(BF16)
    r0 = logf - p0.astype(F32)
    p1 = r0.astype(BF16)
    p2 = (r0 - p1.astype(F32)).astype(BF16)

    feat_ref[ROW_FQ:ROW_FQ + FOX_W, :] = (feat(wfq_ref) * Q_SCALE).astype(BF16)
    wo_bf_ref[...] = wo_ref[...].astype(BF16)

    cs = jnp.dot(jnp.concatenate([p0, p1, p2], axis=0), tri_ref[...],
                 preferred_element_type=F32)

    cum = cs[:FG_ROWS] + cs[FG_ROWS:2 * FG_ROWS] + cs[2 * FG_ROWS:] + carry_scr[...]
    carry_scr[...] = cum[:, cum.shape[1] - 1:]
    negc = cum * (-LOG2E)
    hi = negc.astype(BF16).astype(F32)
    rem = negc - hi
    mid = rem.astype(BF16).astype(F32)
    lo = rem - mid
    ph = phase_ref[...]
    pieces = jnp.where(ph == 0, hi, jnp.where(ph == 1, mid, lo))
    pad = jnp.zeros((LANES - FG_ROWS, pieces.shape[1]), F32)
    aug_ref[...] = jnp.concatenate([pieces, pad], axis=0).T.astype(BF16)

    sqk = feat(wsqk_ref)
    feat_ref[ROW_FV:ROW_FV + FOX_W, :] = feat(wfv_ref).astype(BF16)
    cos_t, sin_t = cost_ref[...], sint_ref[...]
    for hh in range(N_SWA_HEADS + N_SWA_KV_HEADS):
        x1 = sqk[hh * hd:hh * hd + half, :]
        x2 = sqk[hh * hd + half:(hh + 1) * hd, :]
        lo_half = x1 * cos_t - x2 * sin_t
        hi_half = x2 * cos_t + x1 * sin_t
        if hh < N_SWA_HEADS:
            base = ROW_SQ + hh * hd
            feat_ref[base:base + half, :] = (lo_half * Q_SCALE).astype(BF16)
            feat_ref[base + half:base + hd, :] = (hi_half * Q_SCALE).astype(BF16)
        else:
            col = (hh - N_SWA_HEADS) * hd
            ks_ref[:, col:col + hd] = jnp.concatenate(
                [lo_half, hi_half], axis=0).T.astype(BF16)


def _inproj(x2, mod3, g, proj_weights, b_fg, phase, tri, cos_t, sin_t, later_weights,
            *, seq, tm):
    t, d = x2.shape
    steps = t // tm
    tiles_per_seq = seq // tm
    half = HEAD_DIM // 2
    kern = functools.partial(_inproj_kernel, tiles_per_seq=tiles_per_seq)
    const = lambda i: (0, 0)
    pos_t = lambda i: (0, i % tiles_per_seq)
    slab = lambda i: (i, 0)
    slab_specs = [pl.BlockSpec((w.shape[0] // steps, w.shape[1]), slab) for w in later_weights]
    return pl.pallas_call(
        kern,
        grid=(steps,),
        in_specs=[
            pl.BlockSpec((tm, d), lambda i: (i, 0)),
            pl.BlockSpec((1, N_MOD, d), lambda i: (i // tiles_per_seq, 0, 0)),
            _resident((1, d), const),
        ] + [_resident(w.shape, const) for w in proj_weights] + [
            _resident((FG_ROWS, 1), const),
            _resident((FG_ROWS, 1), const),
            _resident((tm, tm), const),
            pl.BlockSpec((half, tm), pos_t),
            pl.BlockSpec((half, tm), pos_t),
        ] + slab_specs,
        out_specs=[
            pl.BlockSpec((FOX_W // FOX_GROUP_W, tm, FOX_GROUP_W), lambda i: (0, i, 0)),
            pl.BlockSpec((tm, SWA_KV_W), lambda i: (i, 0)),
            pl.BlockSpec((tm, LANES), lambda i: (i, 0)),
            pl.BlockSpec((FEAT_W, tm), lambda i: (0, i)),
        ] + slab_specs,
        out_shape=[jax.ShapeDtypeStruct((FOX_W // FOX_GROUP_W, t, FOX_GROUP_W), BF16),
                   jax.ShapeDtypeStruct((t, SWA_KV_W), BF16),
                   jax.ShapeDtypeStruct((t, LANES), BF16),
                   jax.ShapeDtypeStruct((FEAT_W, t), BF16)]
        + [jax.ShapeDtypeStruct(w.shape, BF16) for w in later_weights],
        scratch_shapes=[pltpu.VMEM((tm, d), BF16), pltpu.VMEM((FG_ROWS, 1), F32)],
        compiler_params=pltpu.CompilerParams(
            dimension_semantics=("arbitrary",), vmem_limit_bytes=V7X_VMEM_LIMIT),
        name="in_proj",
    )(x2, mod3, g.reshape(1, d), *proj_weights, b_fg, phase, tri, cos_t, sin_t,
      *later_weights)


def _fox_kernel(qt_ref, k_ref, aug_ref, vt_ref, wu_ref, wd_ref, o_ref, wu_bf_ref, wd_bf_ref,
                *, blk, heads):
    hd = HEAD_DIM
    seq = k_ref.shape[1]
    nblk = seq // blk
    wu_bf_ref[...] = wu_ref[...].astype(BF16)
    wd_bf_ref[...] = wd_ref[...].astype(BF16)
    r = lax.broadcasted_iota(jnp.int32, (LANES, blk), 0)
    kk = lax.broadcasted_iota(jnp.int32, (blk, blk), 0)
    qq = lax.broadcasted_iota(jnp.int32, (blk, blk), 1)
    causal = kk <= qq
    onehot_t = []
    for hh in range(heads):
        hidx = pl.program_id(1) * heads + hh
        sel = (r >= CUM_SPLIT * hidx) & (r < CUM_SPLIT * (hidx + 1))
        onehot_t.append(jnp.where(sel, 1.0, 0.0).astype(BF16))

    def scores(hh, i):
        nk = (i + 1) * blk
        cols = slice(hh * hd, (hh + 1) * hd)
        q_aug = jnp.concatenate([qt_ref[cols, i * blk:nk], onehot_t[hh]], axis=0)
        k_aug = jnp.concatenate([k_ref[0, :nk, cols], aug_ref[0, :nk, :]], axis=1)
        s = jnp.dot(k_aug, q_aug, preferred_element_type=F32)
        s_diag = jnp.where(causal, s[nk - blk:, :], MASK_VALUE)
        s = s_diag if i == 0 else jnp.concatenate([s[:nk - blk, :], s_diag], axis=0)
        return s, jnp.max(s, axis=0, keepdims=True)

    s_next = [scores(hh, 0) for hh in range(heads)]
    for i in range(nblk):
        qs = slice(i * blk, (i + 1) * blk)
        nk = (i + 1) * blk
        s_cur = s_next
        if i + 1 < nblk:
            s_next = [scores(hh, i + 1) for hh in range(heads)]
        for hh in range(heads):
            cols = slice(hh * hd, (hh + 1) * hd)
            s, m = s_cur[hh]
            p = jnp.exp2(s - m)
            l = jnp.sum(p, axis=0, keepdims=True)
            acc = jnp.dot(vt_ref[cols, :nk], p.astype(BF16), preferred_element_type=F32)
            o_ref[cols, qs] = (acc * (1.0 / l)).astype(o_ref.dtype)


def _fox(feat_t, keys, aug3, later_weights, *, blk=256, heads=FOX_HEADS_PER_STEP):
    bsz, seq, _ = aug3.shape
    hw = heads * HEAD_DIM
    hsteps = N_FOX_HEADS // heads
    qb, vb = ROW_FQ // hw, ROW_FV // hw
    kern = functools.partial(_fox_kernel, blk=blk, heads=heads)
    slab = lambda b, h: (b * hsteps + h, 0)
    slab_specs = [pl.BlockSpec((w.shape[0] // (bsz * hsteps), w.shape[1]), slab)
                  for w in later_weights]
    return pl.pallas_call(
        kern,
        grid=(bsz, hsteps),
        in_specs=[
            pl.BlockSpec((hw, seq), lambda b, h: (qb + h, b)),
            pl.BlockSpec((1, seq, hw), lambda b, h: (h, b, 0)),
            pl.BlockSpec((1, seq, LANES), lambda b, h: (b, 0, 0)),
            pl.BlockSpec((hw, seq), lambda b, h: (vb + h, b)),
        ] + slab_specs,
        out_specs=[pl.BlockSpec((hw, seq), lambda b, h: (h, b))] + slab_specs,
        out_shape=[jax.ShapeDtypeStruct((FOX_W, bsz * seq), BF16)]
        + [jax.ShapeDtypeStruct(w.shape, BF16) for w in later_weights],
        compiler_params=pltpu.CompilerParams(
            dimension_semantics=("arbitrary", "arbitrary"),
            vmem_limit_bytes=V7X_VMEM_LIMIT),
        name="fox",
    )(feat_t, keys, aug3, feat_t, *later_weights)


def _swa_kernel(qt_ref, k_ref, vt_ref, sink_ref, o_ref):
    w, hd = SWA_WINDOW, HEAD_DIM
    seq = k_ref.shape[1]
    gl = SWA_GROUP * w
    gw = SWA_GROUP * hd
    groups = range(N_SWA_KV_HEADS)
    sinks = [jnp.concatenate(
        [jnp.full((1, w), sink_ref[g * SWA_GROUP + r] * LOG2E, F32) for r in range(SWA_GROUP)],
        axis=1) for g in groups]
    kk = lax.broadcasted_iota(jnp.int32, (w, gl), 0)
    qq = lax.broadcasted_iota(jnp.int32, (w, gl), 1) & (w - 1)
    from_prev = kk > qq

    def scores(g, n):
        q_t = jnp.concatenate(
            [qt_ref[g * gw + r * hd:g * gw + (r + 1) * hd, n * w:(n + 1) * w]
             for r in range(SWA_GROUP)], axis=1)
        ks = slice(max(n - 1, 0) * w, (n + 1) * w)
        s = jnp.dot(k_ref[0, ks, g * hd:(g + 1) * hd], q_t,
                    preferred_element_type=F32)
        packed = jnp.where(from_prev, s[:w, :] if n > 0 else MASK_VALUE, s[s.shape[0] - w:, :])
        return packed, jnp.maximum(jnp.max(packed, axis=0, keepdims=True), sinks[g])

    nblk = seq // w
    s_next = [scores(g, 0) for g in groups]
    for n in range(nblk):
        ks = slice(max(n - 1, 0) * w, (n + 1) * w)
        s_cur = s_next
        if n + 1 < nblk:
            s_next = [scores(g, n + 1) for g in groups]
        for g in groups:
            s, m = s_cur[g]
            p = jnp.exp2(s - m)
            denom = jnp.sum(p, axis=0, keepdims=True) + jnp.exp2(sinks[g] - m)
            p_own = jnp.where(from_prev, 0.0, p).astype(BF16)
            if n > 0:
                p_own = jnp.concatenate(
                    [jnp.where(from_prev, p, 0.0).astype(BF16), p_own], axis=0)
            o_t = jnp.dot(vt_ref[g * hd:(g + 1) * hd, ks], p_own, preferred_element_type=F32)
            o_t = o_t * (1.0 / denom)
            for r in range(SWA_GROUP):
                col = g * gw + r * hd
                o_ref[col:col + hd, n * w:(n + 1) * w] = (
                    o_t[:, r * w:(r + 1) * w].astype(o_ref.dtype))


def _swa(feat_t, keys3, sinks):
    bsz, seq, _ = keys3.shape
    qb, vb = ROW_SQ // SWA_Q_W, ROW_SV // SWA_KV_W
    return pl.pallas_call(
        _swa_kernel,
        grid=(bsz,),
        in_specs=[
            pl.BlockSpec((SWA_Q_W, seq), lambda b: (qb, b)),
            pl.BlockSpec((1, seq, SWA_KV_W), lambda b: (b, 0, 0)),
            pl.BlockSpec((SWA_KV_W, seq), lambda b: (vb, b)),
            pl.BlockSpec(memory_space=pltpu.SMEM),
        ],
        out_specs=pl.BlockSpec((SWA_Q_W, seq), lambda b: (0, b)),
        out_shape=jax.ShapeDtypeStruct((SWA_Q_W, bsz * seq), BF16),
        compiler_params=pltpu.CompilerParams(
            dimension_semantics=("arbitrary",), vmem_limit_bytes=V7X_VMEM_LIMIT),
        name="swa",
    )(feat_t, keys3, feat_t, sinks)


OUTPROJ_ROW_PARTS = 2


def _outproj_kernel(fox_ref, swa_ref, x_ref, mod_ref, g_ref, wf_ref, ws_ref, o_ref):
    part = x_ref.shape[0] // OUTPROJ_ROW_PARTS
    gate = mod_ref[0, 2:3, :] * g_ref[...]
    for a in range(OUTPROJ_ROW_PARTS):
        rows = slice(a * part, (a + 1) * part)
        o_ref[rows, :] = (
            lax.dot_general(fox_ref[:, rows], wf_ref[...], _TN, preferred_element_type=F32)
            + lax.dot_general(swa_ref[:, rows], ws_ref[...], _TN, preferred_element_type=F32))
    for a in range(OUTPROJ_ROW_PARTS):
        _norm_gate_residual(o_ref, x_ref, gate, a * part, part)


def _outproj(fox2, swa2, x2, mod3, g, w_out_bf, *, seq, tm=1024):
    t, d = x2.shape
    tiles_per_seq = seq // tm
    return pl.pallas_call(
        _outproj_kernel,
        grid=(t // tm,),
        in_specs=[
            pl.BlockSpec((FOX_W, tm), lambda i: (0, i)),
            pl.BlockSpec((SWA_Q_W, tm), lambda i: (0, i)),
            pl.BlockSpec((tm, d), lambda i: (i, 0)),
            pl.BlockSpec((1, N_MOD, d), lambda i: (i // tiles_per_seq, 0, 0)),
            _resident((1, d), lambda i: (0, 0)),
            _resident((FOX_W, d), lambda i: (0, 0)),
            _resident((SWA_Q_W, d), lambda i: (FOX_W // SWA_Q_W, 0)),
        ],
        out_specs=pl.BlockSpec((tm, d), lambda i: (i, 0)),
        out_shape=jax.ShapeDtypeStruct((t, d), F32),
        compiler_params=pltpu.CompilerParams(
            dimension_semantics=("arbitrary",), vmem_limit_bytes=V7X_VMEM_LIMIT),
        name="out_proj",
    )(fox2, swa2, x2, mod3, g.reshape(1, d), w_out_bf, w_out_bf)


MLP_ROW_PARTS = 2


def _mlp_kernel(x_ref, mod_ref, gpre_ref, gpost_ref, wu_ref, wd_ref, o_ref, h_scr):
    j = pl.program_id(1)
    last = pl.num_programs(1) - 1
    part = x_ref.shape[0] // MLP_ROW_PARTS

    def step(first, final):
        if first:
            gain = gpre_ref[...] * (1.0 + mod_ref[0, 4:5, :])
            shift = mod_ref[0, 3:4, :]
        if final:
            gate = mod_ref[0, 5:6, :] * gpost_ref[...]
        ups = []
        for a in range(MLP_ROW_PARTS):
            rows = slice(a * part, (a + 1) * part)
            if first:
                _norm_modulate(x_ref, gain, shift, h_scr, a * part, part)
            ups.append(jnp.dot(h_scr[rows, :], wu_ref[...], preferred_element_type=F32))
        for a in range(MLP_ROW_PARTS):
            rows = slice(a * part, (a + 1) * part)
            act = jnp.square(jnp.maximum(ups[a], 0.0)).astype(BF16)
            y = jnp.dot(act, wd_ref[...], preferred_element_type=F32)
            if first:
                o_ref[rows, :] = y
            else:
                o_ref[rows, :] += y
            if final:
                _norm_gate_residual(o_ref, x_ref, gate, a * part, part)

    pl.when(j == 0)(lambda: step(True, False))
    pl.when((j > 0) & (j < last))(lambda: step(False, False))
    pl.when(j == last)(lambda: step(False, True))


def _mlp(x2, mod3, g_pre, g_post, w_up, w_down, *, seq, tm=1024, tf=1024):
    t, d = x2.shape
    ff = w_up.shape[1]
    tiles_per_seq = seq // tm
    return pl.pallas_call(
        _mlp_kernel,
        grid=(t // tm, ff // tf),
        in_specs=[
            pl.BlockSpec((tm, d), lambda i, j: (i, 0)),
            pl.BlockSpec((1, N_MOD, d), lambda i, j: (i // tiles_per_seq, 0, 0)),
            pl.BlockSpec((1, d), lambda i, j: (0, 0)),
            pl.BlockSpec((1, d), lambda i, j: (0, 0)),
            pl.BlockSpec((d, tf), lambda i, j: (0, j)),
            pl.BlockSpec((tf, d), lambda i, j: (j, 0)),
        ],
        out_specs=pl.BlockSpec((tm, d), lambda i, j: (i, 0)),
        out_shape=jax.ShapeDtypeStruct((t, d), F32),
        scratch_shapes=[pltpu.VMEM((tm, d), BF16)],
        compiler_params=pltpu.CompilerParams(
            dimension_semantics=("arbitrary", "arbitrary"),
            vmem_limit_bytes=V7X_VMEM_LIMIT),
        name="mlp",
    )(x2, mod3, g_pre.reshape(1, d), g_post.reshape(1, d), w_up, w_down)


def _rope_tables(seq):
    half = HEAD_DIM // 2
    inv_freq = 1.0 / (ROPE_THETA ** (jnp.arange(half, dtype=F32) * (2.0 / HEAD_DIM)))
    ang = jnp.arange(seq).astype(F32)[:, None] * inv_freq[None, :]
    return jnp.cos(ang).T, jnp.sin(ang).T


def kernel(x, c, w_mod, b_mod, g_pre_mix, g_post_mix, w_in, b_forget, swa_sinks,
           w_out, g_pre_mlp, g_post_mlp, w_up, w_down):
    bsz, seq, d = x.shape
    depth = w_mod.shape[0]
    t = bsz * seq
    tm_in = 512
    cos_t, sin_t = _rope_tables(seq)
    phase = (jnp.arange(FG_ROWS, dtype=jnp.int32) % CUM_SPLIT).reshape(FG_ROWS, 1)
    tri = jnp.triu(jnp.ones((tm_in, tm_in), BF16))
    o_fk, o_fv, o_fg = FOX_W, 2 * FOX_W, 3 * FOX_W
    o_sq = o_fg + N_FOX_HEADS
    o_sk = o_sq + SWA_Q_W
    o_sv = o_sk + SWA_KV_W
    n_rep = CUM_SPLIT * N_FOX_HEADS

    x2 = x.reshape(t, d)
    for l in range(depth):
        wi = w_in[l]
        w_fg = jnp.zeros((FG_ROWS, d), F32).at[:n_rep].set(
            jnp.repeat(wi[:, o_fg:o_sq].T, CUM_SPLIT, axis=0))
        proj_weights = (
            wi[:, o_fk:o_fv].astype(BF16),
            wi[:, :o_fk].T.astype(BF16), wi[:, o_fv:o_fg].T.astype(BF16),
            wi[:, o_sq:o_sv].T.astype(BF16),
            jnp.concatenate([wi[:, o_sv:].T, w_fg], axis=0).astype(BF16))
        b_fg = jnp.zeros((FG_ROWS, 1), F32).at[:n_rep, 0].set(
            jnp.repeat(b_forget[l].astype(F32), CUM_SPLIT))

        mod3 = _mod(c, w_mod[l], b_mod[l]).reshape(bsz, N_MOD, d)
        kf, ks, aug, feat_t, w_o = _inproj(
            x2, mod3, g_pre_mix[l], proj_weights, b_fg, phase, tri, cos_t, sin_t,
            (w_out[l],), seq=seq, tm=tm_in)
        fox, w_u, w_d = _fox(feat_t, kf, aug.reshape(bsz, seq, LANES), (w_up[l], w_down[l]))
        swa = _swa(feat_t, ks.reshape(bsz, seq, SWA_KV_W), swa_sinks[l])
        x2 = _outproj(fox, swa, x2, mod3,
                      g_post_mix[l], w_o, seq=seq)
        x2 = _mlp(x2, mod3, g_pre_mlp[l], g_post_mlp[l], w_u, w_d, seq=seq)
    return x2.reshape(bsz, seq, d)
```

```python
import functools
import math

import jax
import jax.numpy as jnp
from jax import lax
from jax.experimental import pallas as pl
from jax.experimental.pallas import tpu as pltpu

HEAD_DIM = 128
N_FOX_HEADS = 8
N_SWA_HEADS = 8
N_SWA_KV_HEADS = 2
SWA_GROUP = N_SWA_HEADS // N_SWA_KV_HEADS
SWA_WINDOW = 128
ROPE_THETA = 10000.0
NORM_EPS = 1e-6
N_MOD = 6
MASK_VALUE = -1e30
LOG2E = math.log2(math.e)
Q_SCALE = HEAD_DIM ** -0.5 * LOG2E

FOX_W = N_FOX_HEADS * HEAD_DIM
SWA_Q_W = N_SWA_HEADS * HEAD_DIM
SWA_KV_W = N_SWA_KV_HEADS * HEAD_DIM
TOK_W = FOX_W + SWA_KV_W
ROW_FQ = 0
ROW_FV = FOX_W
ROW_SQ = 2 * FOX_W
ROW_SV = ROW_SQ + SWA_Q_W
FEAT_W = ROW_SV + SWA_KV_W
CUM_SPLIT = 3
FG_ROWS = 32
LANES = 128

V7X_VMEM_LIMIT = 60 * 1024 * 1024

BF16 = jnp.bfloat16
F32 = jnp.float32
_NT = (((1,), (1,)), ((), ()))
_TN = (((0,), (0,)), ((), ()))


NORM_ROWS = 16


def _inv_rms(x):
    return lax.rsqrt(jnp.mean(x * x, axis=-1, keepdims=True) + NORM_EPS)


def _norm_modulate(x_ref, gain_row, shift_row, out_ref, row0=0, nrows=None):
    nrows = x_ref.shape[0] if nrows is None else nrows
    for c in range(nrows // NORM_ROWS):
        rows = slice(row0 + c * NORM_ROWS, row0 + (c + 1) * NORM_ROWS)
        x = x_ref[rows, :]
        out_ref[rows, :] = ((x * _inv_rms(x)) * gain_row + shift_row).astype(out_ref.dtype)


def _norm_gate_residual(y_ref, x_ref, gate_row, row0=0, nrows=None):
    nrows = x_ref.shape[0] if nrows is None else nrows
    for c in range(nrows // NORM_ROWS):
        rows = slice(row0 + c * NORM_ROWS, row0 + (c + 1) * NORM_ROWS)
        y = y_ref[rows, :]
        y_ref[rows, :] = x_ref[rows, :] + (y * _inv_rms(y)) * gate_row


def _resident(shape, index_map):
    return pl.BlockSpec(shape, index_map, pipeline_mode=pl.Buffered(1))


def _mod_kernel(c_ref, w_ref, b_ref, o_ref):
    k = pl.program_id(0)
    c = c_ref[...]
    cond = c * jax.nn.sigmoid(c)
    part = jnp.dot(cond.astype(BF16), w_ref[...].astype(BF16), preferred_element_type=F32)

    @pl.when(k == 0)
    def _():
        o_ref[...] = part + b_ref[...]

    @pl.when(k > 0)
    def _():
        o_ref[...] += part


def _mod(c, w, b, *, tk=256):
    bsz, d = c.shape
    n = w.shape[1]
    return pl.pallas_call(
        _mod_kernel,
        grid=(d // tk,),
        in_specs=[pl.BlockSpec((bsz, tk), lambda k: (0, k)),
                  pl.BlockSpec((tk, n), lambda k: (k, 0)),
                  pl.BlockSpec((1, n), lambda k: (0, 0))],
        out_specs=pl.BlockSpec((bsz, n), lambda k: (0, 0)),
        out_shape=jax.ShapeDtypeStruct((bsz, n), F32),
        compiler_params=pltpu.CompilerParams(
            dimension_semantics=("arbitrary",), vmem_limit_bytes=V7X_VMEM_LIMIT),
        name="mod",
    )(c, w, b.reshape(1, n))


def _inproj_kernel(x_ref, mod_ref, g_ref, wfk_ref, wfq_ref, wfv_ref, wsqk_ref, wsvfg_ref,
                   bfg_ref, phase_ref,
                   tri_ref, cost_ref, sint_ref, wo_ref,
                   tok_ref, aug_ref, feat_ref, wo_bf_ref,
                   h_scr, carry_scr, *, tiles_per_seq):
    i = pl.program_id(0)
    hd = HEAD_DIM
    half = hd // 2

    @pl.when(i % tiles_per_seq == 0)
    def _():
        carry_scr[...] = jnp.zeros_like(carry_scr)

    _norm_modulate(x_ref, g_ref[...] * (1.0 + mod_ref[0, 1:2, :]), mod_ref[0, 0:1, :], h_scr)

    half_rows = h_scr.shape[0] // 2
    for a in range(2):
        rows = slice(a * half_rows, (a + 1) * half_rows)
        tok_ref[rows, :FOX_W] = jnp.dot(h_scr[rows, :], wfk_ref[...],
                                        preferred_element_type=F32).astype(BF16)

    def feat(w_ref):
        return lax.dot_general(w_ref[...], h_scr[...], _NT, preferred_element_type=F32)

    svfg = feat(wsvfg_ref)
    feat_ref[ROW_SV:, :] = svfg[:SWA_KV_W].astype(BF16)
    z = svfg[SWA_KV_W:] + bfg_ref[...]
    logf = jnp.minimum(z, 0.0) - jnp.log1p(jnp.exp(-jnp.abs(z)))
    p0 = logf.astype(BF16)
    r0 = logf - p0.astype(F32)
    p1 = r0.astype(BF16)
    p2 = (r0 - p1.astype(F32)).astype(BF16)

    feat_ref[ROW_FQ:ROW_FQ + FOX_W, :] = (feat(wfq_ref) * Q_SCALE).astype(BF16)
    wo_bf_ref[...] = wo_ref[...].astype(BF16)

    cs = jnp.dot(jnp.concatenate([p0, p1, p2], axis=0), tri_ref[...],
                 preferred_element_type=F32)

    cum = cs[:FG_ROWS] + cs[FG_ROWS:2 * FG_ROWS] + cs[2 * FG_ROWS:] + carry_scr[...]
    carry_scr[...] = cum[:, cum.shape[1] - 1:]
    negc = cum * (-LOG2E)
    hi = negc.astype(BF16).astype(F32)
    rem = negc - hi
    mid = rem.astype(BF16).astype(F32)
    lo = rem - mid
    ph = phase_ref[...]
    pieces = jnp.where(ph == 0, hi, jnp.where(ph == 1, mid, lo))
    pad = jnp.zeros((LANES - FG_ROWS, pieces.shape[1]), F32)
    aug_ref[...] = jnp.concatenate([pieces, pad], axis=0).T.astype(BF16)

    sqk = feat(wsqk_ref)
    feat_ref[ROW_FV:ROW_FV + FOX_W, :] = feat(wfv_ref).astype(BF16)
    cos_t, sin_t = cost_ref[...], sint_ref[...]
    for hh in range(N_SWA_HEADS + N_SWA_KV_HEADS):
        x1 = sqk[hh * hd:hh * hd + half, :]
        x2 = sqk[hh * hd + half:(hh + 1) * hd, :]
        lo_half = x1 * cos_t - x2 * sin_t
        hi_half = x2 * cos_t + x1 * sin_t
        if hh < N_SWA_HEADS:
            base = ROW_SQ + hh * hd
            feat_ref[base:base + half, :] = (lo_half * Q_SCALE).astype(BF16)
            feat_ref[base + half:base + hd, :] = (hi_half * Q_SCALE).astype(BF16)
        else:
            col = FOX_W + (hh - N_SWA_HEADS) * hd
            tok_ref[:, col:col + hd] = jnp.concatenate(
                [lo_half, hi_half], axis=0).T.astype(BF16)


def _inproj(x2, mod3, g, proj_weights, b_fg, phase, tri, cos_t, sin_t, later_weights,
            *, seq, tm):
    t, d = x2.shape
    steps = t // tm
    tiles_per_seq = seq // tm
    half = HEAD_DIM // 2
    kern = functools.partial(_inproj_kernel, tiles_per_seq=tiles_per_seq)
    const = lambda i: (0, 0)
    pos_t = lambda i: (0, i % tiles_per_seq)
    slab = lambda i: (i, 0)
    slab_specs = [pl.BlockSpec((w.shape[0] // steps, w.shape[1]), slab) for w in later_weights]
    return pl.pallas_call(
        kern,
        grid=(steps,),
        in_specs=[
            pl.BlockSpec((tm, d), lambda i: (i, 0)),
            pl.BlockSpec((1, N_MOD, d), lambda i: (i // tiles_per_seq, 0, 0)),
            _resident((1, d), const),
        ] + [_resident(w.shape, const) for w in proj_weights] + [
            _resident((FG_ROWS, 1), const),
            _resident((FG_ROWS, 1), const),
            _resident((tm, tm), const),
            pl.BlockSpec((half, tm), pos_t),
            pl.BlockSpec((half, tm), pos_t),
        ] + slab_specs,
        out_specs=[
            pl.BlockSpec((tm, TOK_W), lambda i: (i, 0)),
            pl.BlockSpec((tm, LANES), lambda i: (i, 0)),
            pl.BlockSpec((FEAT_W, tm), lambda i: (0, i)),
        ] + slab_specs,
        out_shape=[jax.ShapeDtypeStruct((t, TOK_W), BF16),
                   jax.ShapeDtypeStruct((t, LANES), BF16),
                   jax.ShapeDtypeStruct((FEAT_W, t), BF16)]
        + [jax.ShapeDtypeStruct(w.shape, BF16) for w in later_weights],
        scratch_shapes=[pltpu.VMEM((tm, d), BF16), pltpu.VMEM((FG_ROWS, 1), F32)],
        compiler_params=pltpu.CompilerParams(
            dimension_semantics=("arbitrary",), vmem_limit_bytes=V7X_VMEM_LIMIT),
        name="in_proj",
    )(x2, mod3, g.reshape(1, d), *proj_weights, b_fg, phase, tri, cos_t, sin_t,
      *later_weights)


def _fox_kernel(qt_ref, k_ref, aug_ref, vt_ref, wu_ref, wd_ref, o_ref, wu_bf_ref, wd_bf_ref,
                *, blk, heads):
    hd = HEAD_DIM
    seq = k_ref.shape[1]
    nblk = seq // blk
    wu_bf_ref[...] = wu_ref[...].astype(BF16)
    wd_bf_ref[...] = wd_ref[...].astype(BF16)
    r = lax.broadcasted_iota(jnp.int32, (LANES, blk), 0)
    kk = lax.broadcasted_iota(jnp.int32, (blk, blk), 0)
    qq = lax.broadcasted_iota(jnp.int32, (blk, blk), 1)
    causal = kk <= qq
    onehot_t = []
    for hh in range(heads):
        hidx = pl.program_id(1) * heads + hh
        sel = (r >= CUM_SPLIT * hidx) & (r < CUM_SPLIT * (hidx + 1))
        onehot_t.append(jnp.where(sel, 1.0, 0.0).astype(BF16))

    def scores(hh, i):
        nk = (i + 1) * blk
        cols = slice(hh * hd, (hh + 1) * hd)
        q_aug = jnp.concatenate([qt_ref[cols, i * blk:nk], onehot_t[hh]], axis=0)
        k_aug = jnp.concatenate([k_ref[0, :nk, cols], aug_ref[0, :nk, :]], axis=1)
        s = jnp.dot(k_aug, q_aug, preferred_element_type=F32)
        s_diag = jnp.where(causal, s[nk - blk:, :], MASK_VALUE)
        s = s_diag if i == 0 else jnp.concatenate([s[:nk - blk, :], s_diag], axis=0)
        return s, jnp.max(s, axis=0, keepdims=True)

    s_next = [scores(hh, 0) for hh in range(heads)]
    for i in range(nblk):
        qs = slice(i * blk, (i + 1) * blk)
        nk = (i + 1) * blk
        s_cur = s_next
        if i + 1 < nblk:
            s_next = [scores(hh, i + 1) for hh in range(heads)]
        for hh in range(heads):
            cols = slice(hh * hd, (hh + 1) * hd)
            s, m = s_cur[hh]
            p = jnp.exp2(s - m)
            l = jnp.sum(p, axis=0, keepdims=True)
            acc = jnp.dot(vt_ref[cols, :nk], p.astype(BF16), preferred_element_type=F32)
            o_ref[cols, qs] = (acc * (1.0 / l)).astype(o_ref.dtype)


def _fox(feat_t, tok3, aug3, later_weights, *, blk=256, heads=4):
    bsz, seq, _ = tok3.shape
    hw = heads * HEAD_DIM
    hsteps = N_FOX_HEADS // heads
    qb, vb = ROW_FQ // hw, ROW_FV // hw
    kern = functools.partial(_fox_kernel, blk=blk, heads=heads)
    slab = lambda b, h: (b * hsteps + h, 0)
    slab_specs = [pl.BlockSpec((w.shape[0] // (bsz * hsteps), w.shape[1]), slab)
                  for w in later_weights]
    return pl.pallas_call(
        kern,
        grid=(bsz, hsteps),
        in_specs=[
            pl.BlockSpec((hw, seq), lambda b, h: (qb + h, b)),
            pl.BlockSpec((1, seq, hw), lambda b, h: (b, 0, h)),
            pl.BlockSpec((1, seq, LANES), lambda b, h: (b, 0, 0)),
            pl.BlockSpec((hw, seq), lambda b, h: (vb + h, b)),
        ] + slab_specs,
        out_specs=[pl.BlockSpec((hw, seq), lambda b, h: (h, b))] + slab_specs,
        out_shape=[jax.ShapeDtypeStruct((FOX_W, bsz * seq), BF16)]
        + [jax.ShapeDtypeStruct(w.shape, BF16) for w in later_weights],
        compiler_params=pltpu.CompilerParams(
            dimension_semantics=("arbitrary", "arbitrary"),
            vmem_limit_bytes=V7X_VMEM_LIMIT),
        name="fox",
    )(feat_t, tok3, aug3, feat_t, *later_weights)


def _swa_kernel(qt_ref, k_ref, vt_ref, sink_ref, o_ref):
    w, hd = SWA_WINDOW, HEAD_DIM
    seq = k_ref.shape[1]
    gl = SWA_GROUP * w
    gw = SWA_GROUP * hd
    groups = range(N_SWA_KV_HEADS)
    sinks = [jnp.concatenate(
        [jnp.full((1, w), sink_ref[g * SWA_GROUP + r] * LOG2E, F32) for r in range(SWA_GROUP)],
        axis=1) for g in groups]
    kk = lax.broadcasted_iota(jnp.int32, (w, gl), 0)
    qq = lax.broadcasted_iota(jnp.int32, (w, gl), 1) & (w - 1)
    from_prev = kk > qq

    def scores(g, n):
        q_t = jnp.concatenate(
            [qt_ref[g * gw + r * hd:g * gw + (r + 1) * hd, n * w:(n + 1) * w]
             for r in range(SWA_GROUP)], axis=1)
        ks = slice(max(n - 1, 0) * w, (n + 1) * w)
        s = jnp.dot(k_ref[0, ks, g * hd:(g + 1) * hd], q_t,
                    preferred_element_type=F32)
        packed = jnp.where(from_prev, s[:w, :] if n > 0 else MASK_VALUE, s[s.shape[0] - w:, :])
        return packed, jnp.maximum(jnp.max(packed, axis=0, keepdims=True), sinks[g])

    nblk = seq // w
    s_next = [scores(g, 0) for g in groups]
    for n in range(nblk):
        ks = slice(max(n - 1, 0) * w, (n + 1) * w)
        s_cur = s_next
        if n + 1 < nblk:
            s_next = [scores(g, n + 1) for g in groups]
        for g in groups:
            s, m = s_cur[g]
            p = jnp.exp2(s - m)
            denom = jnp.sum(p, axis=0, keepdims=True) + jnp.exp2(sinks[g] - m)
            p_own = jnp.where(from_prev, 0.0, p).astype(BF16)
            if n > 0:
                p_own = jnp.concatenate(
                    [jnp.where(from_prev, p, 0.0).astype(BF16), p_own], axis=0)
            o_t = jnp.dot(vt_ref[g * hd:(g + 1) * hd, ks], p_own, preferred_element_type=F32)
            o_t = o_t * (1.0 / denom)
            for r in range(SWA_GROUP):
                col = g * gw + r * hd
                o_ref[col:col + hd, n * w:(n + 1) * w] = (
                    o_t[:, r * w:(r + 1) * w].astype(o_ref.dtype))


def _swa(feat_t, tok3, sinks):
    bsz, seq, _ = tok3.shape
    qb, kb, vb = ROW_SQ // SWA_Q_W, FOX_W // SWA_KV_W, ROW_SV // SWA_KV_W
    return pl.pallas_call(
        _swa_kernel,
        grid=(bsz,),
        in_specs=[
            pl.BlockSpec((SWA_Q_W, seq), lambda b: (qb, b)),
            pl.BlockSpec((1, seq, SWA_KV_W), lambda b: (b, 0, kb)),
            pl.BlockSpec((SWA_KV_W, seq), lambda b: (vb, b)),
            pl.BlockSpec(memory_space=pltpu.SMEM),
        ],
        out_specs=pl.BlockSpec((SWA_Q_W, seq), lambda b: (0, b)),
        out_shape=jax.ShapeDtypeStruct((SWA_Q_W, bsz * seq), BF16),
        compiler_params=pltpu.CompilerParams(
            dimension_semantics=("arbitrary",), vmem_limit_bytes=V7X_VMEM_LIMIT),
        name="swa",
    )(feat_t, tok3, feat_t, sinks)


OUTPROJ_ROW_PARTS = 2


def _outproj_kernel(fox_ref, swa_ref, x_ref, mod_ref, g_ref, wf_ref, ws_ref, o_ref):
    part = x_ref.shape[0] // OUTPROJ_ROW_PARTS
    gate = mod_ref[0, 2:3, :] * g_ref[...]
    for a in range(OUTPROJ_ROW_PARTS):
        rows = slice(a * part, (a + 1) * part)
        o_ref[rows, :] = (
            lax.dot_general(fox_ref[:, rows], wf_ref[...], _TN, preferred_element_type=F32)
            + lax.dot_general(swa_ref[:, rows], ws_ref[...], _TN, preferred_element_type=F32))
    for a in range(OUTPROJ_ROW_PARTS):
        _norm_gate_residual(o_ref, x_ref, gate, a * part, part)


def _outproj(fox2, swa2, x2, mod3, g, w_out_bf, *, seq, tm=1024):
    t, d = x2.shape
    tiles_per_seq = seq // tm
    return pl.pallas_call(
        _outproj_kernel,
        grid=(t // tm,),
        in_specs=[
            pl.BlockSpec((FOX_W, tm), lambda i: (0, i)),
            pl.BlockSpec((SWA_Q_W, tm), lambda i: (0, i)),
            pl.BlockSpec((tm, d), lambda i: (i, 0)),
            pl.BlockSpec((1, N_MOD, d), lambda i: (i // tiles_per_seq, 0, 0)),
            _resident((1, d), lambda i: (0, 0)),
            _resident((FOX_W, d), lambda i: (0, 0)),
            _resident((SWA_Q_W, d), lambda i: (FOX_W // SWA_Q_W, 0)),
        ],
        out_specs=pl.BlockSpec((tm, d), lambda i: (i, 0)),
        out_shape=jax.ShapeDtypeStruct((t, d), F32),
        compiler_params=pltpu.CompilerParams(
            dimension_semantics=("arbitrary",), vmem_limit_bytes=V7X_VMEM_LIMIT),
        name="out_proj",
    )(fox2, swa2, x2, mod3, g.reshape(1, d), w_out_bf, w_out_bf)


MLP_ROW_PARTS = 2


def _mlp_kernel(x_ref, mod_ref, gpre_ref, gpost_ref, wu_ref, wd_ref, o_ref, h_scr):
    j = pl.program_id(1)
    last = pl.num_programs(1) - 1
    part = x_ref.shape[0] // MLP_ROW_PARTS

    def step(first, final):
        if first:
            gain = gpre_ref[...] * (1.0 + mod_ref[0, 4:5, :])
            shift = mod_ref[0, 3:4, :]
        if final:
            gate = mod_ref[0, 5:6, :] * gpost_ref[...]
        ups = []
        for a in range(MLP_ROW_PARTS):
            rows = slice(a * part, (a + 1) * part)
            if first:
                _norm_modulate(x_ref, gain, shift, h_scr, a * part, part)
            ups.append(jnp.dot(h_scr[rows, :], wu_ref[...], preferred_element_type=F32))
        for a in range(MLP_ROW_PARTS):
            rows = slice(a * part, (a + 1) * part)
            act = jnp.square(jnp.maximum(ups[a], 0.0)).astype(BF16)
            y = jnp.dot(act, wd_ref[...], preferred_element_type=F32)
            if first:
                o_ref[rows, :] = y
            else:
                o_ref[rows, :] += y
            if final:
                _norm_gate_residual(o_ref, x_ref, gate, a * part, part)

    pl.when(j == 0)(lambda: step(True, False))
    pl.when((j > 0) & (j < last))(lambda: step(False, False))
    pl.when(j == last)(lambda: step(False, True))


def _mlp(x2, mod3, g_pre, g_post, w_up, w_down, *, seq, tm=1024, tf=1024):
    t, d = x2.shape
    ff = w_up.shape[1]
    tiles_per_seq = seq // tm
    return pl.pallas_call(
        _mlp_kernel,
        grid=(t // tm, ff // tf),
        in_specs=[
            pl.BlockSpec((tm, d), lambda i, j: (i, 0)),
            pl.BlockSpec((1, N_MOD, d), lambda i, j: (i // tiles_per_seq, 0, 0)),
            pl.BlockSpec((1, d), lambda i, j: (0, 0)),
            pl.BlockSpec((1, d), lambda i, j: (0, 0)),
            pl.BlockSpec((d, tf), lambda i, j: (0, j)),
            pl.BlockSpec((tf, d), lambda i, j: (j, 0)),
        ],
        out_specs=pl.BlockSpec((tm, d), lambda i, j: (i, 0)),
        out_shape=jax.ShapeDtypeStruct((t, d), F32),
        scratch_shapes=[pltpu.VMEM((tm, d), BF16)],
        compiler_params=pltpu.CompilerParams(
            dimension_semantics=("arbitrary", "arbitrary"),
            vmem_limit_bytes=V7X_VMEM_LIMIT),
        name="mlp",
    )(x2, mod3, g_pre.reshape(1, d), g_post.reshape(1, d), w_up, w_down)


def _rope_tables(seq):
    half = HEAD_DIM // 2
    inv_freq = 1.0 / (ROPE_THETA ** (jnp.arange(half, dtype=F32) * (2.0 / HEAD_DIM)))
    ang = jnp.arange(seq).astype(F32)[:, None] * inv_freq[None, :]
    return jnp.cos(ang).T, jnp.sin(ang).T


def kernel(x, c, w_mod, b_mod, g_pre_mix, g_post_mix, w_in, b_forget, swa_sinks,
           w_out, g_pre_mlp, g_post_mlp, w_up, w_down):
    bsz, seq, d = x.shape
    depth = w_mod.shape[0]
    t = bsz * seq
    tm_in = 512
    cos_t, sin_t = _rope_tables(seq)
    phase = (jnp.arange(FG_ROWS, dtype=jnp.int32) % CUM_SPLIT).reshape(FG_ROWS, 1)
    tri = jnp.triu(jnp.ones((tm_in, tm_in), BF16))
    o_fk, o_fv, o_fg = FOX_W, 2 * FOX_W, 3 * FOX_W
    o_sq = o_fg + N_FOX_HEADS
    o_sk = o_sq + SWA_Q_W
    o_sv = o_sk + SWA_KV_W
    n_rep = CUM_SPLIT * N_FOX_HEADS

    x2 = x.reshape(t, d)
    for l in range(depth):
        wi = w_in[l]
        w_fg = jnp.zeros((FG_ROWS, d), F32).at[:n_rep].set(
            jnp.repeat(wi[:, o_fg:o_sq].T, CUM_SPLIT, axis=0))
        proj_weights = (
            wi[:, o_fk:o_fv].astype(BF16),
            wi[:, :o_fk].T.astype(BF16), wi[:, o_fv:o_fg].T.astype(BF16),
            wi[:, o_sq:o_sv].T.astype(BF16),
            jnp.concatenate([wi[:, o_sv:].T, w_fg], axis=0).astype(BF16))
        b_fg = jnp.zeros((FG_ROWS, 1), F32).at[:n_rep, 0].set(
            jnp.repeat(b_forget[l].astype(F32), CUM_SPLIT))

        mod3 = _mod(c, w_mod[l], b_mod[l]).reshape(bsz, N_MOD, d)
        tok, aug, feat_t, w_o = _inproj(
            x2, mod3, g_pre_mix[l], proj_weights, b_fg, phase, tri, cos_t, sin_t,
            (w_out[l],), seq=seq, tm=tm_in)
        tok3 = tok.reshape(bsz, seq, TOK_W)
        fox, w_u, w_d = _fox(feat_t, tok3, aug.reshape(bsz, seq, LANES), (w_up[l], w_down[l]))
        swa = _swa(feat_t, tok3, swa_sinks[l])
        x2 = _outproj(fox, swa, x2, mod3,
                      g_post_mix[l], w_o, seq=seq)
        x2 = _mlp(x2, mod3, g_pre_mlp[l], g_post_mlp[l], w_u, w_d, seq=seq)
    return x2.reshape(bsz, seq, d)
```

```python
import functools
import math

import jax
import jax.numpy as jnp
from jax import lax
from jax.experimental import pallas as pl
from jax.experimental.pallas import tpu as pltpu

HEAD_DIM = 128
N_FOX_HEADS = 8
N_SWA_HEADS = 8
N_SWA_KV_HEADS = 2
SWA_GROUP = N_SWA_HEADS // N_SWA_KV_HEADS
SWA_WINDOW = 128
ROPE_THETA = 10000.0
NORM_EPS = 1e-6
N_MOD = 6
MASK_VALUE = -1e30
LOG2E = math.log2(math.e)
Q_SCALE = HEAD_DIM ** -0.5 * LOG2E

FOX_W = N_FOX_HEADS * HEAD_DIM
SWA_Q_W = N_SWA_HEADS * HEAD_DIM
SWA_KV_W = N_SWA_KV_HEADS * HEAD_DIM
TOK_W = FOX_W + SWA_KV_W
ROW_FQ = 0
ROW_FV = FOX_W
ROW_SQ = 2 * FOX_W
ROW_SV = ROW_SQ + SWA_Q_W
FEAT_W = ROW_SV + SWA_KV_W
CUM_SPLIT = 3
FG_ROWS = 32
LANES = 128

V7X_VMEM_LIMIT = 60 * 1024 * 1024

BF16 = jnp.bfloat16
F32 = jnp.float32
_NT = (((1,), (1,)), ((), ()))
_TN = (((0,), (0,)), ((), ()))


NORM_ROWS = 16


def _inv_rms(x):
    return lax.rsqrt(jnp.mean(x * x, axis=-1, keepdims=True) + NORM_EPS)


def _norm_modulate(x_ref, gain_row, shift_row, out_ref, row0=0, nrows=None):
    nrows = x_ref.shape[0] if nrows is None else nrows
    for c in range(nrows // NORM_ROWS):
        rows = slice(row0 + c * NORM_ROWS, row0 + (c + 1) * NORM_ROWS)
        x = x_ref[rows, :]
        out_ref[rows, :] = ((x * _inv_rms(x)) * gain_row + shift_row).astype(out_ref.dtype)


def _norm_gate_residual(y_ref, x_ref, gate_row, row0=0, nrows=None):
    nrows = x_ref.shape[0] if nrows is None else nrows
    for c in range(nrows // NORM_ROWS):
        rows = slice(row0 + c * NORM_ROWS, row0 + (c + 1) * NORM_ROWS)
        y = y_ref[rows, :]
        y_ref[rows, :] = x_ref[rows, :] + (y * _inv_rms(y)) * gate_row


def _resident(shape, index_map):
    return pl.BlockSpec(shape, index_map, pipeline_mode=pl.Buffered(1))


def _mod_kernel(c_ref, w_ref, b_ref, o_ref):
    k = pl.program_id(0)
    c = c_ref[...]
    cond = c * jax.nn.sigmoid(c)
    part = jnp.dot(cond.astype(BF16), w_ref[...].astype(BF16), preferred_element_type=F32)

    @pl.when(k == 0)
    def _():
        o_ref[...] = part + b_ref[...]

    @pl.when(k > 0)
    def _():
        o_ref[...] += part


def _mod(c, w, b, *, tk=128):
    bsz, d = c.shape
    n = w.shape[1]
    return pl.pallas_call(
        _mod_kernel,
        grid=(d // tk,),
        in_specs=[pl.BlockSpec((bsz, tk), lambda k: (0, k)),
                  pl.BlockSpec((tk, n), lambda k: (k, 0)),
                  pl.BlockSpec((1, n), lambda k: (0, 0))],
        out_specs=pl.BlockSpec((bsz, n), lambda k: (0, 0)),
        out_shape=jax.ShapeDtypeStruct((bsz, n), F32),
        compiler_params=pltpu.CompilerParams(
            dimension_semantics=("arbitrary",), vmem_limit_bytes=V7X_VMEM_LIMIT),
        name="mod",
    )(c, w, b.reshape(1, n))


def _inproj_kernel(x_ref, mod_ref, g_ref, wfk_ref, wfq_ref, wfv_ref, wsqk_ref, wsvfg_ref,
                   bfg_ref, phase_ref,
                   tri_ref, invf_ref, wo_ref,
                   tok_ref, aug_ref, feat_ref, wo_bf_ref,
                   h_scr, carry_scr, *, tiles_per_seq):
    i = pl.program_id(0)
    hd = HEAD_DIM
    half = hd // 2

    @pl.when(i % tiles_per_seq == 0)
    def _():
        carry_scr[...] = jnp.zeros_like(carry_scr)

    _norm_modulate(x_ref, g_ref[...] * (1.0 + mod_ref[0, 1:2, :]), mod_ref[0, 0:1, :], h_scr)

    half_rows = h_scr.shape[0] // 2
    for a in range(2):
        rows = slice(a * half_rows, (a + 1) * half_rows)
        tok_ref[rows, :FOX_W] = jnp.dot(h_scr[rows, :], wfk_ref[...],
                                        preferred_element_type=F32).astype(BF16)

    def feat(w_ref):
        return lax.dot_general(w_ref[...], h_scr[...], _NT, preferred_element_type=F32)

    svfg = feat(wsvfg_ref)
    feat_ref[ROW_SV:, :] = svfg[:SWA_KV_W].astype(BF16)
    z = svfg[SWA_KV_W:] + bfg_ref[...]
    logf = jnp.minimum(z, 0.0) - jnp.log1p(jnp.exp(-jnp.abs(z)))
    p0 = logf.astype(BF16)
    r0 = logf - p0.astype(F32)
    p1 = r0.astype(BF16)
    p2 = (r0 - p1.astype(F32)).astype(BF16)

    feat_ref[ROW_FQ:ROW_FQ + FOX_W, :] = (feat(wfq_ref) * Q_SCALE).astype(BF16)
    wo_bf_ref[...] = wo_ref[...].astype(BF16)

    cs = jnp.dot(jnp.concatenate([p0, p1, p2], axis=0), tri_ref[...],
                 preferred_element_type=F32)

    cum = cs[:FG_ROWS] + cs[FG_ROWS:2 * FG_ROWS] + cs[2 * FG_ROWS:] + carry_scr[...]
    carry_scr[...] = cum[:, cum.shape[1] - 1:]
    negc = cum * (-LOG2E)
    hi = negc.astype(BF16).astype(F32)
    rem = negc - hi
    mid = rem.astype(BF16).astype(F32)
    lo = rem - mid
    ph = phase_ref[...]
    pieces = jnp.where(ph == 0, hi, jnp.where(ph == 1, mid, lo))
    pad = jnp.zeros((LANES - FG_ROWS, pieces.shape[1]), F32)
    aug_ref[...] = jnp.concatenate([pieces, pad], axis=0).T.astype(BF16)

    sqk = feat(wsqk_ref)
    feat_ref[ROW_FV:ROW_FV + FOX_W, :] = feat(wfv_ref).astype(BF16)
    tm = h_scr.shape[0]
    pos = (i % tiles_per_seq) * tm + lax.broadcasted_iota(jnp.int32, (half, tm), 1)
    ang = pos.astype(F32) * invf_ref[...]
    cos_t, sin_t = jnp.cos(ang), jnp.sin(ang)
    for hh in range(N_SWA_HEADS + N_SWA_KV_HEADS):
        x1 = sqk[hh * hd:hh * hd + half, :]
        x2 = sqk[hh * hd + half:(hh + 1) * hd, :]
        lo_half = x1 * cos_t - x2 * sin_t
        hi_half = x2 * cos_t + x1 * sin_t
        if hh < N_SWA_HEADS:
            base = ROW_SQ + hh * hd
            feat_ref[base:base + half, :] = (lo_half * Q_SCALE).astype(BF16)
            feat_ref[base + half:base + hd, :] = (hi_half * Q_SCALE).astype(BF16)
        else:
            col = FOX_W + (hh - N_SWA_HEADS) * hd
            tok_ref[:, col:col + hd] = jnp.concatenate(
                [lo_half, hi_half], axis=0).T.astype(BF16)


def _inproj(x2, mod3, g, proj_weights, b_fg, phase, tri, inv_freq, later_weights,
            *, seq, tm):
    t, d = x2.shape
    steps = t // tm
    tiles_per_seq = seq // tm
    half = HEAD_DIM // 2
    kern = functools.partial(_inproj_kernel, tiles_per_seq=tiles_per_seq)
    const = lambda i: (0, 0)
    slab = lambda i: (i, 0)
    slab_specs = [pl.BlockSpec((w.shape[0] // steps, w.shape[1]), slab) for w in later_weights]
    return pl.pallas_call(
        kern,
        grid=(steps,),
        in_specs=[
            pl.BlockSpec((tm, d), lambda i: (i, 0)),
            pl.BlockSpec((1, N_MOD, d), lambda i: (i // tiles_per_seq, 0, 0)),
            _resident((1, d), const),
        ] + [_resident(w.shape, const) for w in proj_weights] + [
            _resident((FG_ROWS, 1), const),
            _resident((FG_ROWS, 1), const),
            _resident((tm, tm), const),
            _resident((half, 1), const),
        ] + slab_specs,
        out_specs=[
            pl.BlockSpec((tm, TOK_W), lambda i: (i, 0)),
            pl.BlockSpec((tm, LANES), lambda i: (i, 0)),
            pl.BlockSpec((FEAT_W, tm), lambda i: (0, i)),
        ] + slab_specs,
        out_shape=[jax.ShapeDtypeStruct((t, TOK_W), BF16),
                   jax.ShapeDtypeStruct((t, LANES), BF16),
                   jax.ShapeDtypeStruct((FEAT_W, t), BF16)]
        + [jax.ShapeDtypeStruct(w.shape, BF16) for w in later_weights],
        scratch_shapes=[pltpu.VMEM((tm, d), BF16), pltpu.VMEM((FG_ROWS, 1), F32)],
        compiler_params=pltpu.CompilerParams(
            dimension_semantics=("arbitrary",), vmem_limit_bytes=V7X_VMEM_LIMIT),
        name="in_proj",
    )(x2, mod3, g.reshape(1, d), *proj_weights, b_fg, phase, tri, inv_freq,
      *later_weights)


def _fox_kernel(qt_ref, k_ref, aug_ref, vt_ref, wu_ref, wd_ref, o_ref, wu_bf_ref, wd_bf_ref,
                *, blk, heads):
    hd = HEAD_DIM
    seq = k_ref.shape[1]
    nblk = seq // blk
    wu_bf_ref[...] = wu_ref[...].astype(BF16)
    wd_bf_ref[...] = wd_ref[...].astype(BF16)
    r = lax.broadcasted_iota(jnp.int32, (LANES, blk), 0)
    kk = lax.broadcasted_iota(jnp.int32, (blk, blk), 0)
    qq = lax.broadcasted_iota(jnp.int32, (blk, blk), 1)
    causal = kk <= qq
    onehot_t = []
    for hh in range(heads):
        hidx = pl.program_id(1) * heads + hh
        sel = (r >= CUM_SPLIT * hidx) & (r < CUM_SPLIT * (hidx + 1))
        onehot_t.append(jnp.where(sel, 1.0, 0.0).astype(BF16))

    def scores(hh, i):
        nk = (i + 1) * blk
        cols = slice(hh * hd, (hh + 1) * hd)
        q_aug = jnp.concatenate([qt_ref[cols, i * blk:nk], onehot_t[hh]], axis=0)
        k_aug = jnp.concatenate([k_ref[0, :nk, cols], aug_ref[0, :nk, :]], axis=1)
        s = jnp.dot(k_aug, q_aug, preferred_element_type=F32)
        s_diag = jnp.where(causal, s[nk - blk:, :], MASK_VALUE)
        s = s_diag if i == 0 else jnp.concatenate([s[:nk - blk, :], s_diag], axis=0)
        return s, jnp.max(s, axis=0, keepdims=True)

    s_next = [scores(hh, 0) for hh in range(heads)]
    for i in range(nblk):
        qs = slice(i * blk, (i + 1) * blk)
        nk = (i + 1) * blk
        s_cur = s_next
        if i + 1 < nblk:
            s_next = [scores(hh, i + 1) for hh in range(heads)]
        for hh in range(heads):
            cols = slice(hh * hd, (hh + 1) * hd)
            s, m = s_cur[hh]
            p = jnp.exp2(s - m)
            l = jnp.sum(p, axis=0, keepdims=True)
            acc = jnp.dot(vt_ref[cols, :nk], p.astype(BF16), preferred_element_type=F32)
            o_ref[cols, qs] = (acc * (1.0 / l)).astype(o_ref.dtype)


def _fox(feat_t, tok3, aug3, later_weights, *, blk=256, heads=4):
    bsz, seq, _ = tok3.shape
    hw = heads * HEAD_DIM
    hsteps = N_FOX_HEADS // heads
    qb, vb = ROW_FQ // hw, ROW_FV // hw
    kern = functools.partial(_fox_kernel, blk=blk, heads=heads)
    slab = lambda b, h: (b * hsteps + h, 0)
    slab_specs = [pl.BlockSpec((w.shape[0] // (bsz * hsteps), w.shape[1]), slab)
                  for w in later_weights]
    return pl.pallas_call(
        kern,
        grid=(bsz, hsteps),
        in_specs=[
            pl.BlockSpec((hw, seq), lambda b, h: (qb + h, b)),
            pl.BlockSpec((1, seq, hw), lambda b, h: (b, 0, h)),
            pl.BlockSpec((1, seq, LANES), lambda b, h: (b, 0, 0)),
            pl.BlockSpec((hw, seq), lambda b, h: (vb + h, b)),
        ] + slab_specs,
        out_specs=[pl.BlockSpec((hw, seq), lambda b, h: (h, b))] + slab_specs,
        out_shape=[jax.ShapeDtypeStruct((FOX_W, bsz * seq), BF16)]
        + [jax.ShapeDtypeStruct(w.shape, BF16) for w in later_weights],
        compiler_params=pltpu.CompilerParams(
            dimension_semantics=("arbitrary", "arbitrary"),
            vmem_limit_bytes=V7X_VMEM_LIMIT),
        name="fox",
    )(feat_t, tok3, aug3, feat_t, *later_weights)


def _swa_kernel(qt_ref, k_ref, vt_ref, sink_ref, o_ref):
    w, hd = SWA_WINDOW, HEAD_DIM
    seq = k_ref.shape[1]
    gl = SWA_GROUP * w
    gw = SWA_GROUP * hd
    groups = range(N_SWA_KV_HEADS)
    sinks = [jnp.concatenate(
        [jnp.full((1, w), sink_ref[g * SWA_GROUP + r] * LOG2E, F32) for r in range(SWA_GROUP)],
        axis=1) for g in groups]
    kk = lax.broadcasted_iota(jnp.int32, (w, gl), 0)
    qq = lax.broadcasted_iota(jnp.int32, (w, gl), 1) & (w - 1)
    from_prev = kk > qq

    def scores(g, n):
        q_t = jnp.concatenate(
            [qt_ref[g * gw + r * hd:g * gw + (r + 1) * hd, n * w:(n + 1) * w]
             for r in range(SWA_GROUP)], axis=1)
        ks = slice(max(n - 1, 0) * w, (n + 1) * w)
        s = jnp.dot(k_ref[0, ks, g * hd:(g + 1) * hd], q_t,
                    preferred_element_type=F32)
        packed = jnp.where(from_prev, s[:w, :] if n > 0 else MASK_VALUE, s[s.shape[0] - w:, :])
        return packed, jnp.maximum(jnp.max(packed, axis=0, keepdims=True), sinks[g])

    nblk = seq // w
    s_next = [scores(g, 0) for g in groups]
    for n in range(nblk):
        ks = slice(max(n - 1, 0) * w, (n + 1) * w)
        s_cur = s_next
        if n + 1 < nblk:
            s_next = [scores(g, n + 1) for g in groups]
        for g in groups:
            s, m = s_cur[g]
            p = jnp.exp2(s - m)
            denom = jnp.sum(p, axis=0, keepdims=True) + jnp.exp2(sinks[g] - m)
            p_own = jnp.where(from_prev, 0.0, p).astype(BF16)
            if n > 0:
                p_own = jnp.concatenate(
                    [jnp.where(from_prev, p, 0.0).astype(BF16), p_own], axis=0)
            o_t = jnp.dot(vt_ref[g * hd:(g + 1) * hd, ks], p_own, preferred_element_type=F32)
            o_t = o_t * (1.0 / denom)
            for r in range(SWA_GROUP):
                col = g * gw + r * hd
                o_ref[col:col + hd, n * w:(n + 1) * w] = (
                    o_t[:, r * w:(r + 1) * w].astype(o_ref.dtype))


def _swa(feat_t, tok3, sinks):
    bsz, seq, _ = tok3.shape
    qb, kb, vb = ROW_SQ // SWA_Q_W, FOX_W // SWA_KV_W, ROW_SV // SWA_KV_W
    return pl.pallas_call(
        _swa_kernel,
        grid=(bsz,),
        in_specs=[
            pl.BlockSpec((SWA_Q_W, seq), lambda b: (qb, b)),
            pl.BlockSpec((1, seq, SWA_KV_W), lambda b: (b, 0, kb)),
            pl.BlockSpec((SWA_KV_W, seq), lambda b: (vb, b)),
            pl.BlockSpec(memory_space=pltpu.SMEM),
        ],
        out_specs=pl.BlockSpec((SWA_Q_W, seq), lambda b: (0, b)),
        out_shape=jax.ShapeDtypeStruct((SWA_Q_W, bsz * seq), BF16),
        compiler_params=pltpu.CompilerParams(
            dimension_semantics=("arbitrary",), vmem_limit_bytes=V7X_VMEM_LIMIT),
        name="swa",
    )(feat_t, tok3, feat_t, sinks)


OUTPROJ_ROW_PARTS = 2


def _outproj_kernel(fox_ref, swa_ref, x_ref, mod_ref, g_ref, wf_ref, ws_ref, o_ref):
    part = x_ref.shape[0] // OUTPROJ_ROW_PARTS
    gate = mod_ref[0, 2:3, :] * g_ref[...]
    for a in range(OUTPROJ_ROW_PARTS):
        rows = slice(a * part, (a + 1) * part)
        o_ref[rows, :] = (
            lax.dot_general(fox_ref[:, rows], wf_ref[...], _TN, preferred_element_type=F32)
            + lax.dot_general(swa_ref[:, rows], ws_ref[...], _TN, preferred_element_type=F32))
    for a in range(OUTPROJ_ROW_PARTS):
        _norm_gate_residual(o_ref, x_ref, gate, a * part, part)


def _outproj(fox2, swa2, x2, mod3, g, w_out_bf, *, seq, tm=1024):
    t, d = x2.shape
    tiles_per_seq = seq // tm
    return pl.pallas_call(
        _outproj_kernel,
        grid=(t // tm,),
        in_specs=[
            pl.BlockSpec((FOX_W, tm), lambda i: (0, i)),
            pl.BlockSpec((SWA_Q_W, tm), lambda i: (0, i)),
            pl.BlockSpec((tm, d), lambda i: (i, 0)),
            pl.BlockSpec((1, N_MOD, d), lambda i: (i // tiles_per_seq, 0, 0)),
            _resident((1, d), lambda i: (0, 0)),
            _resident((FOX_W, d), lambda i: (0, 0)),
            _resident((SWA_Q_W, d), lambda i: (FOX_W // SWA_Q_W, 0)),
        ],
        out_specs=pl.BlockSpec((tm, d), lambda i: (i, 0)),
        out_shape=jax.ShapeDtypeStruct((t, d), F32),
        compiler_params=pltpu.CompilerParams(
            dimension_semantics=("arbitrary",), vmem_limit_bytes=V7X_VMEM_LIMIT),
        name="out_proj",
    )(fox2, swa2, x2, mod3, g.reshape(1, d), w_out_bf, w_out_bf)


MLP_ROW_PARTS = 2


def _mlp_kernel(x_ref, mod_ref, gpre_ref, gpost_ref, wu_ref, wd_ref, o_ref, h_scr):
    j = pl.program_id(1)
    last = pl.num_programs(1) - 1
    part = x_ref.shape[0] // MLP_ROW_PARTS

    def step(first, final):
        if first:
            gain = gpre_ref[...] * (1.0 + mod_ref[0, 4:5, :])
            shift = mod_ref[0, 3:4, :]
        if final:
            gate = mod_ref[0, 5:6, :] * gpost_ref[...]
        ups = []
        for a in range(MLP_ROW_PARTS):
            rows = slice(a * part, (a + 1) * part)
            if first:
                _norm_modulate(x_ref, gain, shift, h_scr, a * part, part)
            ups.append(jnp.dot(h_scr[rows, :], wu_ref[...], preferred_element_type=F32))
        for a in range(MLP_ROW_PARTS):
            rows = slice(a * part, (a + 1) * part)
            act = jnp.square(jnp.maximum(ups[a], 0.0)).astype(BF16)
            y = jnp.dot(act, wd_ref[...], preferred_element_type=F32)
            if first:
                o_ref[rows, :] = y
            else:
                o_ref[rows, :] += y
            if final:
                _norm_gate_residual(o_ref, x_ref, gate, a * part, part)

    pl.when(j == 0)(lambda: step(True, False))
    pl.when((j > 0) & (j < last))(lambda: step(False, False))
    pl.when(j == last)(lambda: step(False, True))


def _mlp(x2, mod3, g_pre, g_post, w_up, w_down, *, seq, tm=1024, tf=1024):
    t, d = x2.shape
    ff = w_up.shape[1]
    tiles_per_seq = seq // tm
    return pl.pallas_call(
        _mlp_kernel,
        grid=(t // tm, ff // tf),
        in_specs=[
            pl.BlockSpec((tm, d), lambda i, j: (i, 0)),
            pl.BlockSpec((1, N_MOD, d), lambda i, j: (i // tiles_per_seq, 0, 0)),
            pl.BlockSpec((1, d), lambda i, j: (0, 0)),
            pl.BlockSpec((1, d), lambda i, j: (0, 0)),
            pl.BlockSpec((d, tf), lambda i, j: (0, j)),
            pl.BlockSpec((tf, d), lambda i, j: (j, 0)),
        ],
        out_specs=pl.BlockSpec((tm, d), lambda i, j: (i, 0)),
        out_shape=jax.ShapeDtypeStruct((t, d), F32),
        scratch_shapes=[pltpu.VMEM((tm, d), BF16)],
        compiler_params=pltpu.CompilerParams(
            dimension_semantics=("arbitrary", "arbitrary"),
            vmem_limit_bytes=V7X_VMEM_LIMIT),
        name="mlp",
    )(x2, mod3, g_pre.reshape(1, d), g_post.reshape(1, d), w_up, w_down)


def _rope_inv_freq():
    half = HEAD_DIM // 2
    inv_freq = 1.0 / (ROPE_THETA ** (jnp.arange(half, dtype=F32) * (2.0 / HEAD_DIM)))
    return inv_freq.reshape(half, 1)


def kernel(x, c, w_mod, b_mod, g_pre_mix, g_post_mix, w_in, b_forget, swa_sinks,
           w_out, g_pre_mlp, g_post_mlp, w_up, w_down):
    bsz, seq, d = x.shape
    depth = w_mod.shape[0]
    t = bsz * seq
    tm_in = 512
    inv_freq = _rope_inv_freq()
    phase = (jnp.arange(FG_ROWS, dtype=jnp.int32) % CUM_SPLIT).reshape(FG_ROWS, 1)
    tri = jnp.triu(jnp.ones((tm_in, tm_in), BF16))
    o_fk, o_fv, o_fg = FOX_W, 2 * FOX_W, 3 * FOX_W
    o_sq = o_fg + N_FOX_HEADS
    o_sk = o_sq + SWA_Q_W
    o_sv = o_sk + SWA_KV_W
    n_rep = CUM_SPLIT * N_FOX_HEADS

    x2 = x.reshape(t, d)
    for l in range(depth):
        wi = w_in[l]
        w_fg = jnp.zeros((FG_ROWS, d), F32).at[:n_rep].set(
            jnp.repeat(wi[:, o_fg:o_sq].T, CUM_SPLIT, axis=0))
        proj_weights = (
            wi[:, o_fk:o_fv].astype(BF16),
            wi[:, :o_fk].T.astype(BF16), wi[:, o_fv:o_fg].T.astype(BF16),
            wi[:, o_sq:o_sv].T.astype(BF16),
            jnp.concatenate([wi[:, o_sv:].T, w_fg], axis=0).astype(BF16))
        b_fg = jnp.zeros((FG_ROWS, 1), F32).at[:n_rep, 0].set(
            jnp.repeat(b_forget[l].astype(F32), CUM_SPLIT))

        mod3 = _mod(c, w_mod[l], b_mod[l]).reshape(bsz, N_MOD, d)
        tok, aug, feat_t, w_o = _inproj(
            x2, mod3, g_pre_mix[l], proj_weights, b_fg, phase, tri, inv_freq,
            (w_out[l],), seq=seq, tm=tm_in)
        tok3 = tok.reshape(bsz, seq, TOK_W)
        fox, w_u, w_d = _fox(feat_t, tok3, aug.reshape(bsz, seq, LANES), (w_up[l], w_down[l]))
        swa = _swa(feat_t, tok3, swa_sinks[l])
        x2 = _outproj(fox, swa, x2, mod3,
                      g_post_mix[l], w_o, seq=seq)
        x2 = _mlp(x2, mod3, g_pre_mlp[l], g_post_mlp[l], w_u, w_d, seq=seq)
    return x2.reshape(bsz, seq, d)
```

```python
import functools
import math

import jax
import jax.numpy as jnp
from jax import lax
from jax.experimental import pallas as pl
from jax.experimental.pallas import tpu as pltpu

HEAD_DIM = 128
N_FOX_HEADS = 8
N_SWA_HEADS = 8
N_SWA_KV_HEADS = 2
SWA_GROUP = N_SWA_HEADS // N_SWA_KV_HEADS
SWA_WINDOW = 128
ROPE_THETA = 10000.0
NORM_EPS = 1e-6
N_MOD = 6
MASK_VALUE = -1e30
LOG2E = math.log2(math.e)
Q_SCALE = HEAD_DIM ** -0.5 * LOG2E

FOX_W = N_FOX_HEADS * HEAD_DIM
SWA_Q_W = N_SWA_HEADS * HEAD_DIM
SWA_KV_W = N_SWA_KV_HEADS * HEAD_DIM
TOK_W = FOX_W + SWA_KV_W
ROW_FQ = 0
ROW_FV = FOX_W
ROW_SQ = 2 * FOX_W
ROW_SV = ROW_SQ + SWA_Q_W
FEAT_W = ROW_SV + SWA_KV_W
CUM_SPLIT = 3
FG_ROWS = 32
LANES = 128

V7X_VMEM_LIMIT = 60 * 1024 * 1024

BF16 = jnp.bfloat16
F32 = jnp.float32
_NT = (((1,), (1,)), ((), ()))
_TN = (((0,), (0,)), ((), ()))


NORM_ROWS = 16


def _inv_rms(x):
    return lax.rsqrt(jnp.mean(x * x, axis=-1, keepdims=True) + NORM_EPS)


def _norm_modulate(x_ref, gain_row, shift_row, out_ref, row0=0, nrows=None):
    nrows = x_ref.shape[0] if nrows is None else nrows
    for c in range(nrows // NORM_ROWS):
        rows = slice(row0 + c * NORM_ROWS, row0 + (c + 1) * NORM_ROWS)
        x = x_ref[rows, :]
        out_ref[rows, :] = ((x * _inv_rms(x)) * gain_row + shift_row).astype(out_ref.dtype)


def _norm_gate_residual(y_ref, x_ref, gate_row, row0=0, nrows=None):
    nrows = x_ref.shape[0] if nrows is None else nrows
    for c in range(nrows // NORM_ROWS):
        rows = slice(row0 + c * NORM_ROWS, row0 + (c + 1) * NORM_ROWS)
        y = y_ref[rows, :]
        y_ref[rows, :] = x_ref[rows, :] + (y * _inv_rms(y)) * gate_row


def _streamed(shape, index_map):
    return pl.BlockSpec(shape, index_map, pipeline_mode=pl.Buffered(3))


def _resident(shape, index_map):
    return pl.BlockSpec(shape, index_map, pipeline_mode=pl.Buffered(1))


def _mod_kernel(c_ref, w_ref, b_ref, o_ref):
    k = pl.program_id(0)
    c = c_ref[...]
    cond = c * jax.nn.sigmoid(c)
    part = jnp.dot(cond.astype(BF16), w_ref[...].astype(BF16), preferred_element_type=F32)

    @pl.when(k == 0)
    def _():
        o_ref[...] = part + b_ref[...]

    @pl.when(k > 0)
    def _():
        o_ref[...] += part


def _mod(c, w, b, *, tk=256):
    bsz, d = c.shape
    n = w.shape[1]
    return pl.pallas_call(
        _mod_kernel,
        grid=(d // tk,),
        in_specs=[pl.BlockSpec((bsz, tk), lambda k: (0, k)),
                  pl.BlockSpec((tk, n), lambda k: (k, 0)),
                  pl.BlockSpec((1, n), lambda k: (0, 0))],
        out_specs=pl.BlockSpec((bsz, n), lambda k: (0, 0)),
        out_shape=jax.ShapeDtypeStruct((bsz, n), F32),
        compiler_params=pltpu.CompilerParams(
            dimension_semantics=("arbitrary",), vmem_limit_bytes=V7X_VMEM_LIMIT),
        name="mod",
    )(c, w, b.reshape(1, n))


def _inproj_kernel(x_ref, mod_ref, g_ref, wfk_ref, wfq_ref, wfv_ref, wsqk_ref, wsvfg_ref,
                   bfg_ref, phase_ref,
                   tri_ref, cost_ref, sint_ref, wo_ref,
                   tok_ref, aug_ref, feat_ref, wo_bf_ref,
                   h_scr, carry_scr, *, tiles_per_seq):
    i = pl.program_id(0)
    hd = HEAD_DIM
    half = hd // 2

    @pl.when(i % tiles_per_seq == 0)
    def _():
        carry_scr[...] = jnp.zeros_like(carry_scr)

    _norm_modulate(x_ref, g_ref[...] * (1.0 + mod_ref[0, 1:2, :]), mod_ref[0, 0:1, :], h_scr)

    half_rows = h_scr.shape[0] // 2
    for a in range(2):
        rows = slice(a * half_rows, (a + 1) * half_rows)
        tok_ref[rows, :FOX_W] = jnp.dot(h_scr[rows, :], wfk_ref[...],
                                        preferred_element_type=F32).astype(BF16)

    def feat(w_ref):
        return lax.dot_general(w_ref[...], h_scr[...], _NT, preferred_element_type=F32)

    svfg = feat(wsvfg_ref)
    feat_ref[ROW_SV:, :] = svfg[:SWA_KV_W].astype(BF16)
    z = svfg[SWA_KV_W:] + bfg_ref[...]
    logf = jnp.minimum(z, 0.0) - jnp.log1p(jnp.exp(-jnp.abs(z)))
    p0 = logf.astype(BF16)
    r0 = logf - p0.astype(F32)
    p1 = r0.astype(BF16)
    p2 = (r0 - p1.astype(F32)).astype(BF16)

    feat_ref[ROW_FQ:ROW_FQ + FOX_W, :] = (feat(wfq_ref) * Q_SCALE).astype(BF16)
    wo_bf_ref[...] = wo_ref[...].astype(BF16)

    cs = jnp.dot(jnp.concatenate([p0, p1, p2], axis=0), tri_ref[...],
                 preferred_element_type=F32)

    cum = cs[:FG_ROWS] + cs[FG_ROWS:2 * FG_ROWS] + cs[2 * FG_ROWS:] + carry_scr[...]
    carry_scr[...] = cum[:, cum.shape[1] - 1:]
    negc = cum * (-LOG2E)
    hi = negc.astype(BF16).astype(F32)
    rem = negc - hi
    mid = rem.astype(BF16).astype(F32)
    lo = rem - mid
    ph = phase_ref[...]
    pieces = jnp.where(ph == 0, hi, jnp.where(ph == 1, mid, lo))
    pad = jnp.zeros((LANES - FG_ROWS, pieces.shape[1]), F32)
    aug_ref[...] = jnp.concatenate([pieces, pad], axis=0).T.astype(BF16)

    sqk = feat(wsqk_ref)
    feat_ref[ROW_FV:ROW_FV + FOX_W, :] = feat(wfv_ref).astype(BF16)
    cos_t, sin_t = cost_ref[...], sint_ref[...]
    for hh in range(N_SWA_HEADS + N_SWA_KV_HEADS):
        x1 = sqk[hh * hd:hh * hd + half, :]
        x2 = sqk[hh * hd + half:(hh + 1) * hd, :]
        lo_half = x1 * cos_t - x2 * sin_t
        hi_half = x2 * cos_t + x1 * sin_t
        if hh < N_SWA_HEADS:
            base = ROW_SQ + hh * hd
            feat_ref[base:base + half, :] = (lo_half * Q_SCALE).astype(BF16)
            feat_ref[base + half:base + hd, :] = (hi_half * Q_SCALE).astype(BF16)
        else:
            col = FOX_W + (hh - N_SWA_HEADS) * hd
            tok_ref[:, col:col + hd] = jnp.concatenate(
                [lo_half, hi_half], axis=0).T.astype(BF16)


def _inproj(x2, mod3, g, proj_weights, b_fg, phase, tri, cos_t, sin_t, later_weights,
            *, seq, tm):
    t, d = x2.shape
    steps = t // tm
    tiles_per_seq = seq // tm
    half = HEAD_DIM // 2
    kern = functools.partial(_inproj_kernel, tiles_per_seq=tiles_per_seq)
    const = lambda i: (0, 0)
    pos_t = lambda i: (0, i % tiles_per_seq)
    slab = lambda i: (i, 0)
    slab_specs = [pl.BlockSpec((w.shape[0] // steps, w.shape[1]), slab) for w in later_weights]
    return pl.pallas_call(
        kern,
        grid=(steps,),
        in_specs=[
            pl.BlockSpec((tm, d), lambda i: (i, 0)),
            pl.BlockSpec((1, N_MOD, d), lambda i: (i // tiles_per_seq, 0, 0)),
            _resident((1, d), const),
        ] + [_resident(w.shape, const) for w in proj_weights] + [
            _resident((FG_ROWS, 1), const),
            _resident((FG_ROWS, 1), const),
            _resident((tm, tm), const),
            pl.BlockSpec((half, tm), pos_t),
            pl.BlockSpec((half, tm), pos_t),
        ] + slab_specs,
        out_specs=[
            pl.BlockSpec((tm, TOK_W), lambda i: (i, 0)),
            pl.BlockSpec((tm, LANES), lambda i: (i, 0)),
            pl.BlockSpec((FEAT_W, tm), lambda i: (0, i)),
        ] + slab_specs,
        out_shape=[jax.ShapeDtypeStruct((t, TOK_W), BF16),
                   jax.ShapeDtypeStruct((t, LANES), BF16),
                   jax.ShapeDtypeStruct((FEAT_W, t), BF16)]
        + [jax.ShapeDtypeStruct(w.shape, BF16) for w in later_weights],
        scratch_shapes=[pltpu.VMEM((tm, d), BF16), pltpu.VMEM((FG_ROWS, 1), F32)],
        compiler_params=pltpu.CompilerParams(
            dimension_semantics=("arbitrary",), vmem_limit_bytes=V7X_VMEM_LIMIT),
        name="in_proj",
    )(x2, mod3, g.reshape(1, d), *proj_weights, b_fg, phase, tri, cos_t, sin_t,
      *later_weights)


def _fox_kernel(qt_ref, k_ref, aug_ref, vt_ref, wu_ref, wd_ref, o_ref, wu_bf_ref, wd_bf_ref,
                *, blk, heads):
    hd = HEAD_DIM
    seq = k_ref.shape[1]
    nblk = seq // blk
    wu_bf_ref[...] = wu_ref[...].astype(BF16)
    wd_bf_ref[...] = wd_ref[...].astype(BF16)
    r = lax.broadcasted_iota(jnp.int32, (LANES, blk), 0)
    kk = lax.broadcasted_iota(jnp.int32, (blk, blk), 0)
    qq = lax.broadcasted_iota(jnp.int32, (blk, blk), 1)
    causal = kk <= qq
    onehot_t = []
    for hh in range(heads):
        hidx = pl.program_id(1) * heads + hh
        sel = (r >= CUM_SPLIT * hidx) & (r < CUM_SPLIT * (hidx + 1))
        onehot_t.append(jnp.where(sel, 1.0, 0.0).astype(BF16))

    def scores(hh, i):
        nk = (i + 1) * blk
        cols = slice(hh * hd, (hh + 1) * hd)
        q_aug = jnp.concatenate([qt_ref[cols, i * blk:nk], onehot_t[hh]], axis=0)
        k_aug = jnp.concatenate([k_ref[0, :nk, cols], aug_ref[0, :nk, :]], axis=1)
        s = jnp.dot(k_aug, q_aug, preferred_element_type=F32)
        s_diag = jnp.where(causal, s[nk - blk:, :], MASK_VALUE)
        s = s_diag if i == 0 else jnp.concatenate([s[:nk - blk, :], s_diag], axis=0)
        return s, jnp.max(s, axis=0, keepdims=True)

    s_next = [scores(hh, 0) for hh in range(heads)]
    for i in range(nblk):
        qs = slice(i * blk, (i + 1) * blk)
        nk = (i + 1) * blk
        s_cur = s_next
        if i + 1 < nblk:
            s_next = [scores(hh, i + 1) for hh in range(heads)]
        for hh in range(heads):
            cols = slice(hh * hd, (hh + 1) * hd)
            s, m = s_cur[hh]
            p = jnp.exp2(s - m)
            l = jnp.sum(p, axis=0, keepdims=True)
            acc = jnp.dot(vt_ref[cols, :nk], p.astype(BF16), preferred_element_type=F32)
            o_ref[cols, qs] = (acc * (1.0 / l)).astype(o_ref.dtype)


def _fox(feat_t, tok3, aug3, later_weights, *, blk=256, heads=4):
    bsz, seq, _ = tok3.shape
    hw = heads * HEAD_DIM
    hsteps = N_FOX_HEADS // heads
    qb, vb = ROW_FQ // hw, ROW_FV // hw
    kern = functools.partial(_fox_kernel, blk=blk, heads=heads)
    slab = lambda b, h: (b * hsteps + h, 0)
    slab_specs = [pl.BlockSpec((w.shape[0] // (bsz * hsteps), w.shape[1]), slab)
                  for w in later_weights]
    return pl.pallas_call(
        kern,
        grid=(bsz, hsteps),
        in_specs=[
            pl.BlockSpec((hw, seq), lambda b, h: (qb + h, b)),
            pl.BlockSpec((1, seq, hw), lambda b, h: (b, 0, h)),
            pl.BlockSpec((1, seq, LANES), lambda b, h: (b, 0, 0)),
            pl.BlockSpec((hw, seq), lambda b, h: (vb + h, b)),
        ] + slab_specs,
        out_specs=[pl.BlockSpec((hw, seq), lambda b, h: (h, b))] + slab_specs,
        out_shape=[jax.ShapeDtypeStruct((FOX_W, bsz * seq), BF16)]
        + [jax.ShapeDtypeStruct(w.shape, BF16) for w in later_weights],
        compiler_params=pltpu.CompilerParams(
            dimension_semantics=("arbitrary", "arbitrary"),
            vmem_limit_bytes=V7X_VMEM_LIMIT),
        name="fox",
    )(feat_t, tok3, aug3, feat_t, *later_weights)


def _swa_kernel(qt_ref, k_ref, vt_ref, sink_ref, o_ref):
    w, hd = SWA_WINDOW, HEAD_DIM
    seq = k_ref.shape[1]
    gl = SWA_GROUP * w
    gw = SWA_GROUP * hd
    groups = range(N_SWA_KV_HEADS)
    sinks = [jnp.concatenate(
        [jnp.full((1, w), sink_ref[g * SWA_GROUP + r] * LOG2E, F32) for r in range(SWA_GROUP)],
        axis=1) for g in groups]
    kk = lax.broadcasted_iota(jnp.int32, (w, gl), 0)
    qq = lax.broadcasted_iota(jnp.int32, (w, gl), 1) & (w - 1)
    from_prev = kk > qq

    def scores(g, n):
        q_t = jnp.concatenate(
            [qt_ref[g * gw + r * hd:g * gw + (r + 1) * hd, n * w:(n + 1) * w]
             for r in range(SWA_GROUP)], axis=1)
        ks = slice(max(n - 1, 0) * w, (n + 1) * w)
        s = jnp.dot(k_ref[0, ks, g * hd:(g + 1) * hd], q_t,
                    preferred_element_type=F32)
        packed = jnp.where(from_prev, s[:w, :] if n > 0 else MASK_VALUE, s[s.shape[0] - w:, :])
        return packed, jnp.maximum(jnp.max(packed, axis=0, keepdims=True), sinks[g])

    nblk = seq // w
    s_next = [scores(g, 0) for g in groups]
    for n in range(nblk):
        ks = slice(max(n - 1, 0) * w, (n + 1) * w)
        s_cur = s_next
        if n + 1 < nblk:
            s_next = [scores(g, n + 1) for g in groups]
        for g in groups:
            s, m = s_cur[g]
            p = jnp.exp2(s - m)
            denom = jnp.sum(p, axis=0, keepdims=True) + jnp.exp2(sinks[g] - m)
            p_own = jnp.where(from_prev, 0.0, p).astype(BF16)
            if n > 0:
                p_own = jnp.concatenate(
                    [jnp.where(from_prev, p, 0.0).astype(BF16), p_own], axis=0)
            o_t = jnp.dot(vt_ref[g * hd:(g + 1) * hd, ks], p_own, preferred_element_type=F32)
            o_t = o_t * (1.0 / denom)
            for r in range(SWA_GROUP):
                col = g * gw + r * hd
                o_ref[col:col + hd, n * w:(n + 1) * w] = (
                    o_t[:, r * w:(r + 1) * w].astype(o_ref.dtype))


def _swa(feat_t, tok3, sinks):
    bsz, seq, _ = tok3.shape
    qb, kb, vb = ROW_SQ // SWA_Q_W, FOX_W // SWA_KV_W, ROW_SV // SWA_KV_W
    def outer(qt_hbm, k_hbm, vt_hbm, sink_ref, o_hbm):
        def body(qt_ref, k_ref, vt_ref, o_ref):
            _swa_kernel(qt_ref, k_ref, vt_ref, sink_ref, o_ref)

        pltpu.emit_pipeline(
            body,
            grid=(bsz,),
            in_specs=[
                _streamed((SWA_Q_W, seq), lambda b: (qb, b)),
                _streamed((1, seq, SWA_KV_W), lambda b: (b, 0, kb)),
                _streamed((SWA_KV_W, seq), lambda b: (vb, b)),
            ],
            out_specs=[pl.BlockSpec((SWA_Q_W, seq), lambda b: (0, b))],
        )(qt_hbm, k_hbm, vt_hbm, o_hbm)

    any_spec = pl.BlockSpec(memory_space=pl.ANY)
    return pl.pallas_call(
        outer,
        in_specs=[any_spec, any_spec, any_spec, pl.BlockSpec(memory_space=pltpu.SMEM)],
        out_specs=any_spec,
        out_shape=jax.ShapeDtypeStruct((SWA_Q_W, bsz * seq), BF16),
        compiler_params=pltpu.CompilerParams(vmem_limit_bytes=V7X_VMEM_LIMIT),
        name="swa",
    )(feat_t, tok3, feat_t, sinks)


OUTPROJ_ROW_PARTS = 2


def _outproj_kernel(fox_ref, swa_ref, x_ref, mod_ref, g_ref, wf_ref, ws_ref, o_ref):
    part = x_ref.shape[0] // OUTPROJ_ROW_PARTS
    gate = mod_ref[0, 2:3, :] * g_ref[...]
    for a in range(OUTPROJ_ROW_PARTS):
        rows = slice(a * part, (a + 1) * part)
        o_ref[rows, :] = (
            lax.dot_general(fox_ref[:, rows], wf_ref[...], _TN, preferred_element_type=F32)
            + lax.dot_general(swa_ref[:, rows], ws_ref[...], _TN, preferred_element_type=F32))
    for a in range(OUTPROJ_ROW_PARTS):
        _norm_gate_residual(o_ref, x_ref, gate, a * part, part)


def _outproj(fox2, swa2, x2, mod3, g, w_out_bf, *, seq, tm=1024):
    t, d = x2.shape
    tiles_per_seq = seq // tm
    return pl.pallas_call(
        _outproj_kernel,
        grid=(t // tm,),
        in_specs=[
            pl.BlockSpec((FOX_W, tm), lambda i: (0, i)),
            pl.BlockSpec((SWA_Q_W, tm), lambda i: (0, i)),
            pl.BlockSpec((tm, d), lambda i: (i, 0)),
            pl.BlockSpec((1, N_MOD, d), lambda i: (i // tiles_per_seq, 0, 0)),
            _resident((1, d), lambda i: (0, 0)),
            _resident((FOX_W, d), lambda i: (0, 0)),
            _resident((SWA_Q_W, d), lambda i: (FOX_W // SWA_Q_W, 0)),
        ],
        out_specs=pl.BlockSpec((tm, d), lambda i: (i, 0)),
        out_shape=jax.ShapeDtypeStruct((t, d), F32),
        compiler_params=pltpu.CompilerParams(
            dimension_semantics=("arbitrary",), vmem_limit_bytes=V7X_VMEM_LIMIT),
        name="out_proj",
    )(fox2, swa2, x2, mod3, g.reshape(1, d), w_out_bf, w_out_bf)


MLP_ROW_PARTS = 2


def _mlp_kernel(x_ref, mod_ref, gpre_ref, gpost_ref, wu_ref, wd_ref, o_ref, h_scr):
    j = pl.program_id(1)
    last = pl.num_programs(1) - 1
    part = x_ref.shape[0] // MLP_ROW_PARTS

    def step(first, final):
        if first:
            gain = gpre_ref[...] * (1.0 + mod_ref[0, 4:5, :])
            shift = mod_ref[0, 3:4, :]
        if final:
            gate = mod_ref[0, 5:6, :] * gpost_ref[...]
        ups = []
        for a in range(MLP_ROW_PARTS):
            rows = slice(a * part, (a + 1) * part)
            if first:
                _norm_modulate(x_ref, gain, shift, h_scr, a * part, part)
            ups.append(jnp.dot(h_scr[rows, :], wu_ref[...], preferred_element_type=F32))
        for a in range(MLP_ROW_PARTS):
            rows = slice(a * part, (a + 1) * part)
            act = jnp.square(jnp.maximum(ups[a], 0.0)).astype(BF16)
            y = jnp.dot(act, wd_ref[...], preferred_element_type=F32)
            if first:
                o_ref[rows, :] = y
            else:
                o_ref[rows, :] += y
            if final:
                _norm_gate_residual(o_ref, x_ref, gate, a * part, part)

    pl.when(j == 0)(lambda: step(True, False))
    pl.when((j > 0) & (j < last))(lambda: step(False, False))
    pl.when(j == last)(lambda: step(False, True))


def _mlp(x2, mod3, g_pre, g_post, w_up, w_down, *, seq, tm=1024, tf=1024):
    t, d = x2.shape
    ff = w_up.shape[1]
    tiles_per_seq = seq // tm
    return pl.pallas_call(
        _mlp_kernel,
        grid=(t // tm, ff // tf),
        in_specs=[
            pl.BlockSpec((tm, d), lambda i, j: (i, 0)),
            pl.BlockSpec((1, N_MOD, d), lambda i, j: (i // tiles_per_seq, 0, 0)),
            pl.BlockSpec((1, d), lambda i, j: (0, 0)),
            pl.BlockSpec((1, d), lambda i, j: (0, 0)),
            pl.BlockSpec((d, tf), lambda i, j: (0, j)),
            pl.BlockSpec((tf, d), lambda i, j: (j, 0)),
        ],
        out_specs=pl.BlockSpec((tm, d), lambda i, j: (i, 0)),
        out_shape=jax.ShapeDtypeStruct((t, d), F32),
        scratch_shapes=[pltpu.VMEM((tm, d), BF16)],
        compiler_params=pltpu.CompilerParams(
            dimension_semantics=("arbitrary", "arbitrary"),
            vmem_limit_bytes=V7X_VMEM_LIMIT),
        name="mlp",
    )(x2, mod3, g_pre.reshape(1, d), g_post.reshape(1, d), w_up, w_down)


def _rope_tables(seq):
    half = HEAD_DIM // 2
    inv_freq = 1.0 / (ROPE_THETA ** (jnp.arange(half, dtype=F32) * (2.0 / HEAD_DIM)))
    ang = jnp.arange(seq).astype(F32)[:, None] * inv_freq[None, :]
    return jnp.cos(ang).T, jnp.sin(ang).T


def kernel(x, c, w_mod, b_mod, g_pre_mix, g_post_mix, w_in, b_forget, swa_sinks,
           w_out, g_pre_mlp, g_post_mlp, w_up, w_down):
    bsz, seq, d = x.shape
    depth = w_mod.shape[0]
    t = bsz * seq
    tm_in = 512
    cos_t, sin_t = _rope_tables(seq)
    phase = (jnp.arange(FG_ROWS, dtype=jnp.int32) % CUM_SPLIT).reshape(FG_ROWS, 1)
    tri = jnp.triu(jnp.ones((tm_in, tm_in), BF16))
    o_fk, o_fv, o_fg = FOX_W, 2 * FOX_W, 3 * FOX_W
    o_sq = o_fg + N_FOX_HEADS
    o_sk = o_sq + SWA_Q_W
    o_sv = o_sk + SWA_KV_W
    n_rep = CUM_SPLIT * N_FOX_HEADS

    x2 = x.reshape(t, d)
    for l in range(depth):
        wi = w_in[l]
        w_fg = jnp.zeros((FG_ROWS, d), F32).at[:n_rep].set(
            jnp.repeat(wi[:, o_fg:o_sq].T, CUM_SPLIT, axis=0))
        proj_weights = (
            wi[:, o_fk:o_fv].astype(BF16),
            wi[:, :o_fk].T.astype(BF16), wi[:, o_fv:o_fg].T.astype(BF16),
            wi[:, o_sq:o_sv].T.astype(BF16),
            jnp.concatenate([wi[:, o_sv:].T, w_fg], axis=0).astype(BF16))
        b_fg = jnp.zeros((FG_ROWS, 1), F32).at[:n_rep, 0].set(
            jnp.repeat(b_forget[l].astype(F32), CUM_SPLIT))

        mod3 = _mod(c, w_mod[l], b_mod[l]).reshape(bsz, N_MOD, d)
        tok, aug, feat_t, w_o = _inproj(
            x2, mod3, g_pre_mix[l], proj_weights, b_fg, phase, tri, cos_t, sin_t,
            (w_out[l],), seq=seq, tm=tm_in)
        tok3 = tok.reshape(bsz, seq, TOK_W)
        fox, w_u, w_d = _fox(feat_t, tok3, aug.reshape(bsz, seq, LANES), (w_up[l], w_down[l]))
        swa = _swa(feat_t, tok3, swa_sinks[l])
        x2 = _outproj(fox, swa, x2, mod3,
                      g_post_mix[l], w_o, seq=seq)
        x2 = _mlp(x2, mod3, g_pre_mlp[l], g_post_mlp[l], w_u, w_d, seq=seq)
    return x2.reshape(bsz, seq, d)
```

```python
import functools
import math

import jax
import jax.numpy as jnp
from jax import lax
from jax.experimental import pallas as pl
from jax.experimental.pallas import tpu as pltpu

HEAD_DIM = 128
N_FOX_HEADS = 8
N_SWA_HEADS = 8
N_SWA_KV_HEADS = 2
SWA_GROUP = N_SWA_HEADS // N_SWA_KV_HEADS
SWA_WINDOW = 128
ROPE_THETA = 10000.0
NORM_EPS = 1e-6
N_MOD = 6
MASK_VALUE = -1e30
LOG2E = math.log2(math.e)
Q_SCALE = HEAD_DIM ** -0.5 * LOG2E

FOX_W = N_FOX_HEADS * HEAD_DIM
SWA_Q_W = N_SWA_HEADS * HEAD_DIM
SWA_KV_W = N_SWA_KV_HEADS * HEAD_DIM
TOK_W = FOX_W + SWA_KV_W
ROW_FQ = 0
ROW_FV = FOX_W
ROW_SQ = 2 * FOX_W
ROW_SV = ROW_SQ + SWA_Q_W
FEAT_W = ROW_SV + SWA_KV_W
CUM_SPLIT = 3
FG_ROWS = 32
LANES = 128

V7X_VMEM_LIMIT = 60 * 1024 * 1024

BF16 = jnp.bfloat16
F32 = jnp.float32
_NT = (((1,), (1,)), ((), ()))
_TN = (((0,), (0,)), ((), ()))


NORM_ROWS = 16


def _inv_rms(x):
    return lax.rsqrt(jnp.mean(x * x, axis=-1, keepdims=True) + NORM_EPS)


def _norm_modulate(x_ref, gain_row, shift_row, out_ref, row0=0, nrows=None):
    nrows = x_ref.shape[0] if nrows is None else nrows
    for c in range(nrows // NORM_ROWS):
        rows = slice(row0 + c * NORM_ROWS, row0 + (c + 1) * NORM_ROWS)
        x = x_ref[rows, :]
        out_ref[rows, :] = ((x * _inv_rms(x)) * gain_row + shift_row).astype(out_ref.dtype)


def _norm_gate_residual(y_ref, x_ref, gate_row, row0=0, nrows=None):
    nrows = x_ref.shape[0] if nrows is None else nrows
    for c in range(nrows // NORM_ROWS):
        rows = slice(row0 + c * NORM_ROWS, row0 + (c + 1) * NORM_ROWS)
        y = y_ref[rows, :]
        y_ref[rows, :] = x_ref[rows, :] + (y * _inv_rms(y)) * gate_row


def _streamed(shape, index_map):
    return pl.BlockSpec(shape, index_map, pipeline_mode=pl.Buffered(3))


def _resident(shape, index_map):
    return pl.BlockSpec(shape, index_map, pipeline_mode=pl.Buffered(1))


def _mod_kernel(c_ref, w_ref, b_ref, o_ref):
    k = pl.program_id(0)
    c = c_ref[...]
    cond = c * jax.nn.sigmoid(c)
    part = jnp.dot(cond.astype(BF16), w_ref[...].astype(BF16), preferred_element_type=F32)

    @pl.when(k == 0)
    def _():
        o_ref[...] = part + b_ref[...]

    @pl.when(k > 0)
    def _():
        o_ref[...] += part


def _mod(c, w, b, *, tk=256):
    bsz, d = c.shape
    n = w.shape[1]
    return pl.pallas_call(
        _mod_kernel,
        grid=(d // tk,),
        in_specs=[pl.BlockSpec((bsz, tk), lambda k: (0, k)),
                  pl.BlockSpec((tk, n), lambda k: (k, 0)),
                  pl.BlockSpec((1, n), lambda k: (0, 0))],
        out_specs=pl.BlockSpec((bsz, n), lambda k: (0, 0)),
        out_shape=jax.ShapeDtypeStruct((bsz, n), F32),
        compiler_params=pltpu.CompilerParams(
            dimension_semantics=("arbitrary",), vmem_limit_bytes=V7X_VMEM_LIMIT),
        name="mod",
    )(c, w, b.reshape(1, n))


def _inproj_kernel(x_ref, mod_ref, g_ref, wfk_ref, wfq_ref, wfv_ref, wsqk_ref, wsvfg_ref,
                   bfg_ref, phase_ref,
                   tri_ref, cost_ref, sint_ref, wo_ref,
                   tok_ref, aug_ref, feat_ref, wo_bf_ref,
                   h_scr, carry_scr, *, tiles_per_seq):
    i = pl.program_id(0)
    hd = HEAD_DIM
    half = hd // 2

    @pl.when(i % tiles_per_seq == 0)
    def _():
        carry_scr[...] = jnp.zeros_like(carry_scr)

    _norm_modulate(x_ref, g_ref[...] * (1.0 + mod_ref[0, 1:2, :]), mod_ref[0, 0:1, :], h_scr)

    half_rows = h_scr.shape[0] // 2
    for a in range(2):
        rows = slice(a * half_rows, (a + 1) * half_rows)
        tok_ref[rows, :FOX_W] = jnp.dot(h_scr[rows, :], wfk_ref[...],
                                        preferred_element_type=F32).astype(BF16)

    def feat(w_ref):
        return lax.dot_general(w_ref[...], h_scr[...], _NT, preferred_element_type=F32)

    svfg = feat(wsvfg_ref)
    feat_ref[ROW_SV:, :] = svfg[:SWA_KV_W].astype(BF16)
    z = svfg[SWA_KV_W:] + bfg_ref[...]
    logf = jnp.minimum(z, 0.0) - jnp.log1p(jnp.exp(-jnp.abs(z)))
    p0 = logf.astype(BF16)
    r0 = logf - p0.astype(F32)
    p1 = r0.astype(BF16)
    p2 = (r0 - p1.astype(F32)).astype(BF16)

    feat_ref[ROW_FQ:ROW_FQ + FOX_W, :] = (feat(wfq_ref) * Q_SCALE).astype(BF16)
    wo_bf_ref[...] = wo_ref[...].astype(BF16)

    cs = jnp.dot(jnp.concatenate([p0, p1, p2], axis=0), tri_ref[...],
                 preferred_element_type=F32)

    cum = cs[:FG_ROWS] + cs[FG_ROWS:2 * FG_ROWS] + cs[2 * FG_ROWS:] + carry_scr[...]
    carry_scr[...] = cum[:, cum.shape[1] - 1:]
    negc = cum * (-LOG2E)
    hi = negc.astype(BF16).astype(F32)
    rem = negc - hi
    mid = rem.astype(BF16).astype(F32)
    lo = rem - mid
    ph = phase_ref[...]
    pieces = jnp.where(ph == 0, hi, jnp.where(ph == 1, mid, lo))
    pad = jnp.zeros((LANES - FG_ROWS, pieces.shape[1]), F32)
    aug_ref[...] = jnp.concatenate([pieces, pad], axis=0).T.astype(BF16)

    sqk = feat(wsqk_ref)
    feat_ref[ROW_FV:ROW_FV + FOX_W, :] = feat(wfv_ref).astype(BF16)
    cos_t, sin_t = cost_ref[...], sint_ref[...]
    for hh in range(N_SWA_HEADS + N_SWA_KV_HEADS):
        x1 = sqk[hh * hd:hh * hd + half, :]
        x2 = sqk[hh * hd + half:(hh + 1) * hd, :]
        lo_half = x1 * cos_t - x2 * sin_t
        hi_half = x2 * cos_t + x1 * sin_t
        if hh < N_SWA_HEADS:
            base = ROW_SQ + hh * hd
            feat_ref[base:base + half, :] = (lo_half * Q_SCALE).astype(BF16)
            feat_ref[base + half:base + hd, :] = (hi_half * Q_SCALE).astype(BF16)
        else:
            col = FOX_W + (hh - N_SWA_HEADS) * hd
            tok_ref[:, col:col + hd] = jnp.concatenate(
                [lo_half, hi_half], axis=0).T.astype(BF16)


def _inproj(x2, mod3, g, proj_weights, b_fg, phase, tri, cos_t, sin_t, later_weights,
            *, seq, tm):
    t, d = x2.shape
    steps = t // tm
    tiles_per_seq = seq // tm
    half = HEAD_DIM // 2
    kern = functools.partial(_inproj_kernel, tiles_per_seq=tiles_per_seq)
    const = lambda i: (0, 0)
    pos_t = lambda i: (0, i % tiles_per_seq)
    slab = lambda i: (i, 0)
    slab_specs = [pl.BlockSpec((w.shape[0] // steps, w.shape[1]), slab) for w in later_weights]
    return pl.pallas_call(
        kern,
        grid=(steps,),
        in_specs=[
            pl.BlockSpec((tm, d), lambda i: (i, 0)),
            pl.BlockSpec((1, N_MOD, d), lambda i: (i // tiles_per_seq, 0, 0)),
            _resident((1, d), const),
        ] + [_resident(w.shape, const) for w in proj_weights] + [
            _resident((FG_ROWS, 1), const),
            _resident((FG_ROWS, 1), const),
            _resident((tm, tm), const),
            pl.BlockSpec((half, tm), pos_t),
            pl.BlockSpec((half, tm), pos_t),
        ] + slab_specs,
        out_specs=[
            pl.BlockSpec((tm, TOK_W), lambda i: (i, 0)),
            pl.BlockSpec((tm, LANES), lambda i: (i, 0)),
            pl.BlockSpec((FEAT_W, tm), lambda i: (0, i)),
        ] + slab_specs,
        out_shape=[jax.ShapeDtypeStruct((t, TOK_W), BF16),
                   jax.ShapeDtypeStruct((t, LANES), BF16),
                   jax.ShapeDtypeStruct((FEAT_W, t), BF16)]
        + [jax.ShapeDtypeStruct(w.shape, BF16) for w in later_weights],
        scratch_shapes=[pltpu.VMEM((tm, d), BF16), pltpu.VMEM((FG_ROWS, 1), F32)],
        compiler_params=pltpu.CompilerParams(
            dimension_semantics=("arbitrary",), vmem_limit_bytes=V7X_VMEM_LIMIT),
        name="in_proj",
    )(x2, mod3, g.reshape(1, d), *proj_weights, b_fg, phase, tri, cos_t, sin_t,
      *later_weights)


def _fox_kernel(qt_ref, k_ref, aug_ref, vt_ref, oh_ref, wu_ref, wd_ref, o_ref, wu_bf_ref,
                wd_bf_ref, *, blk, heads):
    hd = HEAD_DIM
    seq = k_ref.shape[1]
    nblk = seq // blk
    wu_bf_ref[...] = wu_ref[...].astype(BF16)
    wd_bf_ref[...] = wd_ref[...].astype(BF16)
    kk = lax.broadcasted_iota(jnp.int32, (blk, blk), 0)
    qq = lax.broadcasted_iota(jnp.int32, (blk, blk), 1)
    causal = kk <= qq
    onehot_t = [oh_ref[0, hh * LANES:(hh + 1) * LANES, :] for hh in range(heads)]

    def scores(hh, i):
        nk = (i + 1) * blk
        cols = slice(hh * hd, (hh + 1) * hd)
        q_aug = jnp.concatenate([qt_ref[cols, i * blk:nk], onehot_t[hh]], axis=0)
        k_aug = jnp.concatenate([k_ref[0, :nk, cols], aug_ref[0, :nk, :]], axis=1)
        s = jnp.dot(k_aug, q_aug, preferred_element_type=F32)
        s_diag = jnp.where(causal, s[nk - blk:, :], MASK_VALUE)
        s = s_diag if i == 0 else jnp.concatenate([s[:nk - blk, :], s_diag], axis=0)
        return s, jnp.max(s, axis=0, keepdims=True)

    s_next = [scores(hh, 0) for hh in range(heads)]
    for i in range(nblk):
        qs = slice(i * blk, (i + 1) * blk)
        nk = (i + 1) * blk
        s_cur = s_next
        if i + 1 < nblk:
            s_next = [scores(hh, i + 1) for hh in range(heads)]
        for hh in range(heads):
            cols = slice(hh * hd, (hh + 1) * hd)
            s, m = s_cur[hh]
            p = jnp.exp2(s - m)
            l = jnp.sum(p, axis=0, keepdims=True)
            acc = jnp.dot(vt_ref[cols, :nk], p.astype(BF16), preferred_element_type=F32)
            o_ref[cols, qs] = (acc * (1.0 / l)).astype(o_ref.dtype)


def _fox(feat_t, tok3, aug3, later_weights, *, blk=256, heads=4):
    bsz, seq, _ = tok3.shape
    hw = heads * HEAD_DIM
    hsteps = N_FOX_HEADS // heads
    qb, vb = ROW_FQ // hw, ROW_FV // hw
    kern = functools.partial(_fox_kernel, blk=blk, heads=heads)
    slab = lambda b, h: (b * hsteps + h, 0)
    slab_specs = [pl.BlockSpec((w.shape[0] // (bsz * hsteps), w.shape[1]), slab)
                  for w in later_weights]
    lane = jnp.arange(heads * LANES, dtype=jnp.int32)[None, :, None]
    head = (jnp.arange(hsteps, dtype=jnp.int32)[:, None, None] * heads + lane // LANES)
    onehots = jnp.broadcast_to(((lane % LANES) // CUM_SPLIT == head).astype(BF16),
                               (hsteps, heads * LANES, blk))
    n_in = 5 + len(later_weights)

    def outer(*refs):
        pltpu.emit_pipeline(
            kern,
            grid=(bsz, hsteps),
            in_specs=[
                pl.BlockSpec((hw, seq), lambda b, h: (qb + h, b)),
                pl.BlockSpec((1, seq, hw), lambda b, h: (b, 0, h)),
                pl.BlockSpec((1, seq, LANES), lambda b, h: (b, 0, 0)),
                pl.BlockSpec((hw, seq), lambda b, h: (vb + h, b)),
                pl.BlockSpec((1, heads * LANES, blk), lambda b, h: (h, 0, 0)),
            ] + slab_specs,
            out_specs=[pl.BlockSpec((hw, seq), lambda b, h: (h, b))] + slab_specs,
        )(*refs)

    any_spec = pl.BlockSpec(memory_space=pl.ANY)
    return pl.pallas_call(
        outer,
        in_specs=[any_spec] * n_in,
        out_specs=[any_spec] * (1 + len(later_weights)),
        out_shape=[jax.ShapeDtypeStruct((FOX_W, bsz * seq), BF16)]
        + [jax.ShapeDtypeStruct(w.shape, BF16) for w in later_weights],
        compiler_params=pltpu.CompilerParams(vmem_limit_bytes=V7X_VMEM_LIMIT),
        name="fox",
    )(feat_t, tok3, aug3, feat_t, onehots, *later_weights)


def _swa_kernel(qt_ref, k_ref, vt_ref, sink_ref, o_ref):
    w, hd = SWA_WINDOW, HEAD_DIM
    seq = k_ref.shape[1]
    gl = SWA_GROUP * w
    gw = SWA_GROUP * hd
    groups = range(N_SWA_KV_HEADS)
    sinks = [jnp.concatenate(
        [jnp.full((1, w), sink_ref[g * SWA_GROUP + r] * LOG2E, F32) for r in range(SWA_GROUP)],
        axis=1) for g in groups]
    kk = lax.broadcasted_iota(jnp.int32, (w, gl), 0)
    qq = lax.broadcasted_iota(jnp.int32, (w, gl), 1) & (w - 1)
    from_prev = kk > qq

    def scores(g, n):
        q_t = jnp.concatenate(
            [qt_ref[g * gw + r * hd:g * gw + (r + 1) * hd, n * w:(n + 1) * w]
             for r in range(SWA_GROUP)], axis=1)
        ks = slice(max(n - 1, 0) * w, (n + 1) * w)
        s = jnp.dot(k_ref[0, ks, g * hd:(g + 1) * hd], q_t,
                    preferred_element_type=F32)
        packed = jnp.where(from_prev, s[:w, :] if n > 0 else MASK_VALUE, s[s.shape[0] - w:, :])
        return packed, jnp.maximum(jnp.max(packed, axis=0, keepdims=True), sinks[g])

    nblk = seq // w
    s_next = [scores(g, 0) for g in groups]
    for n in range(nblk):
        ks = slice(max(n - 1, 0) * w, (n + 1) * w)
        s_cur = s_next
        if n + 1 < nblk:
            s_next = [scores(g, n + 1) for g in groups]
        for g in groups:
            s, m = s_cur[g]
            p = jnp.exp2(s - m)
            denom = jnp.sum(p, axis=0, keepdims=True) + jnp.exp2(sinks[g] - m)
            p_own = jnp.where(from_prev, 0.0, p).astype(BF16)
            if n > 0:
                p_own = jnp.concatenate(
                    [jnp.where(from_prev, p, 0.0).astype(BF16), p_own], axis=0)
            o_t = jnp.dot(vt_ref[g * hd:(g + 1) * hd, ks], p_own, preferred_element_type=F32)
            o_t = o_t * (1.0 / denom)
            for r in range(SWA_GROUP):
                col = g * gw + r * hd
                o_ref[col:col + hd, n * w:(n + 1) * w] = (
                    o_t[:, r * w:(r + 1) * w].astype(o_ref.dtype))


def _swa(feat_t, tok3, sinks):
    bsz, seq, _ = tok3.shape
    qb, kb, vb = ROW_SQ // SWA_Q_W, FOX_W // SWA_KV_W, ROW_SV // SWA_KV_W
    def outer(qt_hbm, k_hbm, vt_hbm, sink_ref, o_hbm):
        def body(qt_ref, k_ref, vt_ref, o_ref):
            _swa_kernel(qt_ref, k_ref, vt_ref, sink_ref, o_ref)

        pltpu.emit_pipeline(
            body,
            grid=(bsz,),
            in_specs=[
                _streamed((SWA_Q_W, seq), lambda b: (qb, b)),
                _streamed((1, seq, SWA_KV_W), lambda b: (b, 0, kb)),
                _streamed((SWA_KV_W, seq), lambda b: (vb, b)),
            ],
            out_specs=[pl.BlockSpec((SWA_Q_W, seq), lambda b: (0, b))],
        )(qt_hbm, k_hbm, vt_hbm, o_hbm)

    any_spec = pl.BlockSpec(memory_space=pl.ANY)
    return pl.pallas_call(
        outer,
        in_specs=[any_spec, any_spec, any_spec, pl.BlockSpec(memory_space=pltpu.SMEM)],
        out_specs=any_spec,
        out_shape=jax.ShapeDtypeStruct((SWA_Q_W, bsz * seq), BF16),
        compiler_params=pltpu.CompilerParams(vmem_limit_bytes=V7X_VMEM_LIMIT),
        name="swa",
    )(feat_t, tok3, feat_t, sinks)


OUTPROJ_ROW_PARTS = 2


def _outproj_kernel(fox_ref, swa_ref, x_ref, mod_ref, g_ref, wf_ref, ws_ref, o_ref):
    part = x_ref.shape[0] // OUTPROJ_ROW_PARTS
    gate = mod_ref[0, 2:3, :] * g_ref[...]
    for a in range(OUTPROJ_ROW_PARTS):
        rows = slice(a * part, (a + 1) * part)
        o_ref[rows, :] = (
            lax.dot_general(fox_ref[:, rows], wf_ref[...], _TN, preferred_element_type=F32)
            + lax.dot_general(swa_ref[:, rows], ws_ref[...], _TN, preferred_element_type=F32))
    for a in range(OUTPROJ_ROW_PARTS):
        _norm_gate_residual(o_ref, x_ref, gate, a * part, part)


def _outproj(fox2, swa2, x2, mod3, g, w_out_bf, *, seq, tm=1024):
    t, d = x2.shape
    tiles_per_seq = seq // tm
    return pl.pallas_call(
        _outproj_kernel,
        grid=(t // tm,),
        in_specs=[
            pl.BlockSpec((FOX_W, tm), lambda i: (0, i)),
            pl.BlockSpec((SWA_Q_W, tm), lambda i: (0, i)),
            pl.BlockSpec((tm, d), lambda i: (i, 0)),
            pl.BlockSpec((1, N_MOD, d), lambda i: (i // tiles_per_seq, 0, 0)),
            _resident((1, d), lambda i: (0, 0)),
            _resident((FOX_W, d), lambda i: (0, 0)),
            _resident((SWA_Q_W, d), lambda i: (FOX_W // SWA_Q_W, 0)),
        ],
        out_specs=pl.BlockSpec((tm, d), lambda i: (i, 0)),
        out_shape=jax.ShapeDtypeStruct((t, d), F32),
        compiler_params=pltpu.CompilerParams(
            dimension_semantics=("arbitrary",), vmem_limit_bytes=V7X_VMEM_LIMIT),
        name="out_proj",
    )(fox2, swa2, x2, mod3, g.reshape(1, d), w_out_bf, w_out_bf)


MLP_ROW_PARTS = 2


def _mlp_kernel(x_ref, mod_ref, gpre_ref, gpost_ref, wu_ref, wd_ref, o_ref, h_scr):
    j = pl.program_id(1)
    last = pl.num_programs(1) - 1
    part = x_ref.shape[0] // MLP_ROW_PARTS

    def step(first, final):
        if first:
            gain = gpre_ref[...] * (1.0 + mod_ref[0, 4:5, :])
            shift = mod_ref[0, 3:4, :]
        if final:
            gate = mod_ref[0, 5:6, :] * gpost_ref[...]
        ups = []
        for a in range(MLP_ROW_PARTS):
            rows = slice(a * part, (a + 1) * part)
            if first:
                _norm_modulate(x_ref, gain, shift, h_scr, a * part, part)
            ups.append(jnp.dot(h_scr[rows, :], wu_ref[...], preferred_element_type=F32))
        for a in range(MLP_ROW_PARTS):
            rows = slice(a * part, (a + 1) * part)
            act = jnp.square(jnp.maximum(ups[a], 0.0)).astype(BF16)
            y = jnp.dot(act, wd_ref[...], preferred_element_type=F32)
            if first:
                o_ref[rows, :] = y
            else:
                o_ref[rows, :] += y
            if final:
                _norm_gate_residual(o_ref, x_ref, gate, a * part, part)

    pl.when(j == 0)(lambda: step(True, False))
    pl.when((j > 0) & (j < last))(lambda: step(False, False))
    pl.when(j == last)(lambda: step(False, True))


def _mlp(x2, mod3, g_pre, g_post, w_up, w_down, *, seq, tm=1024, tf=1024):
    t, d = x2.shape
    ff = w_up.shape[1]
    tiles_per_seq = seq // tm
    return pl.pallas_call(
        _mlp_kernel,
        grid=(t // tm, ff // tf),
        in_specs=[
            pl.BlockSpec((tm, d), lambda i, j: (i, 0)),
            pl.BlockSpec((1, N_MOD, d), lambda i, j: (i // tiles_per_seq, 0, 0)),
            pl.BlockSpec((1, d), lambda i, j: (0, 0)),
            pl.BlockSpec((1, d), lambda i, j: (0, 0)),
            pl.BlockSpec((d, tf), lambda i, j: (0, j)),
            pl.BlockSpec((tf, d), lambda i, j: (j, 0)),
        ],
        out_specs=pl.BlockSpec((tm, d), lambda i, j: (i, 0)),
        out_shape=jax.ShapeDtypeStruct((t, d), F32),
        scratch_shapes=[pltpu.VMEM((tm, d), BF16)],
        compiler_params=pltpu.CompilerParams(
            dimension_semantics=("arbitrary", "arbitrary"),
            vmem_limit_bytes=V7X_VMEM_LIMIT),
        name="mlp",
    )(x2, mod3, g_pre.reshape(1, d), g_post.reshape(1, d), w_up, w_down)


def _rope_tables(seq):
    half = HEAD_DIM // 2
    inv_freq = 1.0 / (ROPE_THETA ** (jnp.arange(half, dtype=F32) * (2.0 / HEAD_DIM)))
    ang = jnp.arange(seq).astype(F32)[:, None] * inv_freq[None, :]
    return jnp.cos(ang).T, jnp.sin(ang).T


def kernel(x, c, w_mod, b_mod, g_pre_mix, g_post_mix, w_in, b_forget, swa_sinks,
           w_out, g_pre_mlp, g_post_mlp, w_up, w_down):
    bsz, seq, d = x.shape
    depth = w_mod.shape[0]
    t = bsz * seq
    tm_in = 512
    cos_t, sin_t = _rope_tables(seq)
    phase = (jnp.arange(FG_ROWS, dtype=jnp.int32) % CUM_SPLIT).reshape(FG_ROWS, 1)
    tri = jnp.triu(jnp.ones((tm_in, tm_in), BF16))
    o_fk, o_fv, o_fg = FOX_W, 2 * FOX_W, 3 * FOX_W
    o_sq = o_fg + N_FOX_HEADS
    o_sk = o_sq + SWA_Q_W
    o_sv = o_sk + SWA_KV_W
    n_rep = CUM_SPLIT * N_FOX_HEADS

    x2 = x.reshape(t, d)
    for l in range(depth):
        wi = w_in[l]
        w_fg = jnp.zeros((FG_ROWS, d), F32).at[:n_rep].set(
            jnp.repeat(wi[:, o_fg:o_sq].T, CUM_SPLIT, axis=0))
        proj_weights = (
            wi[:, o_fk:o_fv].astype(BF16),
            wi[:, :o_fk].T.astype(BF16), wi[:, o_fv:o_fg].T.astype(BF16),
            wi[:, o_sq:o_sv].T.astype(BF16),
            jnp.concatenate([wi[:, o_sv:].T, w_fg], axis=0).astype(BF16))
        b_fg = jnp.zeros((FG_ROWS, 1), F32).at[:n_rep, 0].set(
            jnp.repeat(b_forget[l].astype(F32), CUM_SPLIT))

        mod3 = _mod(c, w_mod[l], b_mod[l]).reshape(bsz, N_MOD, d)
        tok, aug, feat_t, w_o = _inproj(
            x2, mod3, g_pre_mix[l], proj_weights, b_fg, phase, tri, cos_t, sin_t,
            (w_out[l],), seq=seq, tm=tm_in)
        tok3 = tok.reshape(bsz, seq, TOK_W)
        fox, w_u, w_d = _fox(feat_t, tok3, aug.reshape(bsz, seq, LANES), (w_up[l], w_down[l]))
        swa = _swa(feat_t, tok3, swa_sinks[l])
        x2 = _outproj(fox, swa, x2, mod3,
                      g_post_mix[l], w_o, seq=seq)
        x2 = _mlp(x2, mod3, g_pre_mlp[l], g_post_mlp[l], w_u, w_d, seq=seq)
    return x2.reshape(bsz, seq, d)
```
